```python
import jax, jax.numpy as jnp
from jax import lax
import numpy as np

D_MODEL = 1024
BATCH = 8
SEQ = 4096
DEPTH = 4

GRID_W = 64
CTX_LEN = 256
EPS = 1e-6

CHUNK = 128
A_GROUPS = 4
A_GROUP_DIM = 128
A_WIDTH = A_GROUPS * A_GROUP_DIM
HEAD_DIM = 64
B_Q_HEADS = 8
B_KV_HEADS = 2
B_GQA = B_Q_HEADS // B_KV_HEADS
B_WIDTH = B_Q_HEADS * HEAD_DIM
KV_WIDTH = B_KV_HEADS * HEAD_DIM
WINDOW = 128
BLOCK = 128
ROPE_THETA = 10000.0
AB_SIZES = (A_WIDTH, A_WIDTH, A_WIDTH, B_WIDTH, KV_WIDTH, KV_WIDTH, B_WIDTH)
AB_IN = 3 * A_WIDTH + 2 * B_WIDTH + 2 * KV_WIDTH
AB_MIX = A_WIDTH + B_WIDTH
AB_K_OFF = 3 * A_WIDTH + B_WIDTH

C_WIDTH = D_MODEL
C_HEADS = 4
C_BLOCK = C_WIDTH // C_HEADS
CONV_W = 4
CONV_LEFT = 2
LRU_C = 8.0

N_EVEN = (DEPTH + 1) // 2
N_ODD = DEPTH // 2

kernel_name = "hybrid_gmlp_swa_rglru_prefix_dit"


def _split(p, sizes):
    outs, off = [], 0
    for s in sizes:
        outs.append(p[..., off:off + s])
        off += s
    return outs


def rmsnorm(x, g):
    xf = x.astype(jnp.float32)
    y = xf * lax.rsqrt(jnp.mean(xf * xf, axis=-1, keepdims=True) + EPS)
    return (y * g).astype(x.dtype)


def group_layernorm(v, g, b):
    vf = v.astype(jnp.float32)
    mu = jnp.mean(vf, axis=-1, keepdims=True)
    var = jnp.mean(jnp.square(vf - mu), axis=-1, keepdims=True)
    return ((vf - mu) * lax.rsqrt(var + EPS) * g + b).astype(v.dtype)


def axial_rope(L):
    rows = L // GRID_W
    r, col = jnp.meshgrid(jnp.arange(rows), jnp.arange(GRID_W), indexing="ij")
    r = r.reshape(-1).astype(jnp.float32)
    col = col.reshape(-1).astype(jnp.float32)
    n_freq = HEAD_DIM // 4
    inv_freq = ROPE_THETA ** (-jnp.arange(n_freq, dtype=jnp.float32) / n_freq)
    ang = jnp.concatenate([r[:, None] * inv_freq, col[:, None] * inv_freq], axis=-1)
    return jnp.cos(ang), jnp.sin(ang)


def apply_rope(x, cos, sin):
    half = HEAD_DIM // 2
    x1, x2 = x[..., :half], x[..., half:]
    c, s = cos[None, :, None, :], sin[None, :, None, :]
    return jnp.concatenate([x1 * c - x2 * s, x2 * c + x1 * s], axis=-1).astype(x.dtype)


def chunk_gmlp(u, v, ln_g, ln_b, w_s, b_s):
    bsz, L, _ = u.shape
    n = L // CHUNK
    v = group_layernorm(v.reshape(bsz, L, A_GROUPS, A_GROUP_DIM),
                        ln_g.reshape(A_GROUPS, A_GROUP_DIM), ln_b.reshape(A_GROUPS, A_GROUP_DIM))
    v = v.reshape(bsz, n, CHUNK, A_GROUPS, A_GROUP_DIM)
    sv = jnp.einsum("gpq,bnqgd->bnpgd", w_s, v) + b_s.T[None, None, :, :, None]
    return u * sv.reshape(bsz, L, A_WIDTH)


def _sink_logits(sink, bsz, nq):
    s = sink.reshape(B_KV_HEADS, B_GQA).astype(jnp.float32)[None, :, :, None, None]
    return jnp.broadcast_to(s, (bsz, B_KV_HEADS, B_GQA, nq, 1))


def window_attention(q, k, v, k_c, v_c, sink):
    bsz, L, _, _ = q.shape
    Lc = k_c.shape[1]
    n = L // BLOCK
    q = q.reshape(bsz, L, B_KV_HEADS, B_GQA, HEAD_DIM) * (HEAD_DIM ** -0.5)
    pad = ((0, 0), (BLOCK, BLOCK), (0, 0), (0, 0))
    k_pad = jnp.pad(k, pad)
    v_pad = jnp.pad(v, pad)
    s_sink = _sink_logits(sink, bsz, BLOCK)

    def block(j):
        start = j * BLOCK
        qb = lax.dynamic_slice_in_dim(q, start, BLOCK, axis=1)
        kb = lax.dynamic_slice_in_dim(k_pad, start, 3 * BLOCK, axis=1)
        vb = lax.dynamic_slice_in_dim(v_pad, start, 3 * BLOCK, axis=1)
        s_loc = jnp.einsum("bqkgd,bskd->bkgqs", qb, kb).astype(jnp.float32)
        qpos = start + jnp.arange(BLOCK)
        kpos = start - BLOCK + jnp.arange(3 * BLOCK)
        valid = (jnp.abs(qpos[:, None] - kpos[None, :]) <= WINDOW) & (kpos[None, :] >= 0) & (kpos[None, :] < L)
        s_loc = jnp.where(valid, s_loc, -1e30)
        s_ctx = jnp.einsum("bqkgd,bskd->bkgqs", qb, k_c).astype(jnp.float32)
        p = jax.nn.softmax(jnp.concatenate([s_sink, s_ctx, s_loc], axis=-1), axis=-1)
        p_ctx = p[..., 1:1 + Lc].astype(v.dtype)
        p_loc = p[..., 1 + Lc:].astype(v.dtype)
        return (jnp.einsum("bkgqs,bskd->bqkgd", p_ctx, v_c)
                + jnp.einsum("bkgqs,bskd->bqkgd", p_loc, vb))

    out = lax.map(block, jnp.arange(n))
    return out.transpose(1, 0, 2, 3, 4, 5).reshape(bsz, L, B_WIDTH)


def context_attention(q_c, k_c, v_c, sink):
    bsz, Lc, _, _ = q_c.shape
    q_c = q_c.reshape(bsz, Lc, B_KV_HEADS, B_GQA, HEAD_DIM) * (HEAD_DIM ** -0.5)
    s = jnp.einsum("bqkgd,bskd->bkgqs", q_c, k_c).astype(jnp.float32)
    p = jax.nn.softmax(jnp.concatenate([_sink_logits(sink, bsz, Lc), s], axis=-1), axis=-1)
    out = jnp.einsum("bkgqs,bskd->bqkgd", p[..., 1:].astype(v_c.dtype), v_c)
    return out.reshape(bsz, Lc, B_WIDTH)


def even_mixer(h, hc, w_in, ln_g, ln_b, w_s, b_s, sink, w_out, cos, sin, need_ctx):
    bsz, L, _ = h.shape
    Lc = hc.shape[1]
    u, v, ga, q, k, vv, gb = _split(h @ w_in, AB_SIZES)
    ya = chunk_gmlp(jax.nn.gelu(u), jax.nn.gelu(v), ln_g, ln_b, w_s, b_s) * jax.nn.silu(ga)
    if need_ctx:
        uc, vc, gac, qc, kc, vvc, gbc = _split(hc @ w_in, AB_SIZES)
    else:
        kc, vvc = _split(hc @ w_in[:, AB_K_OFF:AB_K_OFF + 2 * KV_WIDTH], (KV_WIDTH, KV_WIDTH))
    kc = kc.reshape(bsz, Lc, B_KV_HEADS, HEAD_DIM)
    vvc = vvc.reshape(bsz, Lc, B_KV_HEADS, HEAD_DIM)
    q = apply_rope(q.reshape(bsz, L, B_Q_HEADS, HEAD_DIM), cos, sin)
    k = apply_rope(k.reshape(bsz, L, B_KV_HEADS, HEAD_DIM), cos, sin)
    vv = vv.reshape(bsz, L, B_KV_HEADS, HEAD_DIM)
    yb = window_attention(q, k, vv, kc, vvc, sink) * jax.nn.silu(gb)
    y = jnp.concatenate([ya, yb], axis=-1) @ w_out
    if not need_ctx:
        return y, None
    yac = chunk_gmlp(jax.nn.gelu(uc), jax.nn.gelu(vc), ln_g, ln_b, w_s, b_s) * jax.nn.silu(gac)
    ybc = context_attention(qc.reshape(bsz, Lc, B_Q_HEADS, HEAD_DIM), kc, vvc, sink) * jax.nn.silu(gbc)
    yc = jnp.concatenate([yac, ybc], axis=-1) @ w_out
    return y, yc


def depthwise_conv(z, w, b):
    ch = z.shape[-1]
    out = lax.conv_general_dilated(z, w[:, None, :], window_strides=(1,),
                                   padding=[(CONV_LEFT, CONV_W - 1 - CONV_LEFT)],
                                   dimension_numbers=("NWC", "WIO", "NWC"),
                                   feature_group_count=ch)
    return out + b


def rglru_coeffs(z, w_a, b_a, w_i, b_i, lam):
    bsz, L, _ = z.shape
    zb = z.reshape(bsz, L, C_HEADS, C_BLOCK)
    r = jax.nn.sigmoid((jnp.einsum("blhi,hij->blhj", zb, w_a).reshape(bsz, L, C_WIDTH) + b_a).astype(jnp.float32))
    ig = jax.nn.sigmoid((jnp.einsum("blhi,hij->blhj", zb, w_i).reshape(bsz, L, C_WIDTH) + b_i).astype(jnp.float32))
    log_a = -LRU_C * r * jax.nn.softplus(-lam.astype(jnp.float32))
    a = jnp.exp(log_a)
    bx = jnp.sqrt(-jnp.expm1(2.0 * log_a)) * (ig * z.astype(jnp.float32))
    return a, bx


def linear_scan(a, b, h0, reverse):
    if reverse:
        a, b = jnp.flip(a, axis=1), jnp.flip(b, axis=1)

    def combine(e1, e2):
        a1, b1 = e1
        a2, b2 = e2
        return a1 * a2, a2 * b1 + b2

    a_cum, b_cum = lax.associative_scan(combine, (a, b), axis=1)
    h = a_cum * h0[:, None, :] + b_cum
    if reverse:
        h = jnp.flip(h, axis=1)
    return h


def odd_mixer(h, hc, w_in, conv_w, conv_b, w_a, b_a, w_i, b_i, lam, w_out, need_ctx):
    bsz = h.shape[0]
    xr, g = _split(h @ w_in, (C_WIDTH, C_WIDTH))
    if need_ctx:
        xr_c, g_c = _split(hc @ w_in, (C_WIDTH, C_WIDTH))
    else:
        xr_c = hc @ w_in[:, :C_WIDTH]
    z = depthwise_conv(xr, conv_w, conv_b)
    z_c = depthwise_conv(xr_c, conv_w, conv_b)
    h_lat = None
    h_ctx = None
    for d, reverse in ((0, False), (1, True)):
        a_c, b_c = rglru_coeffs(z_c, w_a[d], b_a[d], w_i[d], b_i[d], lam[d])
        s_c = linear_scan(a_c, b_c, jnp.zeros((bsz, C_WIDTH), jnp.float32), reverse)
        h0 = s_c[:, 0] if reverse else s_c[:, -1]
        a_l, b_l = rglru_coeffs(z, w_a[d], b_a[d], w_i[d], b_i[d], lam[d])
        s_l = linear_scan(a_l, b_l, h0, reverse)
        h_lat = s_l if h_lat is None else h_lat + s_l
        if need_ctx:
            h_ctx = s_c if h_ctx is None else h_ctx + s_c
    y = (h_lat.astype(h.dtype) * jax.nn.silu(g)) @ w_out
    if not need_ctx:
        return y, None
    yc = (h_ctx.astype(hc.dtype) * jax.nn.silu(g_c)) @ w_out
    return y, yc


def setup_inputs(seed: int = 0) -> dict:
    key = jax.random.key(seed)
    ks = jax.random.split(key, 32)
    n = jax.random.normal
    f32 = jnp.float32
    d = D_MODEL
    u_lam = jax.random.uniform(ks[22], (N_ODD, 2, C_WIDTH), f32, minval=0.9, maxval=0.999)
    a0 = u_lam ** (1.0 / LRU_C)
    return {
        "x": n(ks[0], (BATCH, SEQ, d), f32),
        "c": n(ks[1], (BATCH, d), f32),
        "ctx": n(ks[2], (BATCH, CTX_LEN, d), f32),
        "c_ctx": n(ks[3], (d,), f32),
        "norm_g": 1.0 + 0.02 * n(ks[4], (DEPTH, d), f32),
        "w_mod": n(ks[5], (DEPTH, d, 3 * d), f32) * (0.5 * d ** -0.5),
        "b_mod": 0.02 * n(ks[6], (DEPTH, 3 * d), f32),
        "ab_w_in": n(ks[7], (N_EVEN, d, AB_IN), f32) * d ** -0.5,
        "a_ln_g": 1.0 + 0.02 * n(ks[8], (N_EVEN, A_WIDTH), f32),
        "a_ln_b": 0.02 * n(ks[9], (N_EVEN, A_WIDTH), f32),
        "a_w_s": n(ks[10], (N_EVEN, A_GROUPS, CHUNK, CHUNK), f32) * CHUNK ** -0.5,
        "a_b_s": 1.0 + 0.02 * n(ks[11], (N_EVEN, A_GROUPS, CHUNK), f32),
        "b_sink": 0.5 * n(ks[12], (N_EVEN, B_Q_HEADS), f32),
        "ab_w_out": n(ks[13], (N_EVEN, AB_MIX, d), f32) * AB_MIX ** -0.5,
        "c_w_in": n(ks[14], (N_ODD, d, 2 * C_WIDTH), f32) * d ** -0.5,
        "c_conv_w": n(ks[15], (N_ODD, CONV_W, C_WIDTH), f32) * CONV_W ** -0.5,
        "c_conv_b": 0.02 * n(ks[16], (N_ODD, C_WIDTH), f32),
        "c_w_a": n(ks[17], (N_ODD, 2, C_HEADS, C_BLOCK, C_BLOCK), f32) * C_BLOCK ** -0.5,
        "c_b_a": 0.02 * n(ks[18], (N_ODD, 2, C_WIDTH), f32),
        "c_w_i": n(ks[19], (N_ODD, 2, C_HEADS, C_BLOCK, C_BLOCK), f32) * C_BLOCK ** -0.5,
        "c_b_i": 0.02 * n(ks[20], (N_ODD, 2, C_WIDTH), f32),
        "c_lam": jnp.log(a0) - jnp.log1p(-a0),
        "c_w_out": n(ks[21], (N_ODD, C_WIDTH, d), f32) * C_WIDTH ** -0.5,
        "final_g": 1.0 + 0.02 * n(ks[23], (d,), f32),
    }


def reference(x, c, ctx, c_ctx, norm_g, w_mod, b_mod, ab_w_in, a_ln_g, a_ln_b, a_w_s, a_b_s, b_sink,
              ab_w_out, c_w_in, c_conv_w, c_conv_b, c_w_a, c_b_a, c_w_i, c_b_i, c_lam, c_w_out, final_g):
    L = x.shape[1]
    cos, sin = axial_rope(L)
    silu_c = jax.nn.silu(c)
    silu_cc = jax.nn.silu(c_ctx)
    xc = ctx
    for layer in range(DEPTH):
        need_ctx = layer < DEPTH - 1
        shift, scale, gate = jnp.split((silu_c @ w_mod[layer] + b_mod[layer])[:, None, :], 3, axis=-1)
        shift_c, scale_c, gate_c = jnp.split(silu_cc @ w_mod[layer] + b_mod[layer], 3, axis=-1)
        h = rmsnorm(x, norm_g[layer]) * (1.0 + scale) + shift
        hc = rmsnorm(xc, norm_g[layer]) * (1.0 + scale_c) + shift_c
        i = layer // 2
        if layer % 2 == 0:
            y, yc = even_mixer(h, hc, ab_w_in[i], a_ln_g[i], a_ln_b[i], a_w_s[i], a_b_s[i], b_sink[i],
                               ab_w_out[i], cos, sin, need_ctx)
        else:
            y, yc = odd_mixer(h, hc, c_w_in[i], c_conv_w[i], c_conv_b[i], c_w_a[i], c_b_a[i], c_w_i[i],
                              c_b_i[i], c_lam[i], c_w_out[i], need_ctx)
        x = x + gate * y
        if need_ctx:
            xc = xc + gate_c * yc
    return rmsnorm(x, final_g)
```

```python
import functools

import jax
import jax.numpy as jnp
from jax import lax
from jax.experimental import pallas as pl
from jax.experimental.pallas import tpu as pltpu

F32 = jnp.float32
BF16 = jnp.bfloat16

D_MODEL = 1024
DEPTH = 4
GRID_W = 64
EPS = 1e-6
CHUNK = 128
A_GROUPS = 4
A_WIDTH = 512
HEAD_DIM = 64
B_Q_HEADS = 8
B_KV_HEADS = 2
B_GQA = B_Q_HEADS // B_KV_HEADS
B_WIDTH = 512
KV_WIDTH = 128
BLOCK = 128
ROPE_THETA = 10000.0
AB_K_OFF = 3 * A_WIDTH + B_WIDTH
C_WIDTH = 1024
C_HEADS = 4
C_BLOCK = 256
CONV_W = 4
CONV_LEFT = 2
LRU_C = 8.0
MASKED = -1e30

LANES = 128
SUBLANES = 8
N_SLABS = C_WIDTH // LANES
MOD_ROWS = 16
HALO = 16
VMEM_LIMIT = 56 * 1024 * 1024

T_KV = 512
T_EVEN = 256
T_PROJ = 512
T_SCAN = 64


def _sigmoid(x):
    return 0.5 * jnp.tanh(0.5 * x) + 0.5


def _silu(x):
    return x * _sigmoid(x)


def _gelu(x):
    return 0.5 * x * (1.0 + jnp.tanh(0.7978845608028654 * (x + 0.044715 * (x * x * x))))


def _norm_mod(x, g1s, shift):
    ms = jnp.mean(x * x, axis=-1, keepdims=True)
    return x * lax.rsqrt(ms + EPS) * g1s + shift


def _mod_parts(m):
    return m[:, 0:D_MODEL], m[:, D_MODEL:2 * D_MODEL], m[:, 2 * D_MODEL:3 * D_MODEL]


def _rope(x, cosf, sinf):
    n = x.shape[-1]
    lane = lax.broadcasted_iota(jnp.int32, x.shape, 1)
    first = (lane % HEAD_DIM) < (HEAD_DIM // 2)
    sw = jnp.where(first, pltpu.roll(x, n - HEAD_DIM // 2, 1), pltpu.roll(x, HEAD_DIM // 2, 1))
    return x * cosf + sw * sinf


def _dot(a, b):
    return jnp.dot(a, b, preferred_element_type=F32)


def _dot_t(a, b):
    return lax.dot_general(a, b, (((1,), (1,)), ((), ())), preferred_element_type=F32)


def _mod_kernel(c_ref, w_ref, b_ref, o_ref):
    s = _silu(c_ref[...])
    o_ref[...] = _dot(s.astype(BF16), w_ref[...].astype(BF16)) + b_ref[...]


def _modulation(cin, w_mod, b_mod):
    nj = 3
    return pl.pallas_call(
        _mod_kernel,
        grid=(DEPTH, nj),
        in_specs=[
            pl.BlockSpec((MOD_ROWS, D_MODEL), lambda l, j: (0, 0)),
            pl.BlockSpec((None, D_MODEL, D_MODEL), lambda l, j: (l, 0, j)),
            pl.BlockSpec((None, 1, D_MODEL), lambda l, j: (l, 0, j)),
        ],
        out_specs=pl.BlockSpec((None, MOD_ROWS, D_MODEL), lambda l, j: (l, 0, j)),
        out_shape=jax.ShapeDtypeStruct((DEPTH, MOD_ROWS, 3 * D_MODEL), F32),
        compiler_params=pltpu.CompilerParams(
            dimension_semantics=("arbitrary", "arbitrary"), vmem_limit_bytes=VMEM_LIMIT),
        name="modulation",
    )(cin, w_mod, b_mod.reshape(DEPTH, 1, 3 * D_MODEL))


def _kv_kernel(x_ref, m_ref, g_ref, w_ref, cos_ref, sin_ref, k_ref, v_ref, *, rope):
    shift, scale, _ = _mod_parts(m_ref[...])
    h = _norm_mod(x_ref[...], g_ref[...] * (1.0 + scale), shift).astype(BF16)
    kv = _dot(h, w_ref[...])
    k = kv[:, 0:KV_WIDTH]
    if rope:
        k = _rope(k, cos_ref[...], sin_ref[...])
    k_ref[...] = k.astype(BF16)
    v_ref[...] = kv[:, KV_WIDTH:2 * KV_WIDTH].astype(BF16)


def _kv_project(x, mod_l, mod_row, g, w_kv, cosf, sinf, *, rope, tile):
    bsz, L, _ = x.shape
    n = L // tile
    row = (lambda b: b) if mod_row is None else (lambda b: mod_row)
    return pl.pallas_call(
        functools.partial(_kv_kernel, rope=rope),
        grid=(bsz, n),
        in_specs=[
            pl.BlockSpec((None, tile, D_MODEL), lambda b, i: (b, i, 0)),
            pl.BlockSpec((None, 1, 3 * D_MODEL), lambda b, i: (row(b), 0, 0)),
            pl.BlockSpec((1, D_MODEL), lambda b, i: (0, 0)),
            pl.BlockSpec((D_MODEL, 2 * KV_WIDTH), lambda b, i: (0, 0)),
            pl.BlockSpec((tile, LANES), lambda b, i: (i, 0)),
            pl.BlockSpec((tile, LANES), lambda b, i: (i, 0)),
        ],
        out_specs=[
            pl.BlockSpec((None, tile, KV_WIDTH), lambda b, i: (b, i, 0)),
            pl.BlockSpec((None, tile, KV_WIDTH), lambda b, i: (b, i, 0)),
        ],
        out_shape=[jax.ShapeDtypeStruct((bsz, L, KV_WIDTH), BF16)] * 2,
        compiler_params=pltpu.CompilerParams(
            dimension_semantics=("parallel", "parallel"), vmem_limit_bytes=VMEM_LIMIT),
        name="even_kv",
    )(x, mod_l, g, w_kv, cosf, sinf)


def _softmax_pv(scores, values, sink):
    m = sink
    for s in scores:
        m = jnp.maximum(m, jnp.max(s, axis=-1, keepdims=True))
    den = jnp.exp(sink - m)
    out = None
    for s, v in zip(scores, values):
        p = jnp.exp(s - m)
        den = den + jnp.sum(p, axis=-1, keepdims=True)
        pv = _dot(p.astype(BF16), v)
        out = pv if out is None else out + pv
    return out / den


def _even_kernel(*refs, tile, n_blocks_total, latent):
    if latent:
        (x_ref, m_ref, g_ref, win_ref, wout_ref, lng_ref, lnb_ref, ws_ref, bs_ref, sink_ref,
         cos_ref, sin_ref, kp_ref, kc_ref, kn_ref, vp_ref, vc_ref, vn_ref, kctx_ref, vctx_ref,
         o_ref, ga_scr, gb_scr, vln_scr, q_scr, mix_scr, kbuf, vbuf) = refs
    else:
        (x_ref, m_ref, g_ref, win_ref, wout_ref, lng_ref, lnb_ref, ws_ref, bs_ref, sink_ref,
         kctx_ref, vctx_ref, o_ref, ga_scr, gb_scr, vln_scr, q_scr, mix_scr) = refs
    nb = tile // BLOCK
    i = pl.program_id(1)

    x = x_ref[...]
    shift, scale, gate = _mod_parts(m_ref[...])
    h = _norm_mod(x, g_ref[...] * (1.0 + scale), shift).astype(BF16)

    u = _dot(h, win_ref[:, 0:A_WIDTH])
    gate_a = _dot(h, win_ref[:, 2 * A_WIDTH:3 * A_WIDTH])
    ga_scr[...] = _gelu(u) * _silu(gate_a)
    v = _gelu(_dot(h, win_ref[:, A_WIDTH:2 * A_WIDTH]))
    for g in range(A_GROUPS):
        sl = slice(g * LANES, (g + 1) * LANES)
        vg = v[:, sl]
        mu = jnp.mean(vg, axis=-1, keepdims=True)
        vc = vg - mu
        var = jnp.mean(vc * vc, axis=-1, keepdims=True)
        vln_scr[:, sl] = (vc * lax.rsqrt(var + EPS) * lng_ref[:, sl] + lnb_ref[:, sl]).astype(BF16)
    q = _dot(h, win_ref[:, 3 * A_WIDTH:3 * A_WIDTH + B_WIDTH])
    for s in range(B_WIDTH // LANES):
        sl = slice(s * LANES, (s + 1) * LANES)
        qs = q[:, sl]
        if latent:
            qs = _rope(qs, cos_ref[...], sin_ref[...])
        q_scr[:, sl] = (qs * (HEAD_DIM ** -0.5)).astype(BF16)
    gb_scr[...] = _silu(_dot(h, win_ref[:, 3 * A_WIDTH + B_WIDTH:3 * A_WIDTH + 2 * B_WIDTH]))

    if latent:
        kbuf[0:BLOCK, :] = kp_ref[...]
        kbuf[BLOCK:BLOCK + tile, :] = kc_ref[...]
        kbuf[BLOCK + tile:2 * BLOCK + tile, :] = kn_ref[...]
        vbuf[0:BLOCK, :] = vp_ref[...]
        vbuf[BLOCK:BLOCK + tile, :] = vc_ref[...]
        vbuf[BLOCK + tile:2 * BLOCK + tile, :] = vn_ref[...]
        qi = lax.broadcasted_iota(jnp.int32, (BLOCK, 3 * BLOCK), 0)
        kj = lax.broadcasted_iota(jnp.int32, (BLOCK, 3 * BLOCK), 1)
        band = (kj >= qi) & (kj <= qi + 2 * BLOCK)

    for jb in range(nb):
        rows = slice(jb * BLOCK, (jb + 1) * BLOCK)
        for g in range(A_GROUPS):
            sl = slice(g * LANES, (g + 1) * LANES)
            sv = _dot(ws_ref[g], vln_scr[rows, sl]) + bs_ref[g]
            mix_scr[rows, sl] = (ga_scr[rows, sl] * sv).astype(BF16)
        if latent:
            gblk = i * nb + jb
            lo = jnp.where(gblk > 0, 0, BLOCK)
            hi = jnp.where(gblk < n_blocks_total - 1, 3 * BLOCK, 2 * BLOCK)
            mask = band & (kj >= lo) & (kj < hi)
            kloc = kbuf[jb * BLOCK:(jb + 3) * BLOCK, :]
            vloc = vbuf[jb * BLOCK:(jb + 3) * BLOCK, :]
        kctx = kctx_ref[...]
        vctx = vctx_ref[...]
        for pair in range(B_Q_HEADS // 2):
            outs = []
            qpair = q_scr[rows, pair * LANES:(pair + 1) * LANES]
            for hh in range(2):
                head = 2 * pair + hh
                kvh = head // B_GQA
                hs = slice(kvh * HEAD_DIM, (kvh + 1) * HEAD_DIM)
                qh = qpair[:, hh * HEAD_DIM:(hh + 1) * HEAD_DIM]
                scores = [_dot_t(qh, kctx[:, hs])]
                values = [vctx[:, hs]]
                if latent:
                    scores.append(jnp.where(mask, _dot_t(qh, kloc[:, hs]), MASKED))
                    values.append(vloc[:, hs])
                outs.append(_softmax_pv(scores, values, sink_ref[head]))
            sl = slice(pair * LANES, (pair + 1) * LANES)
            yb = jnp.concatenate(outs, axis=-1) * gb_scr[rows, sl]
            mix_scr[rows, A_WIDTH + pair * LANES:A_WIDTH + (pair + 1) * LANES] = yb.astype(BF16)

    y = _dot(mix_scr[...], wout_ref[...])
    o_ref[...] = x + gate * y


def _even_mix(x, mod_l, mod_row, g, w_main, w_out, lng, lnb, ws, bs, sink, kctx, vctx,
              cosf=None, sinf=None, k=None, v=None):
    latent = k is not None
    bsz, L, _ = x.shape
    tile = T_EVEN
    n = L // tile
    nb = tile // BLOCK
    nblk = L // BLOCK
    row = (lambda b: b) if mod_row is None else (lambda b: mod_row)
    full = lambda shape: pl.BlockSpec(shape, lambda b, i: (0,) * len(shape))
    in_specs = [
        pl.BlockSpec((None, tile, D_MODEL), lambda b, i: (b, i, 0)),
        pl.BlockSpec((None, 1, 3 * D_MODEL), lambda b, i: (row(b), 0, 0)),
        full((1, D_MODEL)),
        full(w_main.shape),
        full(w_out.shape),
        full((1, A_WIDTH)),
        full((1, A_WIDTH)),
        full(ws.shape),
        full(bs.shape),
        pl.BlockSpec(memory_space=pltpu.SMEM),
    ]
    args = [x, mod_l, g, w_main, w_out, lng, lnb, ws, bs, sink]
    scratch = [
        pltpu.VMEM((tile, A_WIDTH), F32),
        pltpu.VMEM((tile, B_WIDTH), F32),
        pltpu.VMEM((tile, A_WIDTH), BF16),
        pltpu.VMEM((tile, B_WIDTH), BF16),
        pltpu.VMEM((tile, A_WIDTH + B_WIDTH), BF16),
    ]
    if latent:
        kv_prev = pl.BlockSpec((None, BLOCK, KV_WIDTH), lambda b, i: (b, jnp.maximum(i * nb - 1, 0), 0))
        kv_cur = pl.BlockSpec((None, tile, KV_WIDTH), lambda b, i: (b, i, 0))
        kv_next = pl.BlockSpec((None, BLOCK, KV_WIDTH), lambda b, i: (b, jnp.minimum((i + 1) * nb, nblk - 1), 0))
        in_specs += [
            pl.BlockSpec((tile, LANES), lambda b, i: (i, 0)),
            pl.BlockSpec((tile, LANES), lambda b, i: (i, 0)),
            kv_prev, kv_cur, kv_next, kv_prev, kv_cur, kv_next,
        ]
        args += [cosf, sinf, k, k, k, v, v, v]
        scratch += [pltpu.VMEM((tile + 2 * BLOCK, KV_WIDTH), BF16)] * 2
    ctx_len = kctx.shape[1]
    in_specs += [pl.BlockSpec((None, ctx_len, KV_WIDTH), lambda b, i: (b, 0, 0))] * 2
    args += [kctx, vctx]
    return pl.pallas_call(
        functools.partial(_even_kernel, tile=tile, n_blocks_total=nblk, latent=latent),
        grid=(bsz, n),
        in_specs=in_specs,
        out_specs=pl.BlockSpec((None, tile, D_MODEL), lambda b, i: (b, i, 0)),
        out_shape=jax.ShapeDtypeStruct(x.shape, F32),
        scratch_shapes=scratch,
        compiler_params=pltpu.CompilerParams(
            dimension_semantics=("parallel", "parallel"), vmem_limit_bytes=VMEM_LIMIT),
        name="even_mix" if latent else "even_mix_ctx",
    )(*args)


def _odd_proj_kernel(xp_ref, x_ref, xn_ref, m_ref, g_ref, wx_ref, wg_ref, cw_ref, cb_ref,
                     z_ref, sg_ref, xr_scr, *, tile, n_tiles):
    i = pl.program_id(1)
    shift, scale, _ = _mod_parts(m_ref[...])
    xe = jnp.concatenate([xp_ref[...], x_ref[...], xn_ref[...]], axis=0)
    he = _norm_mod(xe, g_ref[...] * (1.0 + scale), shift).astype(BF16)
    xr = _dot(he, wx_ref[...])
    row = lax.broadcasted_iota(jnp.int32, (tile + 2 * HALO, 1), 0)
    lo = jnp.where(i > 0, 0, HALO)
    hi = jnp.where(i < n_tiles - 1, tile + 2 * HALO, tile + HALO)
    xr_scr[...] = jnp.where((row >= lo) & (row < hi), xr, 0.0)
    z = cb_ref[...]
    for j in range(CONV_W):
        off = HALO - CONV_LEFT + j
        z = z + cw_ref[j:j + 1, :] * xr_scr[off:off + tile, :]
    z_ref[...] = z
    sg_ref[...] = _silu(_dot(he[HALO:HALO + tile], wg_ref[...]))


def _odd_project(x, mod_l, mod_row, g, w_x, w_g, conv_w, conv_b, *, tile):
    bsz, L, _ = x.shape
    n = L // tile
    hb = tile // HALO
    nh = L // HALO
    row = (lambda b: b) if mod_row is None else (lambda b: mod_row)
    full = lambda shape: pl.BlockSpec(shape, lambda b, i: (0,) * len(shape))
    return pl.pallas_call(
        functools.partial(_odd_proj_kernel, tile=tile, n_tiles=n),
        grid=(bsz, n),
        in_specs=[
            pl.BlockSpec((None, HALO, D_MODEL), lambda b, i: (b, jnp.maximum(i * hb - 1, 0), 0)),
            pl.BlockSpec((None, tile, D_MODEL), lambda b, i: (b, i, 0)),
            pl.BlockSpec((None, HALO, D_MODEL), lambda b, i: (b, jnp.minimum((i + 1) * hb, nh - 1), 0)),
            pl.BlockSpec((None, 1, 3 * D_MODEL), lambda b, i: (row(b), 0, 0)),
            full((1, D_MODEL)),
            full(w_x.shape),
            full(w_g.shape),
            full((CONV_W, C_WIDTH)),
            full((1, C_WIDTH)),
        ],
        out_specs=[
            pl.BlockSpec((None, tile, C_WIDTH), lambda b, i: (b, i, 0)),
            pl.BlockSpec((None, tile, C_WIDTH), lambda b, i: (b, i, 0)),
        ],
        out_shape=[jax.ShapeDtypeStruct((bsz, L, C_WIDTH), F32)] * 2,
        scratch_shapes=[pltpu.VMEM((tile + 2 * HALO, C_WIDTH), F32)],
        compiler_params=pltpu.CompilerParams(
            dimension_semantics=("parallel", "parallel"), vmem_limit_bytes=VMEM_LIMIT),
        name="odd_proj",
    )(x, x, x, mod_l, g, w_x, w_g, conv_w, conv_b)


def _gates_to_slabs(z_ref, wa_ref, wi_ref, ba_ref, bi_ref, lam_ref, a_scr, bx_ref, *, bsz, tile):
    zz = z_ref[...].reshape(bsz * tile, C_WIDTH)
    lam = lam_ref[...]
    neg = -lam
    softplus = jnp.maximum(neg, 0.0) + jnp.log1p(jnp.exp(-jnp.abs(neg)))
    nsp = -LRU_C * softplus
    for hd in range(C_HEADS):
        cols = slice(hd * C_BLOCK, (hd + 1) * C_BLOCK)
        zc = zz[:, cols]
        zb = zc.astype(BF16)
        r = _sigmoid(_dot(zb, wa_ref[hd]) + ba_ref[:, cols])
        ig = _sigmoid(_dot(zb, wi_ref[hd]) + bi_ref[:, cols])
        log_a = r * nsp[:, cols]
        a = jnp.exp(log_a)
        bx = jnp.sqrt(-jnp.tanh(log_a) * (a * a + 1.0)) * (ig * zc)
        for half in range(C_BLOCK // LANES):
            slab = hd * (C_BLOCK // LANES) + half
            ls = slice(half * LANES, (half + 1) * LANES)
            for b in range(bsz):
                rs = slice(b * tile, (b + 1) * tile)
                a_scr[slab, pl.ds(b, tile, stride=bsz), :] = a[rs, ls]
                bx_ref[slab, pl.ds(b, tile, stride=bsz), :] = bx[rs, ls]


def _scan_tile(a_scr, bx_ref, h_scr, *, tile, reverse):
    def body(kk, hs):
        t = (tile - 1 - kk) if reverse else kk
        r = pl.multiple_of(t * SUBLANES, SUBLANES)
        new = []
        for c in range(N_SLABS):
            hc = a_scr[c, pl.ds(r, SUBLANES), :] * hs[c] + bx_ref[c, pl.ds(r, SUBLANES), :]
            bx_ref[c, pl.ds(r, SUBLANES), :] = hc
            new.append(hc)
        return tuple(new)

    hs = lax.fori_loop(0, tile, body, tuple(h_scr[c] for c in range(N_SLABS)), unroll=4)
    for c in range(N_SLABS):
        h_scr[c] = hs[c]


def _scan_rev_kernel(z_ref, wa_ref, wi_ref, ba_ref, bi_ref, lam_ref, h0_ref,
                     s_ref, hfin_ref, a_scr, h_scr, *, bsz, tile):
    @pl.when(pl.program_id(0) == 0)
    def _():
        h_scr[...] = h0_ref[...]

    _gates_to_slabs(z_ref, wa_ref, wi_ref, ba_ref, bi_ref, lam_ref, a_scr, s_ref, bsz=bsz, tile=tile)
    _scan_tile(a_scr, s_ref, h_scr, tile=tile, reverse=True)
    hfin_ref[...] = h_scr[...]


def _scan_fwd_kernel(z_ref, srev_ref, sg_ref, x_ref, m_ref, wa_ref, wi_ref, ba_ref, bi_ref, lam_ref,
                     h0_ref, wout_ref, fg_ref, o_ref, hfin_ref, a_scr, bx_scr, h_scr, mix_scr,
                     *, bsz, tile, mod_row, final_norm):
    @pl.when(pl.program_id(0) == 0)
    def _():
        h_scr[...] = h0_ref[...]

    _gates_to_slabs(z_ref, wa_ref, wi_ref, ba_ref, bi_ref, lam_ref, a_scr, bx_scr, bsz=bsz, tile=tile)
    _scan_tile(a_scr, bx_scr, h_scr, tile=tile, reverse=False)
    hfin_ref[...] = h_scr[...]

    for b in range(bsz):
        rs = slice(b * tile, (b + 1) * tile)
        for c in range(N_SLABS):
            ls = slice(c * LANES, (c + 1) * LANES)
            hsum = (bx_scr[c, pl.ds(b, tile, stride=bsz), :]
                    + srev_ref[c, pl.ds(b, tile, stride=bsz), :])
            mix_scr[rs, ls] = (hsum * sg_ref[b, :, ls]).astype(BF16)
    y = _dot(mix_scr[...], wout_ref[...])
    m = m_ref[...]
    for b in range(bsz):
        r = b if mod_row is None else mod_row
        gate = m[r:r + 1, 2 * D_MODEL:3 * D_MODEL]
        xn = x_ref[b] + gate * y[b * tile:(b + 1) * tile]
        if final_norm:
            ms = jnp.mean(xn * xn, axis=-1, keepdims=True)
            xn = xn * lax.rsqrt(ms + EPS) * fg_ref[...]
        o_ref[b] = xn


def _gate_specs(idx):
    full = lambda shape: pl.BlockSpec(shape, lambda i: (0,) * len(shape))
    return [
        full((C_HEADS, C_BLOCK, C_BLOCK)),
        full((C_HEADS, C_BLOCK, C_BLOCK)),
        full((1, C_WIDTH)),
        full((1, C_WIDTH)),
        full((1, C_WIDTH)),
        full((N_SLABS, SUBLANES, LANES)),
    ]


def _scan_reverse(z, w_a, w_i, b_a, b_i, lam, h0):
    bsz, L, _ = z.shape
    tile = T_SCAN
    n = L // tile
    rev = lambda i: (0, n - 1 - i, 0)
    return pl.pallas_call(
        functools.partial(_scan_rev_kernel, bsz=bsz, tile=tile),
        grid=(n,),
        in_specs=[pl.BlockSpec((bsz, tile, C_WIDTH), rev)] + _gate_specs(None),
        out_specs=[
            pl.BlockSpec((N_SLABS, tile * bsz, LANES), rev),
            pl.BlockSpec((N_SLABS, SUBLANES, LANES), lambda i: (0, 0, 0)),
        ],
        out_shape=[
            jax.ShapeDtypeStruct((N_SLABS, L * bsz, LANES), F32),
            jax.ShapeDtypeStruct((N_SLABS, SUBLANES, LANES), F32),
        ],
        scratch_shapes=[
            pltpu.VMEM((N_SLABS, tile * bsz, LANES), F32),
            pltpu.VMEM((N_SLABS, SUBLANES, LANES), F32),
        ],
        compiler_params=pltpu.CompilerParams(
            dimension_semantics=("arbitrary",), vmem_limit_bytes=VMEM_LIMIT),
        name="odd_scan_rev",
    )(z, w_a, w_i, b_a, b_i, lam, h0)


def _scan_forward(z, s_rev, sg, x, mod_l, mod_row, w_a, w_i, b_a, b_i, lam, h0, w_out, final_g):
    bsz, L, _ = z.shape
    tile = T_SCAN
    n = L // tile
    fwd = lambda i: (0, i, 0)
    full = lambda shape: pl.BlockSpec(shape, lambda i: (0,) * len(shape))
    final_norm = final_g is not None
    fg = final_g if final_norm else jnp.ones((1, D_MODEL), F32)
    return pl.pallas_call(
        functools.partial(_scan_fwd_kernel, bsz=bsz, tile=tile, mod_row=mod_row, final_norm=final_norm),
        grid=(n,),
        in_specs=[
            pl.BlockSpec((bsz, tile, C_WIDTH), fwd),
            pl.BlockSpec((N_SLABS, tile * bsz, LANES), fwd),
            pl.BlockSpec((bsz, tile, C_WIDTH), fwd),
            pl.BlockSpec((bsz, tile, D_MODEL), fwd),
            full((MOD_ROWS, 3 * D_MODEL)),
        ] + _gate_specs(None) + [full((C_WIDTH, D_MODEL)), full((1, D_MODEL))],
        out_specs=[
            pl.BlockSpec((bsz, tile, D_MODEL), fwd),
            pl.BlockSpec((N_SLABS, SUBLANES, LANES), lambda i: (0, 0, 0)),
        ],
        out_shape=[
            jax.ShapeDtypeStruct(x.shape, F32),
            jax.ShapeDtypeStruct((N_SLABS, SUBLANES, LANES), F32),
        ],
        scratch_shapes=[
            pltpu.VMEM((N_SLABS, tile * bsz, LANES), F32),
            pltpu.VMEM((N_SLABS, tile * bsz, LANES), F32),
            pltpu.VMEM((N_SLABS, SUBLANES, LANES), F32),
            pltpu.VMEM((bsz * tile, C_WIDTH), BF16),
        ],
        compiler_params=pltpu.CompilerParams(
            dimension_semantics=("arbitrary",), vmem_limit_bytes=VMEM_LIMIT),
        name="odd_scan_fwd",
    )(z, s_rev, sg, x, mod_l, w_a, w_i, b_a, b_i, lam, h0, w_out, fg)


def _rope_tables(L):
    rows = L // GRID_W
    r, col = jnp.meshgrid(jnp.arange(rows), jnp.arange(GRID_W), indexing="ij")
    r = r.reshape(-1).astype(F32)
    col = col.reshape(-1).astype(F32)
    n_freq = HEAD_DIM // 4
    inv_freq = ROPE_THETA ** (-jnp.arange(n_freq, dtype=F32) / n_freq)
    ang = jnp.concatenate([r[:, None] * inv_freq, col[:, None] * inv_freq], axis=-1)
    cos, sin = jnp.cos(ang), jnp.sin(ang)
    cosf = jnp.tile(jnp.concatenate([cos, cos], axis=-1), (1, LANES // HEAD_DIM))
    sinf = jnp.tile(jnp.concatenate([-sin, sin], axis=-1), (1, LANES // HEAD_DIM))
    return cosf, sinf


def kernel(x, c, ctx, c_ctx, norm_g, w_mod, b_mod, ab_w_in, a_ln_g, a_ln_b, a_w_s, a_b_s, b_sink,
           ab_w_out, c_w_in, c_conv_w, c_conv_b, c_w_a, c_b_a, c_w_i, c_b_i, c_lam, c_w_out, final_g):
    bsz, L, d = x.shape
    ctx_len = ctx.shape[1]
    assert d == D_MODEL and bsz == SUBLANES and L % T_PROJ == 0 and ctx_len % T_EVEN == 0
    ctx_row = bsz

    cin = jnp.concatenate([c, c_ctx[None, :], jnp.zeros((MOD_ROWS - bsz - 1, d), F32)], axis=0)
    mod = _modulation(cin, w_mod, b_mod).reshape(DEPTH, MOD_ROWS, 1, 3 * d)
    cosf, sinf = _rope_tables(L)
    zero_state = jnp.zeros((N_SLABS, SUBLANES, LANES), F32)

    xc = ctx
    for layer in range(DEPTH):
        need_ctx = layer < DEPTH - 1
        i = layer // 2
        mod_l = mod[layer]
        g = norm_g[layer][None, :]
        if layer % 2 == 0:
            w_in = ab_w_in[i].astype(BF16)
            w_kv = w_in[:, AB_K_OFF:AB_K_OFF + 2 * KV_WIDTH]
            w_main = jnp.concatenate([w_in[:, :AB_K_OFF], w_in[:, AB_K_OFF + 2 * KV_WIDTH:]], axis=1)
            w_out = ab_w_out[i].astype(BF16)
            lng = a_ln_g[i][None, :]
            lnb = a_ln_b[i][None, :]
            ws = a_w_s[i].astype(BF16)
            bs = jnp.broadcast_to(a_b_s[i][:, :, None], (A_GROUPS, CHUNK, LANES))
            sink = b_sink[i]
            kc, vc = _kv_project(xc, mod_l, ctx_row, g, w_kv, cosf, sinf, rope=False, tile=ctx_len)
            k, v = _kv_project(x, mod_l, None, g, w_kv, cosf, sinf, rope=True, tile=T_KV)
            x = _even_mix(x, mod_l, None, g, w_main, w_out, lng, lnb, ws, bs, sink, kc, vc,
                          cosf=cosf, sinf=sinf, k=k, v=v)
            if need_ctx:
                xc = _even_mix(xc, mod_l, ctx_row, g, w_main, w_out, lng, lnb, ws, bs, sink, kc, vc)
        else:
            w_in = c_w_in[i].astype(BF16)
            w_x, w_g = w_in[:, :C_WIDTH], w_in[:, C_WIDTH:]
            w_out = c_w_out[i].astype(BF16)
            conv_w, conv_b = c_conv_w[i], c_conv_b[i][None, :]
            mod_flat = mod_l.reshape(MOD_ROWS, 3 * d)
            gate_args = [(c_w_a[i, dd].astype(BF16), c_w_i[i, dd].astype(BF16), c_b_a[i, dd][None, :],
                          c_b_i[i, dd][None, :], c_lam[i, dd][None, :]) for dd in range(2)]
            z_c, sg_c = _odd_project(xc, mod_l, ctx_row, g, w_x, w_g, conv_w, conv_b, tile=ctx_len)
            z, sg = _odd_project(x, mod_l, None, g, w_x, w_g, conv_w, conv_b, tile=T_PROJ)
            s_rev_c, h0_rev = _scan_reverse(z_c, *gate_args[1], zero_state)
            xc_new, h0_fwd = _scan_forward(z_c, s_rev_c, sg_c, xc, mod_flat, ctx_row, *gate_args[0],
                                           zero_state, w_out, None)
            s_rev, _ = _scan_reverse(z, *gate_args[1], h0_rev)
            fin = final_g[None, :] if layer == DEPTH - 1 else None
            x, _ = _scan_forward(z, s_rev, sg, x, mod_flat, None, *gate_args[0], h0_fwd, w_out, fin)
            if need_ctx:
                xc = xc_new
    return x
```

```python
import functools

import jax
import jax.numpy as jnp
from jax import lax
from jax.experimental import pallas as pl
from jax.experimental.pallas import tpu as pltpu

F32 = jnp.float32
BF16 = jnp.bfloat16

D_MODEL = 1024
DEPTH = 4
GRID_W = 64
EPS = 1e-6
CHUNK = 128
A_GROUPS = 4
A_WIDTH = 512
HEAD_DIM = 64
B_Q_HEADS = 8
B_KV_HEADS = 2
B_GQA = B_Q_HEADS // B_KV_HEADS
B_WIDTH = 512
KV_WIDTH = 128
BLOCK = 128
ROPE_THETA = 10000.0
AB_Q_OFF = 3 * A_WIDTH
AB_K_OFF = 3 * A_WIDTH + B_WIDTH
AB_GB_OFF = AB_K_OFF + 2 * KV_WIDTH
C_WIDTH = 1024
C_HEADS = 4
C_BLOCK = 256
CONV_W = 4
CONV_LEFT = 2
LRU_C = 8.0
MASKED = -1e30

LANES = 128
SUBLANES = 8
N_SLABS = C_WIDTH // LANES
MOD_ROWS = 16
HALO = 16
VMEM_LIMIT = 56 * 1024 * 1024
KV_DUP = 2 * KV_WIDTH

T_KV = 512
T_EVEN = 512
T_PROJ = 512
T_SCAN = 64


def _sigmoid(x):
    return 0.5 * jnp.tanh(0.5 * x) + 0.5


def _silu(x):
    return x * _sigmoid(x)


def _gelu(x):
    return 0.5 * x * (1.0 + jnp.tanh(0.7978845608028654 * (x + 0.044715 * (x * x * x))))


def _norm_mod(x, g1s, shift):
    ms = jnp.mean(x * x, axis=-1, keepdims=True)
    return x * lax.rsqrt(ms + EPS) * g1s + shift


def _mod_parts(m):
    return m[:, 0:D_MODEL], m[:, D_MODEL:2 * D_MODEL], m[:, 2 * D_MODEL:3 * D_MODEL]


def _rope(x, cosf, sinf):
    n = x.shape[-1]
    lane = lax.broadcasted_iota(jnp.int32, x.shape, 1)
    first = (lane % HEAD_DIM) < (HEAD_DIM // 2)
    sw = jnp.where(first, pltpu.roll(x, n - HEAD_DIM // 2, 1), pltpu.roll(x, HEAD_DIM // 2, 1))
    return x * cosf + sw * sinf


def _dot(a, b):
    return jnp.dot(a, b, preferred_element_type=F32)


def _dot_t(a, b):
    return lax.dot_general(a, b, (((1,), (1,)), ((), ())), preferred_element_type=F32)


def _mod_kernel(c_ref, w_ref, b_ref, o_ref):
    s = _silu(c_ref[...])
    o_ref[...] = _dot(s.astype(BF16), w_ref[...].astype(BF16)) + b_ref[...]


def _modulation(cin, w_mod, b_mod):
    nj = 3
    return pl.pallas_call(
        _mod_kernel,
        grid=(DEPTH, nj),
        in_specs=[
            pl.BlockSpec((MOD_ROWS, D_MODEL), lambda l, j: (0, 0)),
            pl.BlockSpec((None, D_MODEL, D_MODEL), lambda l, j: (l, 0, j)),
            pl.BlockSpec((None, 1, D_MODEL), lambda l, j: (l, 0, j)),
        ],
        out_specs=pl.BlockSpec((None, MOD_ROWS, D_MODEL), lambda l, j: (l, 0, j)),
        out_shape=jax.ShapeDtypeStruct((DEPTH, MOD_ROWS, 3 * D_MODEL), F32),
        compiler_params=pltpu.CompilerParams(
            dimension_semantics=("arbitrary", "arbitrary"), vmem_limit_bytes=VMEM_LIMIT),
        name="modulation",
    )(cin, w_mod, b_mod.reshape(DEPTH, 1, 3 * D_MODEL))


def _kv_kernel(x_ref, m_ref, g_ref, w_ref, cos_ref, sin_ref, k_ref, v_ref, *, rope):
    shift, scale, _ = _mod_parts(m_ref[...])
    h = _norm_mod(x_ref[...], g_ref[...] * (1.0 + scale), shift).astype(BF16)
    kv = _dot(h, w_ref[...])
    for s in range(KV_DUP // LANES):
        sl = slice(s * LANES, (s + 1) * LANES)
        k = kv[:, sl]
        if rope:
            k = _rope(k, cos_ref[...], sin_ref[...])
        k_ref[:, sl] = k.astype(BF16)
    v_ref[...] = kv[:, KV_DUP:2 * KV_DUP].astype(BF16)


def _kv_project(x, mod_l, mod_row, g, w_kv, cosf, sinf, *, rope, tile):
    bsz, L, _ = x.shape
    n = L // tile
    row = (lambda b: b) if mod_row is None else (lambda b: mod_row)
    return pl.pallas_call(
        functools.partial(_kv_kernel, rope=rope),
        grid=(bsz, n),
        in_specs=[
            pl.BlockSpec((None, tile, D_MODEL), lambda b, i: (b, i, 0)),
            pl.BlockSpec((None, 1, 3 * D_MODEL), lambda b, i: (row(b), 0, 0)),
            pl.BlockSpec((1, D_MODEL), lambda b, i: (0, 0)),
            pl.BlockSpec((D_MODEL, 2 * KV_DUP), lambda b, i: (0, 0)),
            pl.BlockSpec((tile, LANES), lambda b, i: (i, 0)),
            pl.BlockSpec((tile, LANES), lambda b, i: (i, 0)),
        ],
        out_specs=[
            pl.BlockSpec((None, tile, KV_DUP), lambda b, i: (b, i, 0)),
            pl.BlockSpec((None, tile, KV_DUP), lambda b, i: (b, i, 0)),
        ],
        out_shape=[jax.ShapeDtypeStruct((bsz, L, KV_DUP), BF16)] * 2,
        compiler_params=pltpu.CompilerParams(
            dimension_semantics=("parallel", "parallel"), vmem_limit_bytes=VMEM_LIMIT),
        name="even_kv",
    )(x, mod_l, g, w_kv, cosf, sinf)


def _even_kernel(*refs, tile, n_blocks_total, latent):
    if latent:
        (x_ref, m_ref, g_ref, win_ref, wout_ref, lng_ref, lnb_ref, ws_ref, bs_ref, sink_ref,
         cos_ref, sin_ref, kp_ref, kc_ref, kn_ref, vp_ref, vc_ref, vn_ref, kctx_ref, vctx_ref,
         o_ref, ga_scr, gb_scr, vln_scr, q_scr, mix_scr, kbuf, vbuf) = refs
    else:
        (x_ref, m_ref, g_ref, win_ref, wout_ref, lng_ref, lnb_ref, ws_ref, bs_ref, sink_ref,
         kctx_ref, vctx_ref, o_ref, ga_scr, gb_scr, vln_scr, q_scr, mix_scr) = refs
    nb = tile // BLOCK
    i = pl.program_id(1)
    rows4 = B_GQA * BLOCK

    x = x_ref[...]
    shift, scale, gate = _mod_parts(m_ref[...])
    h = _norm_mod(x, g_ref[...] * (1.0 + scale), shift).astype(BF16)

    u = _dot(h, win_ref[:, 0:A_WIDTH])
    gate_a = _dot(h, win_ref[:, 2 * A_WIDTH:3 * A_WIDTH])
    ga_scr[...] = _gelu(u) * _silu(gate_a)
    v = _gelu(_dot(h, win_ref[:, A_WIDTH:2 * A_WIDTH]))
    for g in range(A_GROUPS):
        sl = slice(g * LANES, (g + 1) * LANES)
        vg = v[:, sl]
        mu = jnp.mean(vg, axis=-1, keepdims=True)
        vc = vg - mu
        var = jnp.mean(vc * vc, axis=-1, keepdims=True)
        vln_scr[:, sl] = (vc * lax.rsqrt(var + EPS) * lng_ref[:, sl] + lnb_ref[:, sl]).astype(BF16)
    q = _dot(h, win_ref[:, AB_Q_OFF:AB_Q_OFF + B_WIDTH])
    low_half = lax.broadcasted_iota(jnp.int32, (tile, LANES), 1) < HEAD_DIM
    for pair in range(B_Q_HEADS // 2):
        qs = q[:, pair * LANES:(pair + 1) * LANES]
        if latent:
            qs = _rope(qs, cos_ref[...], sin_ref[...])
        qs = qs * (HEAD_DIM ** -0.5)
        q_lo = jnp.where(low_half, qs, 0.0).astype(BF16)
        q_hi = jnp.where(low_half, 0.0, qs).astype(BF16)
        for jb in range(nb):
            rows = slice(jb * BLOCK, (jb + 1) * BLOCK)
            base = (jb * B_Q_HEADS + 2 * pair) * BLOCK
            q_scr[base:base + BLOCK, :] = q_lo[rows]
            q_scr[base + BLOCK:base + 2 * BLOCK, :] = q_hi[rows]
    gb_scr[...] = _silu(_dot(h, win_ref[:, AB_GB_OFF:AB_GB_OFF + B_WIDTH]))

    if latent:
        kbuf[0:BLOCK, :] = kp_ref[...]
        kbuf[BLOCK:BLOCK + tile, :] = kc_ref[...]
        kbuf[BLOCK + tile:2 * BLOCK + tile, :] = kn_ref[...]
        vbuf[0:BLOCK, :] = vp_ref[...]
        vbuf[BLOCK:BLOCK + tile, :] = vc_ref[...]
        vbuf[BLOCK + tile:2 * BLOCK + tile, :] = vn_ref[...]
        qi = lax.broadcasted_iota(jnp.int32, (rows4, 3 * BLOCK), 0) & (BLOCK - 1)
        kj = lax.broadcasted_iota(jnp.int32, (rows4, 3 * BLOCK), 1)
        band = (kj >= qi) & (kj <= qi + 2 * BLOCK)
    out_low = lax.broadcasted_iota(jnp.int32, (BLOCK, LANES), 1) < HEAD_DIM

    for jb in range(nb):
        rows = slice(jb * BLOCK, (jb + 1) * BLOCK)
        for g in range(A_GROUPS):
            sl = slice(g * LANES, (g + 1) * LANES)
            sv = _dot(ws_ref[g], vln_scr[rows, sl]) + bs_ref[g]
            mix_scr[rows, sl] = (ga_scr[rows, sl] * sv).astype(BF16)
        if latent:
            mask = band
            if jb == 0:
                mask = mask & (kj >= jnp.where(i > 0, 0, BLOCK))
            if jb == nb - 1:
                mask = mask & (kj < jnp.where(i < n_blocks_total // nb - 1, 3 * BLOCK, 2 * BLOCK))
        for kvh in range(B_KV_HEADS):
            ks = slice(kvh * LANES, (kvh + 1) * LANES)
            base = (jb * B_Q_HEADS + kvh * B_GQA) * BLOCK
            q4 = q_scr[base:base + rows4, :]
            sink = jnp.concatenate(
                [jnp.full((BLOCK, 1), sink_ref[kvh * B_GQA + j], F32) for j in range(B_GQA)], axis=0)
            s_c = _dot_t(q4, kctx_ref[:, ks])
            m = jnp.maximum(sink, jnp.max(s_c, axis=-1, keepdims=True))
            if latent:
                s_l = jnp.where(mask, _dot_t(q4, kbuf[jb * BLOCK:(jb + 3) * BLOCK, ks]), MASKED)
                m = jnp.maximum(m, jnp.max(s_l, axis=-1, keepdims=True))
            p_c = jnp.exp(s_c - m)
            den = jnp.exp(sink - m) + jnp.sum(p_c, axis=-1, keepdims=True)
            o = _dot(p_c.astype(BF16), vctx_ref[:, ks])
            if latent:
                p_l = jnp.exp(s_l - m)
                den = den + jnp.sum(p_l, axis=-1, keepdims=True)
                o = o + _dot(p_l.astype(BF16), vbuf[jb * BLOCK:(jb + 3) * BLOCK, ks])
            o = o * (1.0 / den)
            for hp in range(B_GQA // 2):
                pair_out = jnp.where(out_low, o[2 * hp * BLOCK:(2 * hp + 1) * BLOCK],
                                     o[(2 * hp + 1) * BLOCK:(2 * hp + 2) * BLOCK])
                col = (kvh * (B_GQA // 2) + hp) * LANES
                yb = pair_out * gb_scr[rows, col:col + LANES]
                mix_scr[rows, A_WIDTH + col:A_WIDTH + col + LANES] = yb.astype(BF16)

    y = _dot(mix_scr[...], wout_ref[...])
    o_ref[...] = x + gate * y


def _even_mix(x, mod_l, mod_row, g, w_in, w_out, lng, lnb, ws, bs, sink, kctx, vctx,
              cosf=None, sinf=None, k=None, v=None, *, tile):
    latent = k is not None
    bsz, L, _ = x.shape
    n = L // tile
    nb = tile // BLOCK
    nblk = L // BLOCK
    row = (lambda b: b) if mod_row is None else (lambda b: mod_row)
    full = lambda shape: pl.BlockSpec(shape, lambda b, i: (0,) * len(shape))
    in_specs = [
        pl.BlockSpec((None, tile, D_MODEL), lambda b, i: (b, i, 0)),
        pl.BlockSpec((None, 1, 3 * D_MODEL), lambda b, i: (row(b), 0, 0)),
        full((1, D_MODEL)),
        full(w_in.shape),
        full(w_out.shape),
        full((1, A_WIDTH)),
        full((1, A_WIDTH)),
        full(ws.shape),
        full(bs.shape),
        pl.BlockSpec(memory_space=pltpu.SMEM),
    ]
    args = [x, mod_l, g, w_in, w_out, lng, lnb, ws, bs, sink]
    scratch = [
        pltpu.VMEM((tile, A_WIDTH), F32),
        pltpu.VMEM((tile, B_WIDTH), F32),
        pltpu.VMEM((tile, A_WIDTH), BF16),
        pltpu.VMEM((nb * B_Q_HEADS * BLOCK, LANES), BF16),
        pltpu.VMEM((tile, A_WIDTH + B_WIDTH), BF16),
    ]
    if latent:
        kv_prev = pl.BlockSpec((None, BLOCK, KV_DUP), lambda b, i: (b, jnp.maximum(i * nb - 1, 0), 0))
        kv_cur = pl.BlockSpec((None, tile, KV_DUP), lambda b, i: (b, i, 0))
        kv_next = pl.BlockSpec((None, BLOCK, KV_DUP), lambda b, i: (b, jnp.minimum((i + 1) * nb, nblk - 1), 0))
        in_specs += [
            pl.BlockSpec((tile, LANES), lambda b, i: (i, 0)),
            pl.BlockSpec((tile, LANES), lambda b, i: (i, 0)),
            kv_prev, kv_cur, kv_next, kv_prev, kv_cur, kv_next,
        ]
        args += [cosf, sinf, k, k, k, v, v, v]
        scratch += [pltpu.VMEM((tile + 2 * BLOCK, KV_DUP), BF16)] * 2
    ctx_len = kctx.shape[1]
    in_specs += [pl.BlockSpec((None, ctx_len, KV_DUP), lambda b, i: (b, 0, 0))] * 2
    args += [kctx, vctx]
    return pl.pallas_call(
        functools.partial(_even_kernel, tile=tile, n_blocks_total=nblk, latent=latent),
        grid=(bsz, n),
        in_specs=in_specs,
        out_specs=pl.BlockSpec((None, tile, D_MODEL), lambda b, i: (b, i, 0)),
        out_shape=jax.ShapeDtypeStruct(x.shape, F32),
        scratch_shapes=scratch,
        compiler_params=pltpu.CompilerParams(
            dimension_semantics=("parallel", "parallel"), vmem_limit_bytes=VMEM_LIMIT),
        name="even_mix" if latent else "even_mix_ctx",
    )(*args)


def _odd_proj_kernel(xp_ref, x_ref, xn_ref, m_ref, g_ref, wx_ref, wg_ref, cw_ref, cb_ref,
                     z_ref, sg_ref, xr_scr, *, tile, n_tiles):
    i = pl.program_id(1)
    shift, scale, _ = _mod_parts(m_ref[...])
    xe = jnp.concatenate([xp_ref[...], x_ref[...], xn_ref[...]], axis=0)
    he = _norm_mod(xe, g_ref[...] * (1.0 + scale), shift).astype(BF16)
    xr = _dot(he, wx_ref[...])
    row = lax.broadcasted_iota(jnp.int32, (tile + 2 * HALO, 1), 0)
    lo = jnp.where(i > 0, 0, HALO)
    hi = jnp.where(i < n_tiles - 1, tile + 2 * HALO, tile + HALO)
    xr_scr[...] = jnp.where((row >= lo) & (row < hi), xr, 0.0)
    z = cb_ref[...]
    for j in range(CONV_W):
        off = HALO - CONV_LEFT + j
        z = z + cw_ref[j:j + 1, :] * xr_scr[off:off + tile, :]
    z_ref[...] = z
    sg_ref[...] = _silu(_dot(he[HALO:HALO + tile], wg_ref[...]))


def _odd_project(x, mod_l, mod_row, g, w_x, w_g, conv_w, conv_b, *, tile):
    bsz, L, _ = x.shape
    n = L // tile
    hb = tile // HALO
    nh = L // HALO
    row = (lambda b: b) if mod_row is None else (lambda b: mod_row)
    full = lambda shape: pl.BlockSpec(shape, lambda b, i: (0,) * len(shape))
    return pl.pallas_call(
        functools.partial(_odd_proj_kernel, tile=tile, n_tiles=n),
        grid=(bsz, n),
        in_specs=[
            pl.BlockSpec((None, HALO, D_MODEL), lambda b, i: (b, jnp.maximum(i * hb - 1, 0), 0)),
            pl.BlockSpec((None, tile, D_MODEL), lambda b, i: (b, i, 0)),
            pl.BlockSpec((None, HALO, D_MODEL), lambda b, i: (b, jnp.minimum((i + 1) * hb, nh - 1), 0)),
            pl.BlockSpec((None, 1, 3 * D_MODEL), lambda b, i: (row(b), 0, 0)),
            full((1, D_MODEL)),
            full(w_x.shape),
            full(w_g.shape),
            full((CONV_W, C_WIDTH)),
            full((1, C_WIDTH)),
        ],
        out_specs=[
            pl.BlockSpec((None, tile, C_WIDTH), lambda b, i: (b, i, 0)),
            pl.BlockSpec((None, tile, C_WIDTH), lambda b, i: (b, i, 0)),
        ],
        out_shape=[jax.ShapeDtypeStruct((bsz, L, C_WIDTH), F32)] * 2,
        scratch_shapes=[pltpu.VMEM((tile + 2 * HALO, C_WIDTH), F32)],
        compiler_params=pltpu.CompilerParams(
            dimension_semantics=("parallel", "parallel"), vmem_limit_bytes=VMEM_LIMIT),
        name="odd_proj",
    )(x, x, x, mod_l, g, w_x, w_g, conv_w, conv_b)


def _gates_to_slabs(z_ref, wa_ref, wi_ref, ba_ref, bi_ref, lam_ref, a_scr, bx_ref, *, bsz, tile):
    zz = z_ref[...].reshape(bsz * tile, C_WIDTH)
    lam = lam_ref[...]
    neg = -lam
    softplus = jnp.maximum(neg, 0.0) + jnp.log1p(jnp.exp(-jnp.abs(neg)))
    nsp = -LRU_C * softplus
    for hd in range(C_HEADS):
        cols = slice(hd * C_BLOCK, (hd + 1) * C_BLOCK)
        zc = zz[:, cols]
        zb = zc.astype(BF16)
        r = _sigmoid(_dot(zb, wa_ref[hd]) + ba_ref[:, cols])
        ig = _sigmoid(_dot(zb, wi_ref[hd]) + bi_ref[:, cols])
        log_a = r * nsp[:, cols]
        a = jnp.exp(log_a)
        bx = jnp.sqrt(-jnp.tanh(log_a) * (a * a + 1.0)) * (ig * zc)
        for half in range(C_BLOCK // LANES):
            slab = hd * (C_BLOCK // LANES) + half
            ls = slice(half * LANES, (half + 1) * LANES)
            for b in range(bsz):
                rs = slice(b * tile, (b + 1) * tile)
                a_scr[slab, pl.ds(b, tile, stride=bsz), :] = a[rs, ls]
                bx_ref[slab, pl.ds(b, tile, stride=bsz), :] = bx[rs, ls]


def _scan_tile(a_scr, bx_ref, h_scr, *, tile, reverse):
    def body(kk, hs):
        t = (tile - 1 - kk) if reverse else kk
        r = pl.multiple_of(t * SUBLANES, SUBLANES)
        new = []
        for c in range(N_SLABS):
            hc = a_scr[c, pl.ds(r, SUBLANES), :] * hs[c] + bx_ref[c, pl.ds(r, SUBLANES), :]
            bx_ref[c, pl.ds(r, SUBLANES), :] = hc
            new.append(hc)
        return tuple(new)

    hs = lax.fori_loop(0, tile, body, tuple(h_scr[c] for c in range(N_SLABS)), unroll=4)
    for c in range(N_SLABS):
        h_scr[c] = hs[c]


def _scan_rev_kernel(z_ref, wa_ref, wi_ref, ba_ref, bi_ref, lam_ref, h0_ref,
                     s_ref, hfin_ref, a_scr, h_scr, *, bsz, tile):
    @pl.when(pl.program_id(0) == 0)
    def _():
        h_scr[...] = h0_ref[...]

    _gates_to_slabs(z_ref, wa_ref, wi_ref, ba_ref, bi_ref, lam_ref, a_scr, s_ref, bsz=bsz, tile=tile)
    _scan_tile(a_scr, s_ref, h_scr, tile=tile, reverse=True)
    hfin_ref[...] = h_scr[...]


def _scan_fwd_kernel(z_ref, srev_ref, sg_ref, x_ref, m_ref, wa_ref, wi_ref, ba_ref, bi_ref, lam_ref,
                     h0_ref, wout_ref, fg_ref, o_ref, hfin_ref, a_scr, bx_scr, h_scr, mix_scr,
                     *, bsz, tile, mod_row, final_norm):
    @pl.when(pl.program_id(0) == 0)
    def _():
        h_scr[...] = h0_ref[...]

    _gates_to_slabs(z_ref, wa_ref, wi_ref, ba_ref, bi_ref, lam_ref, a_scr, bx_scr, bsz=bsz, tile=tile)
    _scan_tile(a_scr, bx_scr, h_scr, tile=tile, reverse=False)
    hfin_ref[...] = h_scr[...]

    for b in range(bsz):
        rs = slice(b * tile, (b + 1) * tile)
        for c in range(N_SLABS):
            ls = slice(c * LANES, (c + 1) * LANES)
            hsum = (bx_scr[c, pl.ds(b, tile, stride=bsz), :]
                    + srev_ref[c, pl.ds(b, tile, stride=bsz), :])
            mix_scr[rs, ls] = (hsum * sg_ref[b, :, ls]).astype(BF16)
    y = _dot(mix_scr[...], wout_ref[...])
    m = m_ref[...]
    for b in range(bsz):
        r = b if mod_row is None else mod_row
        gate = m[r:r + 1, 2 * D_MODEL:3 * D_MODEL]
        xn = x_ref[b] + gate * y[b * tile:(b + 1) * tile]
        if final_norm:
            ms = jnp.mean(xn * xn, axis=-1, keepdims=True)
            xn = xn * lax.rsqrt(ms + EPS) * fg_ref[...]
        o_ref[b] = xn


def _gate_specs():
    full = lambda shape: pl.BlockSpec(shape, lambda i: (0,) * len(shape))
    return [
        full((C_HEADS, C_BLOCK, C_BLOCK)),
        full((C_HEADS, C_BLOCK, C_BLOCK)),
        full((1, C_WIDTH)),
        full((1, C_WIDTH)),
        full((1, C_WIDTH)),
        full((N_SLABS, SUBLANES, LANES)),
    ]


def _scan_reverse(z, w_a, w_i, b_a, b_i, lam, h0):
    bsz, L, _ = z.shape
    tile = T_SCAN
    n = L // tile
    rev = lambda i: (0, n - 1 - i, 0)
    return pl.pallas_call(
        functools.partial(_scan_rev_kernel, bsz=bsz, tile=tile),
        grid=(n,),
        in_specs=[pl.BlockSpec((bsz, tile, C_WIDTH), rev)] + _gate_specs(),
        out_specs=[
            pl.BlockSpec((N_SLABS, tile * bsz, LANES), rev),
            pl.BlockSpec((N_SLABS, SUBLANES, LANES), lambda i: (0, 0, 0)),
        ],
        out_shape=[
            jax.ShapeDtypeStruct((N_SLABS, L * bsz, LANES), F32),
            jax.ShapeDtypeStruct((N_SLABS, SUBLANES, LANES), F32),
        ],
        scratch_shapes=[
            pltpu.VMEM((N_SLABS, tile * bsz, LANES), F32),
            pltpu.VMEM((N_SLABS, SUBLANES, LANES), F32),
        ],
        compiler_params=pltpu.CompilerParams(
            dimension_semantics=("arbitrary",), vmem_limit_bytes=VMEM_LIMIT),
        name="odd_scan_rev",
    )(z, w_a, w_i, b_a, b_i, lam, h0)


def _scan_forward(z, s_rev, sg, x, mod_l, mod_row, w_a, w_i, b_a, b_i, lam, h0, w_out, final_g):
    bsz, L, _ = z.shape
    tile = T_SCAN
    n = L // tile
    fwd = lambda i: (0, i, 0)
    full = lambda shape: pl.BlockSpec(shape, lambda i: (0,) * len(shape))
    final_norm = final_g is not None
    fg = final_g if final_norm else jnp.ones((1, D_MODEL), F32)
    return pl.pallas_call(
        functools.partial(_scan_fwd_kernel, bsz=bsz, tile=tile, mod_row=mod_row, final_norm=final_norm),
        grid=(n,),
        in_specs=[
            pl.BlockSpec((bsz, tile, C_WIDTH), fwd),
            pl.BlockSpec((N_SLABS, tile * bsz, LANES), fwd),
            pl.BlockSpec((bsz, tile, C_WIDTH), fwd),
            pl.BlockSpec((bsz, tile, D_MODEL), fwd),
            full((MOD_ROWS, 3 * D_MODEL)),
        ] + _gate_specs() + [full((C_WIDTH, D_MODEL)), full((1, D_MODEL))],
        out_specs=[
            pl.BlockSpec((bsz, tile, D_MODEL), fwd),
            pl.BlockSpec((N_SLABS, SUBLANES, LANES), lambda i: (0, 0, 0)),
        ],
        out_shape=[
            jax.ShapeDtypeStruct(x.shape, F32),
            jax.ShapeDtypeStruct((N_SLABS, SUBLANES, LANES), F32),
        ],
        scratch_shapes=[
            pltpu.VMEM((N_SLABS, tile * bsz, LANES), F32),
            pltpu.VMEM((N_SLABS, tile * bsz, LANES), F32),
            pltpu.VMEM((N_SLABS, SUBLANES, LANES), F32),
            pltpu.VMEM((bsz * tile, C_WIDTH), BF16),
        ],
        compiler_params=pltpu.CompilerParams(
            dimension_semantics=("arbitrary",), vmem_limit_bytes=VMEM_LIMIT),
        name="odd_scan_fwd",
    )(z, s_rev, sg, x, mod_l, w_a, w_i, b_a, b_i, lam, h0, w_out, fg)


def _rope_tables(L):
    rows = L // GRID_W
    r, col = jnp.meshgrid(jnp.arange(rows), jnp.arange(GRID_W), indexing="ij")
    r = r.reshape(-1).astype(F32)
    col = col.reshape(-1).astype(F32)
    n_freq = HEAD_DIM // 4
    inv_freq = ROPE_THETA ** (-jnp.arange(n_freq, dtype=F32) / n_freq)
    ang = jnp.concatenate([r[:, None] * inv_freq, col[:, None] * inv_freq], axis=-1)
    cos, sin = jnp.cos(ang), jnp.sin(ang)
    cosf = jnp.tile(jnp.concatenate([cos, cos], axis=-1), (1, LANES // HEAD_DIM))
    sinf = jnp.tile(jnp.concatenate([-sin, sin], axis=-1), (1, LANES // HEAD_DIM))
    return cosf, sinf


def _dup_heads(w):
    return jnp.concatenate([w[:, :HEAD_DIM], w[:, :HEAD_DIM], w[:, HEAD_DIM:], w[:, HEAD_DIM:]], axis=1)


def kernel(x, c, ctx, c_ctx, norm_g, w_mod, b_mod, ab_w_in, a_ln_g, a_ln_b, a_w_s, a_b_s, b_sink,
           ab_w_out, c_w_in, c_conv_w, c_conv_b, c_w_a, c_b_a, c_w_i, c_b_i, c_lam, c_w_out, final_g):
    bsz, L, d = x.shape
    ctx_len = ctx.shape[1]
    assert d == D_MODEL and bsz == SUBLANES and L % T_PROJ == 0 and L % T_EVEN == 0
    assert ctx_len % BLOCK == 0 and ctx_len % T_SCAN == 0 and DEPTH % 2 == 0
    ctx_row = bsz

    cin = jnp.concatenate([c, c_ctx[None, :], jnp.zeros((MOD_ROWS - bsz - 1, d), F32)], axis=0)
    mod = _modulation(cin, w_mod, b_mod).reshape(DEPTH, MOD_ROWS, 1, 3 * d)
    cosf, sinf = _rope_tables(L)
    zero_state = jnp.zeros((N_SLABS, SUBLANES, LANES), F32)

    xc = ctx
    for layer in range(DEPTH):
        need_ctx = layer < DEPTH - 1
        i = layer // 2
        mod_l = mod[layer]
        g = norm_g[layer][None, :]
        if layer % 2 == 0:
            w_in = ab_w_in[i].astype(BF16)
            w_kv = jnp.concatenate([_dup_heads(w_in[:, AB_K_OFF:AB_K_OFF + KV_WIDTH]),
                                    _dup_heads(w_in[:, AB_K_OFF + KV_WIDTH:AB_GB_OFF])], axis=1)
            w_out = ab_w_out[i].astype(BF16)
            lng = a_ln_g[i][None, :]
            lnb = a_ln_b[i][None, :]
            ws = a_w_s[i].astype(BF16)
            bs = jnp.broadcast_to(a_b_s[i][:, :, None], (A_GROUPS, CHUNK, LANES))
            sink = b_sink[i]
            kc, vc = _kv_project(xc, mod_l, ctx_row, g, w_kv, cosf, sinf, rope=False, tile=ctx_len)
            k, v = _kv_project(x, mod_l, None, g, w_kv, cosf, sinf, rope=True, tile=T_KV)
            x = _even_mix(x, mod_l, None, g, w_in, w_out, lng, lnb, ws, bs, sink, kc, vc,
                          cosf=cosf, sinf=sinf, k=k, v=v, tile=T_EVEN)
            if need_ctx:
                xc = _even_mix(xc, mod_l, ctx_row, g, w_in, w_out, lng, lnb, ws, bs, sink, kc, vc,
                               tile=ctx_len)
        else:
            w_in = c_w_in[i].astype(BF16)
            w_x, w_g = w_in[:, :C_WIDTH], w_in[:, C_WIDTH:]
            w_out = c_w_out[i].astype(BF16)
            conv_w, conv_b = c_conv_w[i], c_conv_b[i][None, :]
            mod_flat = mod_l.reshape(MOD_ROWS, 3 * d)
            gate_args = [(c_w_a[i, dd].astype(BF16), c_w_i[i, dd].astype(BF16), c_b_a[i, dd][None, :],
                          c_b_i[i, dd][None, :], c_lam[i, dd][None, :]) for dd in range(2)]
            z_c, sg_c = _odd_project(xc, mod_l, ctx_row, g, w_x, w_g, conv_w, conv_b, tile=ctx_len)
            z, sg = _odd_project(x, mod_l, None, g, w_x, w_g, conv_w, conv_b, tile=T_PROJ)
            s_rev_c, h0_rev = _scan_reverse(z_c, *gate_args[1], zero_state)
            xc_new, h0_fwd = _scan_forward(z_c, s_rev_c, sg_c, xc, mod_flat, ctx_row, *gate_args[0],
                                           zero_state, w_out, None)
            s_rev, _ = _scan_reverse(z, *gate_args[1], h0_rev)
            fin = final_g[None, :] if layer == DEPTH - 1 else None
            x, _ = _scan_forward(z, s_rev, sg, x, mod_flat, None, *gate_args[0], h0_fwd, w_out, fin)
            if need_ctx:
                xc = xc_new
    return x
```

```python
import functools

import jax
import jax.numpy as jnp
from jax import lax
from jax.experimental import pallas as pl
from jax.experimental.pallas import tpu as pltpu

F32 = jnp.float32
BF16 = jnp.bfloat16

D_MODEL = 1024
DEPTH = 4
GRID_W = 64
EPS = 1e-6
CHUNK = 128
A_GROUPS = 4
A_WIDTH = 512
HEAD_DIM = 64
B_Q_HEADS = 8
B_KV_HEADS = 2
B_GQA = B_Q_HEADS // B_KV_HEADS
B_WIDTH = 512
KV_WIDTH = 128
BLOCK = 128
ROPE_THETA = 10000.0
AB_Q_OFF = 3 * A_WIDTH
AB_K_OFF = 3 * A_WIDTH + B_WIDTH
AB_GB_OFF = AB_K_OFF + 2 * KV_WIDTH
C_WIDTH = 1024
C_HEADS = 4
C_BLOCK = 256
CONV_W = 4
CONV_LEFT = 2
LRU_C = 8.0
MASKED = -1e30

LANES = 128
SUBLANES = 8
N_SLABS = C_WIDTH // LANES
MOD_ROWS = 16
HALO = 16
VMEM_LIMIT = 56 * 1024 * 1024
KV_DUP = 2 * KV_WIDTH

T_KV = 512
T_EVEN = 512
T_PROJ = 512
T_SCAN = 64


def _sigmoid(x):
    return 0.5 * jnp.tanh(0.5 * x) + 0.5


def _silu(x):
    return x * _sigmoid(x)


def _gelu(x):
    return 0.5 * x * (1.0 + jnp.tanh(0.7978845608028654 * (x + 0.044715 * (x * x * x))))


def _norm_mod(x, g1s, shift):
    ms = jnp.mean(x * x, axis=-1, keepdims=True)
    return x * lax.rsqrt(ms + EPS) * g1s + shift


def _mod_parts(m):
    return m[:, 0:D_MODEL], m[:, D_MODEL:2 * D_MODEL], m[:, 2 * D_MODEL:3 * D_MODEL]


def _rope(x, cosf, sinf):
    n = x.shape[-1]
    lane = lax.broadcasted_iota(jnp.int32, x.shape, 1)
    first = (lane % HEAD_DIM) < (HEAD_DIM // 2)
    sw = jnp.where(first, pltpu.roll(x, n - HEAD_DIM // 2, 1), pltpu.roll(x, HEAD_DIM // 2, 1))
    return x * cosf + sw * sinf


def _dot(a, b):
    return jnp.dot(a, b, preferred_element_type=F32)


def _dot_t(a, b):
    return lax.dot_general(a, b, (((1,), (1,)), ((), ())), preferred_element_type=F32)


def _mod_kernel(c_ref, w_ref, b_ref, o_ref):
    s = _silu(c_ref[...])
    o_ref[...] = _dot(s.astype(BF16), w_ref[...].astype(BF16)) + b_ref[...]


def _modulation(cin, w_mod, b_mod):
    nj = 3
    return pl.pallas_call(
        _mod_kernel,
        grid=(DEPTH, nj),
        in_specs=[
            pl.BlockSpec((MOD_ROWS, D_MODEL), lambda l, j: (0, 0)),
            pl.BlockSpec((None, D_MODEL, D_MODEL), lambda l, j: (l, 0, j)),
            pl.BlockSpec((None, 1, D_MODEL), lambda l, j: (l, 0, j)),
        ],
        out_specs=pl.BlockSpec((None, MOD_ROWS, D_MODEL), lambda l, j: (l, 0, j)),
        out_shape=jax.ShapeDtypeStruct((DEPTH, MOD_ROWS, 3 * D_MODEL), F32),
        compiler_params=pltpu.CompilerParams(
            dimension_semantics=("arbitrary", "arbitrary"), vmem_limit_bytes=VMEM_LIMIT),
        name="modulation",
    )(cin, w_mod, b_mod.reshape(DEPTH, 1, 3 * D_MODEL))


def _kv_kernel(x_ref, m_ref, g_ref, w_ref, cos_ref, sin_ref, k_ref, v_ref, *, rope):
    shift, scale, _ = _mod_parts(m_ref[...])
    h = _norm_mod(x_ref[...], g_ref[...] * (1.0 + scale), shift).astype(BF16)
    kv = _dot(h, w_ref[...])
    for s in range(KV_DUP // LANES):
        sl = slice(s * LANES, (s + 1) * LANES)
        k = kv[:, sl]
        if rope:
            k = _rope(k, cos_ref[...], sin_ref[...])
        k_ref[:, sl] = k.astype(BF16)
    v_ref[...] = kv[:, KV_DUP:2 * KV_DUP].astype(BF16)


def _kv_project(x, mod_l, mod_row, g, w_kv, cosf, sinf, *, rope, tile):
    bsz, L, _ = x.shape
    n = L // tile
    row = (lambda b: b) if mod_row is None else (lambda b: mod_row)
    return pl.pallas_call(
        functools.partial(_kv_kernel, rope=rope),
        grid=(bsz, n),
        in_specs=[
            pl.BlockSpec((None, tile, D_MODEL), lambda b, i: (b, i, 0)),
            pl.BlockSpec((None, 1, 3 * D_MODEL), lambda b, i: (row(b), 0, 0)),
            pl.BlockSpec((1, D_MODEL), lambda b, i: (0, 0)),
            pl.BlockSpec((D_MODEL, 2 * KV_DUP), lambda b, i: (0, 0)),
            pl.BlockSpec((tile, LANES), lambda b, i: (i, 0)),
            pl.BlockSpec((tile, LANES), lambda b, i: (i, 0)),
        ],
        out_specs=[
            pl.BlockSpec((None, tile, KV_DUP), lambda b, i: (b, i, 0)),
            pl.BlockSpec((None, tile, KV_DUP), lambda b, i: (b, i, 0)),
        ],
        out_shape=[jax.ShapeDtypeStruct((bsz, L, KV_DUP), BF16)] * 2,
        compiler_params=pltpu.CompilerParams(
            dimension_semantics=("parallel", "parallel"), vmem_limit_bytes=VMEM_LIMIT),
        name="even_kv",
    )(x, mod_l, g, w_kv, cosf, sinf)


def _even_kernel(*refs, tile, n_blocks_total, latent):
    if latent:
        (x_ref, m_ref, g_ref, win_ref, wout_ref, lng_ref, lnb_ref, ws_ref, bs_ref, sink_ref,
         cos_ref, sin_ref, kp_ref, kc_ref, kn_ref, vp_ref, vc_ref, vn_ref, kctx_ref, vctx_ref,
         o_ref, ga_scr, gb_scr, vln_scr, q_scr, mix_scr, kbuf, vbuf) = refs
    else:
        (x_ref, m_ref, g_ref, win_ref, wout_ref, lng_ref, lnb_ref, ws_ref, bs_ref, sink_ref,
         kctx_ref, vctx_ref, o_ref, ga_scr, gb_scr, vln_scr, q_scr, mix_scr) = refs
    nb = tile // BLOCK
    i = pl.program_id(1)
    rows4 = B_GQA * BLOCK

    x = x_ref[...]
    shift, scale, gate = _mod_parts(m_ref[...])
    h = _norm_mod(x, g_ref[...] * (1.0 + scale), shift).astype(BF16)

    u = _dot(h, win_ref[:, 0:A_WIDTH])
    gate_a = _dot(h, win_ref[:, 2 * A_WIDTH:3 * A_WIDTH])
    ga_scr[...] = _gelu(u) * _silu(gate_a)
    v = _gelu(_dot(h, win_ref[:, A_WIDTH:2 * A_WIDTH]))
    for g in range(A_GROUPS):
        sl = slice(g * LANES, (g + 1) * LANES)
        vg = v[:, sl]
        mu = jnp.mean(vg, axis=-1, keepdims=True)
        vc = vg - mu
        var = jnp.mean(vc * vc, axis=-1, keepdims=True)
        vln_scr[:, sl] = (vc * lax.rsqrt(var + EPS) * lng_ref[:, sl] + lnb_ref[:, sl]).astype(BF16)
    q = _dot(h, win_ref[:, AB_Q_OFF:AB_Q_OFF + B_WIDTH])
    low_half = lax.broadcasted_iota(jnp.int32, (tile, LANES), 1) < HEAD_DIM
    for pair in range(B_Q_HEADS // 2):
        qs = q[:, pair * LANES:(pair + 1) * LANES]
        if latent:
            qs = _rope(qs, cos_ref[...], sin_ref[...])
        qs = qs * (HEAD_DIM ** -0.5)
        q_lo = jnp.where(low_half, qs, 0.0).astype(BF16)
        q_hi = jnp.where(low_half, 0.0, qs).astype(BF16)
        for jb in range(nb):
            rows = slice(jb * BLOCK, (jb + 1) * BLOCK)
            base = (jb * B_Q_HEADS + 2 * pair) * BLOCK
            q_scr[base:base + BLOCK, :] = q_lo[rows]
            q_scr[base + BLOCK:base + 2 * BLOCK, :] = q_hi[rows]
    gb_scr[...] = _silu(_dot(h, win_ref[:, AB_GB_OFF:AB_GB_OFF + B_WIDTH]))

    if latent:
        kbuf[0:BLOCK, :] = kp_ref[...]
        kbuf[BLOCK:BLOCK + tile, :] = kc_ref[...]
        kbuf[BLOCK + tile:2 * BLOCK + tile, :] = kn_ref[...]
        vbuf[0:BLOCK, :] = vp_ref[...]
        vbuf[BLOCK:BLOCK + tile, :] = vc_ref[...]
        vbuf[BLOCK + tile:2 * BLOCK + tile, :] = vn_ref[...]
        qi = lax.broadcasted_iota(jnp.int32, (rows4, 3 * BLOCK), 0) & (BLOCK - 1)
        kj = lax.broadcasted_iota(jnp.int32, (rows4, 3 * BLOCK), 1)
        band = (kj >= qi) & (kj <= qi + 2 * BLOCK)
    out_low = lax.broadcasted_iota(jnp.int32, (BLOCK, LANES), 1) < HEAD_DIM

    for jb in range(nb):
        rows = slice(jb * BLOCK, (jb + 1) * BLOCK)
        for g in range(A_GROUPS):
            sl = slice(g * LANES, (g + 1) * LANES)
            sv = _dot(ws_ref[g], vln_scr[rows, sl]) + bs_ref[g]
            mix_scr[rows, sl] = (ga_scr[rows, sl] * sv).astype(BF16)
        if latent:
            mask = band
            if jb == 0:
                mask = mask & (kj >= jnp.where(i > 0, 0, BLOCK))
            if jb == nb - 1:
                mask = mask & (kj < jnp.where(i < n_blocks_total // nb - 1, 3 * BLOCK, 2 * BLOCK))
        for kvh in range(B_KV_HEADS):
            ks = slice(kvh * LANES, (kvh + 1) * LANES)
            base = (jb * B_Q_HEADS + kvh * B_GQA) * BLOCK
            q4 = q_scr[base:base + rows4, :]
            sink = jnp.concatenate(
                [jnp.full((BLOCK, 1), sink_ref[kvh * B_GQA + j], F32) for j in range(B_GQA)], axis=0)
            s_c = _dot_t(q4, kctx_ref[:, ks])
            m = jnp.maximum(sink, jnp.max(s_c, axis=-1, keepdims=True))
            if latent:
                s_l = jnp.where(mask, _dot_t(q4, kbuf[jb * BLOCK:(jb + 3) * BLOCK, ks]), MASKED)
                m = jnp.maximum(m, jnp.max(s_l, axis=-1, keepdims=True))
            p_c = jnp.exp(s_c - m)
            den = jnp.exp(sink - m) + jnp.sum(p_c, axis=-1, keepdims=True)
            o = _dot(p_c.astype(BF16), vctx_ref[:, ks])
            if latent:
                p_l = jnp.exp(s_l - m)
                den = den + jnp.sum(p_l, axis=-1, keepdims=True)
                o = o + _dot(p_l.astype(BF16), vbuf[jb * BLOCK:(jb + 3) * BLOCK, ks])
            o = o * (1.0 / den)
            for hp in range(B_GQA // 2):
                pair_out = jnp.where(out_low, o[2 * hp * BLOCK:(2 * hp + 1) * BLOCK],
                                     o[(2 * hp + 1) * BLOCK:(2 * hp + 2) * BLOCK])
                col = (kvh * (B_GQA // 2) + hp) * LANES
                yb = pair_out * gb_scr[rows, col:col + LANES]
                mix_scr[rows, A_WIDTH + col:A_WIDTH + col + LANES] = yb.astype(BF16)

    y = _dot(mix_scr[...], wout_ref[...])
    o_ref[...] = x + gate * y


def _even_mix(x, mod_l, mod_row, g, w_in, w_out, lng, lnb, ws, bs, sink, kctx, vctx,
              cosf=None, sinf=None, k=None, v=None, *, tile):
    latent = k is not None
    bsz, L, _ = x.shape
    n = L // tile
    nb = tile // BLOCK
    nblk = L // BLOCK
    row = (lambda b: b) if mod_row is None else (lambda b: mod_row)
    full = lambda shape: pl.BlockSpec(shape, lambda b, i: (0,) * len(shape))
    in_specs = [
        pl.BlockSpec((None, tile, D_MODEL), lambda b, i: (b, i, 0)),
        pl.BlockSpec((None, 1, 3 * D_MODEL), lambda b, i: (row(b), 0, 0)),
        full((1, D_MODEL)),
        full(w_in.shape),
        full(w_out.shape),
        full((1, A_WIDTH)),
        full((1, A_WIDTH)),
        full(ws.shape),
        full(bs.shape),
        pl.BlockSpec(memory_space=pltpu.SMEM),
    ]
    args = [x, mod_l, g, w_in, w_out, lng, lnb, ws, bs, sink]
    scratch = [
        pltpu.VMEM((tile, A_WIDTH), F32),
        pltpu.VMEM((tile, B_WIDTH), F32),
        pltpu.VMEM((tile, A_WIDTH), BF16),
        pltpu.VMEM((nb * B_Q_HEADS * BLOCK, LANES), BF16),
        pltpu.VMEM((tile, A_WIDTH + B_WIDTH), BF16),
    ]
    if latent:
        kv_prev = pl.BlockSpec((None, BLOCK, KV_DUP), lambda b, i: (b, jnp.maximum(i * nb - 1, 0), 0))
        kv_cur = pl.BlockSpec((None, tile, KV_DUP), lambda b, i: (b, i, 0))
        kv_next = pl.BlockSpec((None, BLOCK, KV_DUP), lambda b, i: (b, jnp.minimum((i + 1) * nb, nblk - 1), 0))
        in_specs += [
            pl.BlockSpec((tile, LANES), lambda b, i: (i, 0)),
            pl.BlockSpec((tile, LANES), lambda b, i: (i, 0)),
            kv_prev, kv_cur, kv_next, kv_prev, kv_cur, kv_next,
        ]
        args += [cosf, sinf, k, k, k, v, v, v]
        scratch += [pltpu.VMEM((tile + 2 * BLOCK, KV_DUP), BF16)] * 2
    ctx_len = kctx.shape[1]
    in_specs += [pl.BlockSpec((None, ctx_len, KV_DUP), lambda b, i: (b, 0, 0))] * 2
    args += [kctx, vctx]
    return pl.pallas_call(
        functools.partial(_even_kernel, tile=tile, n_blocks_total=nblk, latent=latent),
        grid=(bsz, n),
        in_specs=in_specs,
        out_specs=pl.BlockSpec((None, tile, D_MODEL), lambda b, i: (b, i, 0)),
        out_shape=jax.ShapeDtypeStruct(x.shape, F32),
        scratch_shapes=scratch,
        compiler_params=pltpu.CompilerParams(
            dimension_semantics=("parallel", "parallel"), vmem_limit_bytes=VMEM_LIMIT),
        name="even_mix" if latent else "even_mix_ctx",
    )(*args)


def _odd_proj_kernel(x_ref, m_ref, g_ref, wx_ref, wg_ref, xr_ref, sg_ref, *, tile, bsz):
    b = pl.program_id(1)
    shift, scale, _ = _mod_parts(m_ref[...])
    h = _norm_mod(x_ref[...], g_ref[...] * (1.0 + scale), shift).astype(BF16)
    xr = _dot(h, wx_ref[...])
    for c in range(N_SLABS):
        xr_ref[c, pl.ds(b, tile, stride=bsz), :] = xr[:, c * LANES:(c + 1) * LANES]
    sg_ref[...] = _silu(_dot(h, wg_ref[...])).astype(BF16)


def _odd_project(x, mod_l, mod_row, g, w_x, w_g, *, tile):
    bsz, L, _ = x.shape
    n = L // tile
    row = (lambda b: b) if mod_row is None else (lambda b: mod_row)
    full = lambda shape: pl.BlockSpec(shape, lambda i, b: (0,) * len(shape))
    return pl.pallas_call(
        functools.partial(_odd_proj_kernel, tile=tile, bsz=bsz),
        grid=(n, bsz),
        in_specs=[
            pl.BlockSpec((None, tile, D_MODEL), lambda i, b: (b, i, 0)),
            pl.BlockSpec((None, 1, 3 * D_MODEL), lambda i, b: (row(b), 0, 0)),
            full((1, D_MODEL)),
            full(w_x.shape),
            full(w_g.shape),
        ],
        out_specs=[
            pl.BlockSpec((N_SLABS, tile * bsz, LANES), lambda i, b: (0, i, 0)),
            pl.BlockSpec((None, tile, C_WIDTH), lambda i, b: (b, i, 0)),
        ],
        out_shape=[
            jax.ShapeDtypeStruct((N_SLABS, L * bsz, LANES), F32),
            jax.ShapeDtypeStruct((bsz, L, C_WIDTH), BF16),
        ],
        compiler_params=pltpu.CompilerParams(
            dimension_semantics=("arbitrary", "arbitrary"), vmem_limit_bytes=VMEM_LIMIT),
        name="odd_proj",
    )(x, mod_l, g, w_x, w_g)


def _decay_rate(lam_ref):
    neg = -lam_ref[...]
    return -LRU_C * (jnp.maximum(neg, 0.0) + jnp.log1p(jnp.exp(-jnp.abs(neg))))


def _gates(zc, zb, hd, wa_ref, wi_ref, ba_ref, bi_ref, nsp, a_scr, bx_ref):
    cols = slice(hd * C_BLOCK, (hd + 1) * C_BLOCK)
    r = 0.5 * jnp.tanh(_dot(zb, wa_ref[hd]) + ba_ref[:, cols]) + 0.5
    ig = 0.5 * jnp.tanh(_dot(zb, wi_ref[hd]) + bi_ref[:, cols]) + 0.5
    log_a = r * nsp[:, cols]
    a = jnp.exp(log_a)
    bx = jnp.sqrt(-jnp.tanh(log_a) * (a * a + 1.0)) * (ig * zc)
    for half in range(C_BLOCK // LANES):
        slab = hd * (C_BLOCK // LANES) + half
        ls = slice(half * LANES, (half + 1) * LANES)
        a_scr[slab] = a[:, ls]
        bx_ref[slab] = bx[:, ls]


def _scan_tile(a_scr, bx_ref, h_scr, *, tile, reverse):
    def body(kk, hs):
        t = (tile - 1 - kk) if reverse else kk
        r = pl.multiple_of(t * SUBLANES, SUBLANES)
        new = []
        for c in range(N_SLABS):
            hc = a_scr[c, pl.ds(r, SUBLANES), :] * hs[c] + bx_ref[c, pl.ds(r, SUBLANES), :]
            bx_ref[c, pl.ds(r, SUBLANES), :] = hc
            new.append(hc)
        return tuple(new)

    hs = lax.fori_loop(0, tile, body, tuple(h_scr[c] for c in range(N_SLABS)), unroll=4)
    for c in range(N_SLABS):
        h_scr[c] = hs[c]


def _scan_rev_kernel(xp_ref, xc_ref, xn_ref, cw_ref, cb_ref, wa_ref, wi_ref, ba_ref, bi_ref, lam_ref,
                     h0_ref, z_ref, s_ref, hfin_ref, a_scr, h_scr, *, bsz, tile, n_tiles):
    step = pl.program_id(0)
    tidx = n_tiles - 1 - step

    @pl.when(step == 0)
    def _():
        h_scr[...] = h0_ref[...]

    nsp = _decay_rate(lam_ref)
    rows = tile * bsz
    for hd in range(C_HEADS):
        zs = []
        for half in range(C_BLOCK // LANES):
            c = hd * (C_BLOCK // LANES) + half
            ls = slice(c * LANES, (c + 1) * LANES)
            ext = jnp.concatenate([jnp.where(tidx > 0, xp_ref[c], 0.0), xc_ref[c],
                                   jnp.where(tidx < n_tiles - 1, xn_ref[c], 0.0)], axis=0)
            z = cb_ref[:, ls]
            for j in range(CONV_W):
                z = z + cw_ref[j:j + 1, ls] * ext[j * bsz:j * bsz + rows]
            zs.append(z)
        zc = jnp.concatenate(zs, axis=1)
        zb = zc.astype(BF16)
        z_ref[:, hd * C_BLOCK:(hd + 1) * C_BLOCK] = zb
        _gates(zc, zb, hd, wa_ref, wi_ref, ba_ref, bi_ref, nsp, a_scr, s_ref)
    _scan_tile(a_scr, s_ref, h_scr, tile=tile, reverse=True)
    hfin_ref[...] = h_scr[...]


def _scan_fwd_kernel(z_ref, srev_ref, sg_ref, x_ref, m_ref, wa_ref, wi_ref, ba_ref, bi_ref, lam_ref,
                     h0_ref, wout_ref, fg_ref, o_ref, hfin_ref, a_scr, bx_scr, h_scr, mix_scr,
                     *, bsz, tile, mod_row, final_norm):
    @pl.when(pl.program_id(0) == 0)
    def _():
        h_scr[...] = h0_ref[...]

    nsp = _decay_rate(lam_ref)
    for hd in range(C_HEADS):
        zb = z_ref[:, hd * C_BLOCK:(hd + 1) * C_BLOCK]
        _gates(zb.astype(F32), zb, hd, wa_ref, wi_ref, ba_ref, bi_ref, nsp, a_scr, bx_scr)
    _scan_tile(a_scr, bx_scr, h_scr, tile=tile, reverse=False)
    hfin_ref[...] = h_scr[...]

    for b in range(bsz):
        rs = slice(b * tile, (b + 1) * tile)
        for c in range(N_SLABS):
            ls = slice(c * LANES, (c + 1) * LANES)
            hsum = (bx_scr[c, pl.ds(b, tile, stride=bsz), :]
                    + srev_ref[c, pl.ds(b, tile, stride=bsz), :])
            mix_scr[rs, ls] = (hsum * sg_ref[b, :, ls]).astype(BF16)
    y = _dot(mix_scr[...], wout_ref[...])
    m = m_ref[...]
    for b in range(bsz):
        r = b if mod_row is None else mod_row
        gate = m[r:r + 1, 2 * D_MODEL:3 * D_MODEL]
        xn = x_ref[b] + gate * y[b * tile:(b + 1) * tile]
        if final_norm:
            ms = jnp.mean(xn * xn, axis=-1, keepdims=True)
            xn = xn * lax.rsqrt(ms + EPS) * fg_ref[...]
        o_ref[b] = xn


def _gate_specs():
    full = lambda shape: pl.BlockSpec(shape, lambda i: (0,) * len(shape))
    return [
        full((C_HEADS, C_BLOCK, C_BLOCK)),
        full((C_HEADS, C_BLOCK, C_BLOCK)),
        full((1, C_WIDTH)),
        full((1, C_WIDTH)),
        full((1, C_WIDTH)),
        full((N_SLABS, SUBLANES, LANES)),
    ]


def _scan_reverse(xr, bsz, conv_w, conv_b, w_a, w_i, b_a, b_i, lam, h0):
    L = xr.shape[1] // bsz
    tile = T_SCAN
    n = L // tile
    full = lambda shape: pl.BlockSpec(shape, lambda i: (0,) * len(shape))
    rev = lambda i: (0, n - 1 - i, 0)
    left = CONV_LEFT * bsz
    right = (CONV_W - 1 - CONV_LEFT) * bsz
    rows = tile * bsz
    return pl.pallas_call(
        functools.partial(_scan_rev_kernel, bsz=bsz, tile=tile, n_tiles=n),
        grid=(n,),
        in_specs=[
            pl.BlockSpec((N_SLABS, left, LANES),
                         lambda i: (0, jnp.maximum((n - 1 - i) * (rows // left) - 1, 0), 0)),
            pl.BlockSpec((N_SLABS, rows, LANES), rev),
            pl.BlockSpec((N_SLABS, right, LANES),
                         lambda i: (0, jnp.minimum((n - i) * (rows // right), L * bsz // right - 1), 0)),
            full((CONV_W, C_WIDTH)),
            full((1, C_WIDTH)),
        ] + _gate_specs(),
        out_specs=[
            pl.BlockSpec((rows, C_WIDTH), lambda i: (n - 1 - i, 0)),
            pl.BlockSpec((N_SLABS, rows, LANES), rev),
            pl.BlockSpec((N_SLABS, SUBLANES, LANES), lambda i: (0, 0, 0)),
        ],
        out_shape=[
            jax.ShapeDtypeStruct((L * bsz, C_WIDTH), BF16),
            jax.ShapeDtypeStruct((N_SLABS, L * bsz, LANES), F32),
            jax.ShapeDtypeStruct((N_SLABS, SUBLANES, LANES), F32),
        ],
        scratch_shapes=[
            pltpu.VMEM((N_SLABS, rows, LANES), F32),
            pltpu.VMEM((N_SLABS, SUBLANES, LANES), F32),
        ],
        compiler_params=pltpu.CompilerParams(
            dimension_semantics=("arbitrary",), vmem_limit_bytes=VMEM_LIMIT),
        name="odd_scan_rev",
    )(xr, xr, xr, conv_w, conv_b, w_a, w_i, b_a, b_i, lam, h0)


def _scan_forward(z, s_rev, sg, x, mod_l, mod_row, w_a, w_i, b_a, b_i, lam, h0, w_out, final_g):
    bsz, L, _ = x.shape
    tile = T_SCAN
    n = L // tile
    fwd = lambda i: (0, i, 0)
    full = lambda shape: pl.BlockSpec(shape, lambda i: (0,) * len(shape))
    final_norm = final_g is not None
    fg = final_g if final_norm else jnp.ones((1, D_MODEL), F32)
    return pl.pallas_call(
        functools.partial(_scan_fwd_kernel, bsz=bsz, tile=tile, mod_row=mod_row, final_norm=final_norm),
        grid=(n,),
        in_specs=[
            pl.BlockSpec((tile * bsz, C_WIDTH), lambda i: (i, 0)),
            pl.BlockSpec((N_SLABS, tile * bsz, LANES), fwd),
            pl.BlockSpec((bsz, tile, C_WIDTH), fwd),
            pl.BlockSpec((bsz, tile, D_MODEL), fwd),
            full((MOD_ROWS, 3 * D_MODEL)),
        ] + _gate_specs() + [full((C_WIDTH, D_MODEL)), full((1, D_MODEL))],
        out_specs=[
            pl.BlockSpec((bsz, tile, D_MODEL), fwd),
            pl.BlockSpec((N_SLABS, SUBLANES, LANES), lambda i: (0, 0, 0)),
        ],
        out_shape=[
            jax.ShapeDtypeStruct(x.shape, F32),
            jax.ShapeDtypeStruct((N_SLABS, SUBLANES, LANES), F32),
        ],
        scratch_shapes=[
            pltpu.VMEM((N_SLABS, tile * bsz, LANES), F32),
            pltpu.VMEM((N_SLABS, tile * bsz, LANES), F32),
            pltpu.VMEM((N_SLABS, SUBLANES, LANES), F32),
            pltpu.VMEM((bsz * tile, C_WIDTH), BF16),
        ],
        compiler_params=pltpu.CompilerParams(
            dimension_semantics=("arbitrary",), vmem_limit_bytes=VMEM_LIMIT),
        name="odd_scan_fwd",
    )(z, s_rev, sg, x, mod_l, w_a, w_i, b_a, b_i, lam, h0, w_out, fg)


def _rope_tables(L):
    rows = L // GRID_W
    r, col = jnp.meshgrid(jnp.arange(rows), jnp.arange(GRID_W), indexing="ij")
    r = r.reshape(-1).astype(F32)
    col = col.reshape(-1).astype(F32)
    n_freq = HEAD_DIM // 4
    inv_freq = ROPE_THETA ** (-jnp.arange(n_freq, dtype=F32) / n_freq)
    ang = jnp.concatenate([r[:, None] * inv_freq, col[:, None] * inv_freq], axis=-1)
    cos, sin = jnp.cos(ang), jnp.sin(ang)
    cosf = jnp.tile(jnp.concatenate([cos, cos], axis=-1), (1, LANES // HEAD_DIM))
    sinf = jnp.tile(jnp.concatenate([-sin, sin], axis=-1), (1, LANES // HEAD_DIM))
    return cosf, sinf


def _dup_heads(w):
    return jnp.concatenate([w[:, :HEAD_DIM], w[:, :HEAD_DIM], w[:, HEAD_DIM:], w[:, HEAD_DIM:]], axis=1)


def kernel(x, c, ctx, c_ctx, norm_g, w_mod, b_mod, ab_w_in, a_ln_g, a_ln_b, a_w_s, a_b_s, b_sink,
           ab_w_out, c_w_in, c_conv_w, c_conv_b, c_w_a, c_b_a, c_w_i, c_b_i, c_lam, c_w_out, final_g):
    bsz, L, d = x.shape
    ctx_len = ctx.shape[1]
    assert d == D_MODEL and bsz == SUBLANES and L % T_PROJ == 0 and L % T_EVEN == 0
    assert ctx_len % BLOCK == 0 and ctx_len % T_SCAN == 0 and DEPTH % 2 == 0
    ctx_row = bsz

    cin = jnp.concatenate([c, c_ctx[None, :], jnp.zeros((MOD_ROWS - bsz - 1, d), F32)], axis=0)
    mod = _modulation(cin, w_mod, b_mod).reshape(DEPTH, MOD_ROWS, 1, 3 * d)
    cosf, sinf = _rope_tables(L)
    zero_state = jnp.zeros((N_SLABS, SUBLANES, LANES), F32)

    xc = ctx
    for layer in range(DEPTH):
        need_ctx = layer < DEPTH - 1
        i = layer // 2
        mod_l = mod[layer]
        g = norm_g[layer][None, :]
        if layer % 2 == 0:
            w_in = ab_w_in[i].astype(BF16)
            w_kv = jnp.concatenate([_dup_heads(w_in[:, AB_K_OFF:AB_K_OFF + KV_WIDTH]),
                                    _dup_heads(w_in[:, AB_K_OFF + KV_WIDTH:AB_GB_OFF])], axis=1)
            w_out = ab_w_out[i].astype(BF16)
            lng = a_ln_g[i][None, :]
            lnb = a_ln_b[i][None, :]
            ws = a_w_s[i].astype(BF16)
            bs = jnp.broadcast_to(a_b_s[i][:, :, None], (A_GROUPS, CHUNK, LANES))
            sink = b_sink[i]
            kc, vc = _kv_project(xc, mod_l, ctx_row, g, w_kv, cosf, sinf, rope=False, tile=ctx_len)
            k, v = _kv_project(x, mod_l, None, g, w_kv, cosf, sinf, rope=True, tile=T_KV)
            x = _even_mix(x, mod_l, None, g, w_in, w_out, lng, lnb, ws, bs, sink, kc, vc,
                          cosf=cosf, sinf=sinf, k=k, v=v, tile=T_EVEN)
            if need_ctx:
                xc = _even_mix(xc, mod_l, ctx_row, g, w_in, w_out, lng, lnb, ws, bs, sink, kc, vc,
                               tile=ctx_len)
        else:
            w_in = c_w_in[i].astype(BF16)
            w_x, w_g = w_in[:, :C_WIDTH], w_in[:, C_WIDTH:]
            w_out = c_w_out[i].astype(BF16)
            conv_w, conv_b = c_conv_w[i], c_conv_b[i][None, :]
            mod_flat = mod_l.reshape(MOD_ROWS, 3 * d)
            gate_args = [((0.5 * c_w_a[i, dd]).astype(BF16), (0.5 * c_w_i[i, dd]).astype(BF16),
                          0.5 * c_b_a[i, dd][None, :], 0.5 * c_b_i[i, dd][None, :], c_lam[i, dd][None, :])
                         for dd in range(2)]
            xr_c, sg_c = _odd_project(xc, mod_l, ctx_row, g, w_x, w_g, tile=ctx_len)
            xr, sg = _odd_project(x, mod_l, None, g, w_x, w_g, tile=T_PROJ)
            z_c, s_rev_c, h0_rev = _scan_reverse(xr_c, bsz, conv_w, conv_b, *gate_args[1], zero_state)
            xc_new, h0_fwd = _scan_forward(z_c, s_rev_c, sg_c, xc, mod_flat, ctx_row, *gate_args[0],
                                           zero_state, w_out, None)
            z, s_rev, _ = _scan_reverse(xr, bsz, conv_w, conv_b, *gate_args[1], h0_rev)
            fin = final_g[None, :] if layer == DEPTH - 1 else None
            x, _ = _scan_forward(z, s_rev, sg, x, mod_flat, None, *gate_args[0], h0_fwd, w_out, fin)
            if need_ctx:
                xc = xc_new
    return x
```

```python
import functools

import jax
import jax.numpy as jnp
from jax import lax
from jax.experimental import pallas as pl
from jax.experimental.pallas import tpu as pltpu

F32 = jnp.float32
BF16 = jnp.bfloat16

D_MODEL = 1024
DEPTH = 4
GRID_W = 64
EPS = 1e-6
CHUNK = 128
A_GROUPS = 4
A_WIDTH = 512
HEAD_DIM = 64
B_Q_HEADS = 8
B_KV_HEADS = 2
B_GQA = B_Q_HEADS // B_KV_HEADS
B_WIDTH = 512
KV_WIDTH = 128
BLOCK = 128
ROPE_THETA = 10000.0
AB_Q_OFF = 3 * A_WIDTH
AB_K_OFF = 3 * A_WIDTH + B_WIDTH
AB_GB_OFF = AB_K_OFF + 2 * KV_WIDTH
C_WIDTH = 1024
C_HEADS = 4
C_BLOCK = 256
CONV_W = 4
CONV_LEFT = 2
LRU_C = 8.0
LOG2_E = 1.4426950408889634
MASKED = -1e30

LANES = 128
SUBLANES = 8
N_SLABS = C_WIDTH // LANES
MOD_ROWS = 16
VMEM_LIMIT = 56 * 1024 * 1024
KV_DUP = 2 * KV_WIDTH

T_EVEN = 512
T_PROJ = 512
T_SCAN = 64


def _sigmoid(x):
    return 0.5 * jnp.tanh(0.5 * x) + 0.5


def _silu(x):
    return x * _sigmoid(x)


def _gelu(x):
    return 0.5 * x * (1.0 + jnp.tanh(0.7978845608028654 * (x + 0.044715 * (x * x * x))))


def _norm_mod(x, g1s, shift):
    ms = jnp.mean(x * x, axis=-1, keepdims=True)
    return x * lax.rsqrt(ms + EPS) * g1s + shift


def _mod_parts(m):
    return m[:, 0:D_MODEL], m[:, D_MODEL:2 * D_MODEL], m[:, 2 * D_MODEL:3 * D_MODEL]


def _rope(x, cosf, sinf):
    n = x.shape[-1]
    lane = lax.broadcasted_iota(jnp.int32, x.shape, 1)
    first = (lane % HEAD_DIM) < (HEAD_DIM // 2)
    sw = jnp.where(first, pltpu.roll(x, n - HEAD_DIM // 2, 1), pltpu.roll(x, HEAD_DIM // 2, 1))
    return x * cosf + sw * sinf


def _dot(a, b):
    return jnp.dot(a, b, preferred_element_type=F32)


def _dot_t(a, b):
    return lax.dot_general(a, b, (((1,), (1,)), ((), ())), preferred_element_type=F32)


def _mod_kernel(c_ref, w_ref, b_ref, o_ref):
    s = _silu(c_ref[...])
    o_ref[...] = _dot(s.astype(BF16), w_ref[...].astype(BF16)) + b_ref[...]


def _modulation(cin, w_mod, b_mod):
    nj = 3
    return pl.pallas_call(
        _mod_kernel,
        grid=(DEPTH, nj),
        in_specs=[
            pl.BlockSpec((MOD_ROWS, D_MODEL), lambda l, j: (0, 0)),
            pl.BlockSpec((None, D_MODEL, D_MODEL), lambda l, j: (l, 0, j)),
            pl.BlockSpec((None, 1, D_MODEL), lambda l, j: (l, 0, j)),
        ],
        out_specs=pl.BlockSpec((None, MOD_ROWS, D_MODEL), lambda l, j: (l, 0, j)),
        out_shape=jax.ShapeDtypeStruct((DEPTH, MOD_ROWS, 3 * D_MODEL), F32),
        compiler_params=pltpu.CompilerParams(
            dimension_semantics=("arbitrary", "arbitrary"), vmem_limit_bytes=VMEM_LIMIT),
        name="modulation",
    )(cin, w_mod, b_mod.reshape(DEPTH, 1, 3 * D_MODEL))


def _ctx_kv_kernel(x_ref, m_ref, g_ref, w_ref, k_ref, v_ref):
    shift, scale, _ = _mod_parts(m_ref[...])
    h = _norm_mod(x_ref[...], g_ref[...] * (1.0 + scale), shift).astype(BF16)
    kv = _dot(h, w_ref[...])
    k_ref[...] = kv[:, 0:KV_DUP].astype(BF16)
    v_ref[...] = kv[:, KV_DUP:2 * KV_DUP].astype(BF16)


def _ctx_kv_project(xc, mod_l, mod_row, g, w_kv):
    bsz, lc, _ = xc.shape
    return pl.pallas_call(
        _ctx_kv_kernel,
        grid=(bsz,),
        in_specs=[
            pl.BlockSpec((None, lc, D_MODEL), lambda b: (b, 0, 0)),
            pl.BlockSpec((None, 1, 3 * D_MODEL), lambda b: (mod_row, 0, 0)),
            pl.BlockSpec((1, D_MODEL), lambda b: (0, 0)),
            pl.BlockSpec((D_MODEL, 2 * KV_DUP), lambda b: (0, 0)),
        ],
        out_specs=[
            pl.BlockSpec((None, lc, KV_DUP), lambda b: (b, 0, 0)),
            pl.BlockSpec((None, lc, KV_DUP), lambda b: (b, 0, 0)),
        ],
        out_shape=[jax.ShapeDtypeStruct((bsz, lc, KV_DUP), BF16)] * 2,
        compiler_params=pltpu.CompilerParams(
            dimension_semantics=("parallel",), vmem_limit_bytes=VMEM_LIMIT),
        name="even_ctx_kv",
    )(xc, mod_l, g, w_kv)


def _even_kernel(*refs, tile, n_blocks_total, latent):
    if latent:
        (x_ref, m_ref, g_ref, win_ref, wout_ref, lng_ref, lnb_ref, ws_ref, bs_ref, sink_ref,
         xp_ref, xn_ref, wkv_ref, cosp_ref, cos_ref, cosn_ref, sinp_ref, sin_ref, sinn_ref,
         kctx_ref, vctx_ref, o_ref, ga_scr, gb_scr, vln_scr, q_scr, mix_scr, kbuf, vbuf) = refs
    else:
        (x_ref, m_ref, g_ref, win_ref, wout_ref, lng_ref, lnb_ref, ws_ref, bs_ref, sink_ref,
         kctx_ref, vctx_ref, o_ref, ga_scr, gb_scr, vln_scr, q_scr, mix_scr) = refs
    nb = tile // BLOCK
    i = pl.program_id(1)
    rows4 = B_GQA * BLOCK

    x = x_ref[...]
    shift, scale, gate = _mod_parts(m_ref[...])
    g1s = g_ref[...] * (1.0 + scale)
    if latent:
        he = _norm_mod(jnp.concatenate([xp_ref[...], x, xn_ref[...]], axis=0), g1s, shift).astype(BF16)
        h = he[BLOCK:BLOCK + tile]
        kv = _dot(he, wkv_ref[...])
        cos_e = jnp.concatenate([cosp_ref[...], cos_ref[...], cosn_ref[...]], axis=0)
        sin_e = jnp.concatenate([sinp_ref[...], sin_ref[...], sinn_ref[...]], axis=0)
        for s in range(KV_DUP // LANES):
            sl = slice(s * LANES, (s + 1) * LANES)
            kbuf[:, sl] = _rope(kv[:, sl], cos_e, sin_e).astype(BF16)
        vbuf[...] = kv[:, KV_DUP:2 * KV_DUP].astype(BF16)
    else:
        h = _norm_mod(x, g1s, shift).astype(BF16)

    u = _dot(h, win_ref[:, 0:A_WIDTH])
    gate_a = _dot(h, win_ref[:, 2 * A_WIDTH:3 * A_WIDTH])
    ga_scr[...] = _gelu(u) * _silu(gate_a)
    v = _gelu(_dot(h, win_ref[:, A_WIDTH:2 * A_WIDTH]))
    for g in range(A_GROUPS):
        sl = slice(g * LANES, (g + 1) * LANES)
        vg = v[:, sl]
        mu = jnp.mean(vg, axis=-1, keepdims=True)
        vc = vg - mu
        var = jnp.mean(vc * vc, axis=-1, keepdims=True)
        vln_scr[:, sl] = (vc * lax.rsqrt(var + EPS) * lng_ref[:, sl] + lnb_ref[:, sl]).astype(BF16)
    q = _dot(h, win_ref[:, AB_Q_OFF:AB_Q_OFF + B_WIDTH])
    low_half = lax.broadcasted_iota(jnp.int32, (tile, LANES), 1) < HEAD_DIM
    for pair in range(B_Q_HEADS // 2):
        qs = q[:, pair * LANES:(pair + 1) * LANES]
        if latent:
            qs = _rope(qs, cos_ref[...], sin_ref[...])
        qs = qs * (HEAD_DIM ** -0.5)
        q_lo = jnp.where(low_half, qs, 0.0).astype(BF16)
        q_hi = jnp.where(low_half, 0.0, qs).astype(BF16)
        for jb in range(nb):
            rows = slice(jb * BLOCK, (jb + 1) * BLOCK)
            base = (jb * B_Q_HEADS + 2 * pair) * BLOCK
            q_scr[base:base + BLOCK, :] = q_lo[rows]
            q_scr[base + BLOCK:base + 2 * BLOCK, :] = q_hi[rows]
    gb_scr[...] = _silu(_dot(h, win_ref[:, AB_GB_OFF:AB_GB_OFF + B_WIDTH]))

    if latent:
        qi = lax.broadcasted_iota(jnp.int32, (rows4, 3 * BLOCK), 0) & (BLOCK - 1)
        kj = lax.broadcasted_iota(jnp.int32, (rows4, 3 * BLOCK), 1)
        band = (kj >= qi) & (kj <= qi + 2 * BLOCK)
    out_low = lax.broadcasted_iota(jnp.int32, (BLOCK, LANES), 1) < HEAD_DIM

    for jb in range(nb):
        rows = slice(jb * BLOCK, (jb + 1) * BLOCK)
        for g in range(A_GROUPS):
            sl = slice(g * LANES, (g + 1) * LANES)
            sv = _dot(ws_ref[g], vln_scr[rows, sl]) + bs_ref[g]
            mix_scr[rows, sl] = (ga_scr[rows, sl] * sv).astype(BF16)
        if latent:
            mask = band
            if jb == 0:
                mask = mask & (kj >= jnp.where(i > 0, 0, BLOCK))
            if jb == nb - 1:
                mask = mask & (kj < jnp.where(i < n_blocks_total // nb - 1, 3 * BLOCK, 2 * BLOCK))
        for kvh in range(B_KV_HEADS):
            ks = slice(kvh * LANES, (kvh + 1) * LANES)
            base = (jb * B_Q_HEADS + kvh * B_GQA) * BLOCK
            q4 = q_scr[base:base + rows4, :]
            sink = jnp.concatenate(
                [jnp.full((BLOCK, 1), sink_ref[kvh * B_GQA + j], F32) for j in range(B_GQA)], axis=0)
            s_c = _dot_t(q4, kctx_ref[:, ks])
            m = jnp.maximum(sink, jnp.max(s_c, axis=-1, keepdims=True))
            if latent:
                s_l = jnp.where(mask, _dot_t(q4, kbuf[jb * BLOCK:(jb + 3) * BLOCK, ks]), MASKED)
                m = jnp.maximum(m, jnp.max(s_l, axis=-1, keepdims=True))
            p_c = jnp.exp(s_c - m)
            den = jnp.exp(sink - m) + jnp.sum(p_c, axis=-1, keepdims=True)
            o = _dot(p_c.astype(BF16), vctx_ref[:, ks])
            if latent:
                p_l = jnp.exp(s_l - m)
                den = den + jnp.sum(p_l, axis=-1, keepdims=True)
                o = o + _dot(p_l.astype(BF16), vbuf[jb * BLOCK:(jb + 3) * BLOCK, ks])
            o = o * (1.0 / den)
            for hp in range(B_GQA // 2):
                pair_out = jnp.where(out_low, o[2 * hp * BLOCK:(2 * hp + 1) * BLOCK],
                                     o[(2 * hp + 1) * BLOCK:(2 * hp + 2) * BLOCK])
                col = (kvh * (B_GQA // 2) + hp) * LANES
                yb = pair_out * gb_scr[rows, col:col + LANES]
                mix_scr[rows, A_WIDTH + col:A_WIDTH + col + LANES] = yb.astype(BF16)

    y = _dot(mix_scr[...], wout_ref[...])
    o_ref[...] = x + gate * y


def _even_mix(x, mod_l, mod_row, g, w_in, w_out, lng, lnb, ws, bs, sink, kctx, vctx,
              cosf=None, sinf=None, w_kv=None, *, tile):
    latent = w_kv is not None
    bsz, L, _ = x.shape
    n = L // tile
    nb = tile // BLOCK
    nblk = L // BLOCK
    row = (lambda b: b) if mod_row is None else (lambda b: mod_row)
    full = lambda shape: pl.BlockSpec(shape, lambda b, i: (0,) * len(shape), pipeline_mode=pl.Buffered(1))
    in_specs = [
        pl.BlockSpec((None, tile, D_MODEL), lambda b, i: (b, i, 0)),
        pl.BlockSpec((None, 1, 3 * D_MODEL), lambda b, i: (row(b), 0, 0)),
        full((1, D_MODEL)),
        full(w_in.shape),
        full(w_out.shape),
        full((1, A_WIDTH)),
        full((1, A_WIDTH)),
        full(ws.shape),
        full(bs.shape),
        pl.BlockSpec(memory_space=pltpu.SMEM),
    ]
    args = [x, mod_l, g, w_in, w_out, lng, lnb, ws, bs, sink]
    scratch = [
        pltpu.VMEM((tile, A_WIDTH), F32),
        pltpu.VMEM((tile, B_WIDTH), F32),
        pltpu.VMEM((tile, A_WIDTH), BF16),
        pltpu.VMEM((nb * B_Q_HEADS * BLOCK, LANES), BF16),
        pltpu.VMEM((tile, A_WIDTH + B_WIDTH), BF16),
    ]
    if latent:
        prev_blk = lambda i: jnp.maximum(i * nb - 1, 0)
        next_blk = lambda i: jnp.minimum((i + 1) * nb, nblk - 1)
        table = [pl.BlockSpec((BLOCK, LANES), lambda b, i: (prev_blk(i), 0)),
                 pl.BlockSpec((tile, LANES), lambda b, i: (i, 0)),
                 pl.BlockSpec((BLOCK, LANES), lambda b, i: (next_blk(i), 0))]
        in_specs += [
            pl.BlockSpec((None, BLOCK, D_MODEL), lambda b, i: (b, prev_blk(i), 0)),
            pl.BlockSpec((None, BLOCK, D_MODEL), lambda b, i: (b, next_blk(i), 0)),
            full(w_kv.shape),
        ] + table + table
        args += [x, x, w_kv, cosf, cosf, cosf, sinf, sinf, sinf]
        scratch += [pltpu.VMEM((tile + 2 * BLOCK, KV_DUP), BF16)] * 2
    ctx_len = kctx.shape[1]
    in_specs += [pl.BlockSpec((None, ctx_len, KV_DUP), lambda b, i: (b, 0, 0))] * 2
    args += [kctx, vctx]
    return pl.pallas_call(
        functools.partial(_even_kernel, tile=tile, n_blocks_total=nblk, latent=latent),
        grid=(bsz, n),
        in_specs=in_specs,
        out_specs=pl.BlockSpec((None, tile, D_MODEL), lambda b, i: (b, i, 0)),
        out_shape=jax.ShapeDtypeStruct(x.shape, F32),
        scratch_shapes=scratch,
        compiler_params=pltpu.CompilerParams(
            dimension_semantics=("parallel", "parallel"), vmem_limit_bytes=VMEM_LIMIT),
        name="even_mix" if latent else "even_mix_ctx",
    )(*args)


def _odd_proj_kernel(x_ref, m_ref, g_ref, wx_ref, wg_ref, xr_ref, sg_ref, *, tile, bsz):
    b = pl.program_id(1)
    shift, scale, _ = _mod_parts(m_ref[...])
    h = _norm_mod(x_ref[...], g_ref[...] * (1.0 + scale), shift).astype(BF16)
    xr = _dot(h, wx_ref[...])
    for c in range(N_SLABS):
        xr_ref[c, pl.ds(b, tile, stride=bsz), :] = xr[:, c * LANES:(c + 1) * LANES]
    sg_ref[...] = _silu(_dot(h, wg_ref[...])).astype(BF16)


def _odd_project(x, mod_l, mod_row, g, w_x, w_g, *, tile):
    bsz, L, _ = x.shape
    n = L // tile
    row = (lambda b: b) if mod_row is None else (lambda b: mod_row)
    full = lambda shape: pl.BlockSpec(shape, lambda i, b: (0,) * len(shape))
    return pl.pallas_call(
        functools.partial(_odd_proj_kernel, tile=tile, bsz=bsz),
        grid=(n, bsz),
        in_specs=[
            pl.BlockSpec((None, tile, D_MODEL), lambda i, b: (b, i, 0)),
            pl.BlockSpec((None, 1, 3 * D_MODEL), lambda i, b: (row(b), 0, 0)),
            full((1, D_MODEL)),
            full(w_x.shape),
            full(w_g.shape),
        ],
        out_specs=[
            pl.BlockSpec((N_SLABS, tile * bsz, LANES), lambda i, b: (0, i, 0)),
            pl.BlockSpec((None, tile, C_WIDTH), lambda i, b: (b, i, 0)),
        ],
        out_shape=[
            jax.ShapeDtypeStruct((N_SLABS, L * bsz, LANES), F32),
            jax.ShapeDtypeStruct((bsz, L, C_WIDTH), BF16),
        ],
        compiler_params=pltpu.CompilerParams(
            dimension_semantics=("arbitrary", "arbitrary"), vmem_limit_bytes=VMEM_LIMIT),
        name="odd_proj",
    )(x, mod_l, g, w_x, w_g)


def _half_decay_rate(lam_ref):
    neg = -lam_ref[...]
    return (0.5 * LRU_C) * (jnp.maximum(neg, 0.0) + jnp.log1p(jnp.exp(-jnp.abs(neg))))


def _gates(hz, hzb, hd, wa_ref, wi_ref, ba_ref, bi_ref, hsp, a_scr, bx_ref):
    cols = slice(hd * C_BLOCK, (hd + 1) * C_BLOCK)
    tr = jnp.tanh(_dot(hzb, wa_ref[hd]) + ba_ref[:, cols])
    ti = jnp.tanh(_dot(hzb, wi_ref[hd]) + bi_ref[:, cols])
    hs = hsp[:, cols]
    nla = tr * hs + hs
    a = jnp.exp2(nla * (-LOG2_E))
    y = jnp.tanh(nla) * (a * a + 1.0)
    root = jnp.where(y > 0.0, y * lax.rsqrt(y), 0.0)
    bx = root * (hz * ti + hz)
    for half in range(C_BLOCK // LANES):
        slab = hd * (C_BLOCK // LANES) + half
        ls = slice(half * LANES, (half + 1) * LANES)
        a_scr[slab] = a[:, ls]
        bx_ref[slab] = bx[:, ls]


def _scan_tile(a_scr, bx_scr, out_ref, h_scr, *, tile, reverse):
    def body(kk, hs):
        t = (tile - 1 - kk) if reverse else kk
        r = pl.multiple_of(t * SUBLANES, SUBLANES)
        new = []
        for c in range(N_SLABS):
            hc = a_scr[c, pl.ds(r, SUBLANES), :] * hs[c] + bx_scr[c, pl.ds(r, SUBLANES), :]
            out_ref[c, pl.ds(r, SUBLANES), :] = hc
            new.append(hc)
        return tuple(new)

    hs = lax.fori_loop(0, tile, body, tuple(h_scr[c] for c in range(N_SLABS)), unroll=4)
    for c in range(N_SLABS):
        h_scr[c] = hs[c]


def _scan_rev_kernel(xp_ref, xc_ref, xn_ref, cw_ref, cb_ref, wa_ref, wi_ref, ba_ref, bi_ref, lam_ref,
                     h0_ref, z_ref, s_ref, hfin_ref, a_scr, bx_scr, h_scr, *, bsz, tile, n_tiles):
    step = pl.program_id(0)
    tidx = n_tiles - 1 - step

    @pl.when(step == 0)
    def _():
        h_scr[...] = h0_ref[...]

    hsp = _half_decay_rate(lam_ref)
    rows = tile * bsz
    for hd in range(C_HEADS):
        zs = []
        for half in range(C_BLOCK // LANES):
            c = hd * (C_BLOCK // LANES) + half
            ls = slice(c * LANES, (c + 1) * LANES)
            ext = jnp.concatenate([jnp.where(tidx > 0, xp_ref[c], 0.0), xc_ref[c],
                                   jnp.where(tidx < n_tiles - 1, xn_ref[c], 0.0)], axis=0)
            z = cb_ref[:, ls]
            for j in range(CONV_W):
                z = z + cw_ref[j:j + 1, ls] * ext[j * bsz:j * bsz + rows]
            zs.append(z)
        hz = jnp.concatenate(zs, axis=1)
        hzb = hz.astype(BF16)
        z_ref[:, hd * C_BLOCK:(hd + 1) * C_BLOCK] = hzb
        _gates(hz, hzb, hd, wa_ref, wi_ref, ba_ref, bi_ref, hsp, a_scr, bx_scr)
    _scan_tile(a_scr, bx_scr, s_ref, h_scr, tile=tile, reverse=True)
    hfin_ref[...] = h_scr[...]


def _scan_fwd_kernel(z_ref, srev_ref, sg_ref, x_ref, m_ref, wa_ref, wi_ref, ba_ref, bi_ref, lam_ref,
                     h0_ref, wout_ref, fg_ref, o_ref, hfin_ref, a_scr, bx_scr, hs_scr, h_scr, mix_scr,
                     *, bsz, tile, mod_row, final_norm):
    @pl.when(pl.program_id(0) == 0)
    def _():
        h_scr[...] = h0_ref[...]

    hsp = _half_decay_rate(lam_ref)
    for hd in range(C_HEADS):
        hzb = z_ref[:, hd * C_BLOCK:(hd + 1) * C_BLOCK]
        _gates(hzb.astype(F32), hzb, hd, wa_ref, wi_ref, ba_ref, bi_ref, hsp, a_scr, bx_scr)
    _scan_tile(a_scr, bx_scr, hs_scr, h_scr, tile=tile, reverse=False)
    hfin_ref[...] = h_scr[...]

    for b in range(bsz):
        rs = slice(b * tile, (b + 1) * tile)
        for c in range(N_SLABS):
            ls = slice(c * LANES, (c + 1) * LANES)
            hsum = (hs_scr[c, pl.ds(b, tile, stride=bsz), :]
                    + srev_ref[c, pl.ds(b, tile, stride=bsz), :])
            mix_scr[rs, ls] = (hsum * sg_ref[b, :, ls]).astype(BF16)
    y = _dot(mix_scr[...], wout_ref[...])
    m = m_ref[...]
    for b in range(bsz):
        r = b if mod_row is None else mod_row
        gate = m[r:r + 1, 2 * D_MODEL:3 * D_MODEL]
        xn = x_ref[b] + gate * y[b * tile:(b + 1) * tile]
        if final_norm:
            ms = jnp.mean(xn * xn, axis=-1, keepdims=True)
            xn = xn * lax.rsqrt(ms + EPS) * fg_ref[...]
        o_ref[b] = xn


def _gate_specs():
    full = lambda shape: pl.BlockSpec(shape, lambda i: (0,) * len(shape))
    return [
        full((C_HEADS, C_BLOCK, C_BLOCK)),
        full((C_HEADS, C_BLOCK, C_BLOCK)),
        full((1, C_WIDTH)),
        full((1, C_WIDTH)),
        full((1, C_WIDTH)),
        full((N_SLABS, SUBLANES, LANES)),
    ]


def _scan_reverse(xr, bsz, conv_w, conv_b, w_a, w_i, b_a, b_i, lam, h0):
    L = xr.shape[1] // bsz
    tile = T_SCAN
    n = L // tile
    full = lambda shape: pl.BlockSpec(shape, lambda i: (0,) * len(shape))
    rev = lambda i: (0, n - 1 - i, 0)
    left = CONV_LEFT * bsz
    right = (CONV_W - 1 - CONV_LEFT) * bsz
    rows = tile * bsz
    return pl.pallas_call(
        functools.partial(_scan_rev_kernel, bsz=bsz, tile=tile, n_tiles=n),
        grid=(n,),
        in_specs=[
            pl.BlockSpec((N_SLABS, left, LANES),
                         lambda i: (0, jnp.maximum((n - 1 - i) * (rows // left) - 1, 0), 0)),
            pl.BlockSpec((N_SLABS, rows, LANES), rev),
            pl.BlockSpec((N_SLABS, right, LANES),
                         lambda i: (0, jnp.minimum((n - i) * (rows // right), L * bsz // right - 1), 0)),
            full((CONV_W, C_WIDTH)),
            full((1, C_WIDTH)),
        ] + _gate_specs(),
        out_specs=[
            pl.BlockSpec((rows, C_WIDTH), lambda i: (n - 1 - i, 0)),
            pl.BlockSpec((N_SLABS, rows, LANES), rev),
            pl.BlockSpec((N_SLABS, SUBLANES, LANES), lambda i: (0, 0, 0)),
        ],
        out_shape=[
            jax.ShapeDtypeStruct((L * bsz, C_WIDTH), BF16),
            jax.ShapeDtypeStruct((N_SLABS, L * bsz, LANES), F32),
            jax.ShapeDtypeStruct((N_SLABS, SUBLANES, LANES), F32),
        ],
        scratch_shapes=[
            pltpu.VMEM((N_SLABS, rows, LANES), F32),
            pltpu.VMEM((N_SLABS, rows, LANES), F32),
            pltpu.VMEM((N_SLABS, SUBLANES, LANES), F32),
        ],
        compiler_params=pltpu.CompilerParams(
            dimension_semantics=("arbitrary",), vmem_limit_bytes=VMEM_LIMIT),
        name="odd_scan_rev",
    )(xr, xr, xr, conv_w, conv_b, w_a, w_i, b_a, b_i, lam, h0)


def _scan_forward(z, s_rev, sg, x, mod_l, mod_row, w_a, w_i, b_a, b_i, lam, h0, w_out, final_g):
    bsz, L, _ = x.shape
    tile = T_SCAN
    n = L // tile
    fwd = lambda i: (0, i, 0)
    full = lambda shape: pl.BlockSpec(shape, lambda i: (0,) * len(shape))
    final_norm = final_g is not None
    fg = final_g if final_norm else jnp.ones((1, D_MODEL), F32)
    return pl.pallas_call(
        functools.partial(_scan_fwd_kernel, bsz=bsz, tile=tile, mod_row=mod_row, final_norm=final_norm),
        grid=(n,),
        in_specs=[
            pl.BlockSpec((tile * bsz, C_WIDTH), lambda i: (i, 0)),
            pl.BlockSpec((N_SLABS, tile * bsz, LANES), fwd),
            pl.BlockSpec((bsz, tile, C_WIDTH), fwd),
            pl.BlockSpec((bsz, tile, D_MODEL), fwd),
            full((MOD_ROWS, 3 * D_MODEL)),
        ] + _gate_specs() + [full((C_WIDTH, D_MODEL)), full((1, D_MODEL))],
        out_specs=[
            pl.BlockSpec((bsz, tile, D_MODEL), fwd),
            pl.BlockSpec((N_SLABS, SUBLANES, LANES), lambda i: (0, 0, 0)),
        ],
        out_shape=[
            jax.ShapeDtypeStruct(x.shape, F32),
            jax.ShapeDtypeStruct((N_SLABS, SUBLANES, LANES), F32),
        ],
        scratch_shapes=[
            pltpu.VMEM((N_SLABS, tile * bsz, LANES), F32),
            pltpu.VMEM((N_SLABS, tile * bsz, LANES), F32),
            pltpu.VMEM((N_SLABS, tile * bsz, LANES), F32),
            pltpu.VMEM((N_SLABS, SUBLANES, LANES), F32),
            pltpu.VMEM((bsz * tile, C_WIDTH), BF16),
        ],
        compiler_params=pltpu.CompilerParams(
            dimension_semantics=("arbitrary",), vmem_limit_bytes=VMEM_LIMIT),
        name="odd_scan_fwd",
    )(z, s_rev, sg, x, mod_l, w_a, w_i, b_a, b_i, lam, h0, w_out, fg)


def _rope_tables(L):
    rows = L // GRID_W
    r, col = jnp.meshgrid(jnp.arange(rows), jnp.arange(GRID_W), indexing="ij")
    r = r.reshape(-1).astype(F32)
    col = col.reshape(-1).astype(F32)
    n_freq = HEAD_DIM // 4
    inv_freq = ROPE_THETA ** (-jnp.arange(n_freq, dtype=F32) / n_freq)
    ang = jnp.concatenate([r[:, None] * inv_freq, col[:, None] * inv_freq], axis=-1)
    cos, sin = jnp.cos(ang), jnp.sin(ang)
    cosf = jnp.tile(jnp.concatenate([cos, cos], axis=-1), (1, LANES // HEAD_DIM))
    sinf = jnp.tile(jnp.concatenate([-sin, sin], axis=-1), (1, LANES // HEAD_DIM))
    return cosf, sinf


def _dup_heads(w):
    return jnp.concatenate([w[:, :HEAD_DIM], w[:, :HEAD_DIM], w[:, HEAD_DIM:], w[:, HEAD_DIM:]], axis=1)


def kernel(x, c, ctx, c_ctx, norm_g, w_mod, b_mod, ab_w_in, a_ln_g, a_ln_b, a_w_s, a_b_s, b_sink,
           ab_w_out, c_w_in, c_conv_w, c_conv_b, c_w_a, c_b_a, c_w_i, c_b_i, c_lam, c_w_out, final_g):
    bsz, L, d = x.shape
    ctx_len = ctx.shape[1]
    assert d == D_MODEL and bsz == SUBLANES and L % T_PROJ == 0 and L % T_EVEN == 0
    assert ctx_len % BLOCK == 0 and ctx_len % T_SCAN == 0 and DEPTH % 2 == 0
    ctx_row = bsz

    cin = jnp.concatenate([c, c_ctx[None, :], jnp.zeros((MOD_ROWS - bsz - 1, d), F32)], axis=0)
    mod = _modulation(cin, w_mod, b_mod).reshape(DEPTH, MOD_ROWS, 1, 3 * d)
    cosf, sinf = _rope_tables(L)
    zero_state = jnp.zeros((N_SLABS, SUBLANES, LANES), F32)

    xc = ctx
    for layer in range(DEPTH):
        need_ctx = layer < DEPTH - 1
        i = layer // 2
        mod_l = mod[layer]
        g = norm_g[layer][None, :]
        if layer % 2 == 0:
            w_in = ab_w_in[i].astype(BF16)
            w_kv = jnp.concatenate([_dup_heads(w_in[:, AB_K_OFF:AB_K_OFF + KV_WIDTH]),
                                    _dup_heads(w_in[:, AB_K_OFF + KV_WIDTH:AB_GB_OFF])], axis=1)
            w_out = ab_w_out[i].astype(BF16)
            lng = a_ln_g[i][None, :]
            lnb = a_ln_b[i][None, :]
            ws = a_w_s[i].astype(BF16)
            bs = jnp.broadcast_to(a_b_s[i][:, :, None], (A_GROUPS, CHUNK, LANES))
            sink = b_sink[i]
            kc, vc = _ctx_kv_project(xc, mod_l, ctx_row, g, w_kv)
            x = _even_mix(x, mod_l, None, g, w_in, w_out, lng, lnb, ws, bs, sink, kc, vc,
                          cosf=cosf, sinf=sinf, w_kv=w_kv, tile=T_EVEN)
            if need_ctx:
                xc = _even_mix(xc, mod_l, ctx_row, g, w_in, w_out, lng, lnb, ws, bs, sink, kc, vc,
                               tile=ctx_len)
        else:
            w_in = c_w_in[i].astype(BF16)
            w_x, w_g = w_in[:, :C_WIDTH], w_in[:, C_WIDTH:]
            w_out = c_w_out[i].astype(BF16)
            conv_w, conv_b = 0.5 * c_conv_w[i], 0.5 * c_conv_b[i][None, :]
            mod_flat = mod_l.reshape(MOD_ROWS, 3 * d)
            gate_args = [(c_w_a[i, dd].astype(BF16), c_w_i[i, dd].astype(BF16),
                          0.5 * c_b_a[i, dd][None, :], 0.5 * c_b_i[i, dd][None, :], c_lam[i, dd][None, :])
                         for dd in range(2)]
            xr_c, sg_c = _odd_project(xc, mod_l, ctx_row, g, w_x, w_g, tile=ctx_len)
            xr, sg = _odd_project(x, mod_l, None, g, w_x, w_g, tile=T_PROJ)
            z_c, s_rev_c, h0_rev = _scan_reverse(xr_c, bsz, conv_w, conv_b, *gate_args[1], zero_state)
            xc_new, h0_fwd = _scan_forward(z_c, s_rev_c, sg_c, xc, mod_flat, ctx_row, *gate_args[0],
                                           zero_state, w_out, None)
            z, s_rev, _ = _scan_reverse(xr, bsz, conv_w, conv_b, *gate_args[1], h0_rev)
            fin = final_g[None, :] if layer == DEPTH - 1 else None
            x, _ = _scan_forward(z, s_rev, sg, x, mod_flat, None, *gate_args[0], h0_fwd, w_out, fin)
            if need_ctx:
                xc = xc_new
    return x
```

```python
import functools

import jax
import jax.numpy as jnp
from jax import lax
from jax.experimental import pallas as pl
from jax.experimental.pallas import tpu as pltpu

F32 = jnp.float32
BF16 = jnp.bfloat16

D_MODEL = 1024
DEPTH = 4
GRID_W = 64
EPS = 1e-6
CHUNK = 128
A_GROUPS = 4
A_WIDTH = 512
HEAD_DIM = 64
B_Q_HEADS = 8
B_KV_HEADS = 2
B_GQA = B_Q_HEADS // B_KV_HEADS
B_WIDTH = 512
KV_WIDTH = 128
BLOCK = 128
ROPE_THETA = 10000.0
AB_Q_OFF = 3 * A_WIDTH
AB_K_OFF = 3 * A_WIDTH + B_WIDTH
AB_GB_OFF = AB_K_OFF + 2 * KV_WIDTH
C_WIDTH = 1024
C_HEADS = 4
C_BLOCK = 256
CONV_W = 4
CONV_LEFT = 2
LRU_C = 8.0
LOG2_E = 1.4426950408889634
MASKED = -1e30

LANES = 128
SUBLANES = 8
N_SLABS = C_WIDTH // LANES
MOD_ROWS = 16
VMEM_LIMIT = 56 * 1024 * 1024
KV_DUP = 2 * KV_WIDTH

T_EVEN = 512
T_SCAN = 64


def _sigmoid(x):
    return 0.5 * jnp.tanh(0.5 * x) + 0.5


def _silu(x):
    return x * _sigmoid(x)


def _gelu(x):
    return 0.5 * x * (1.0 + jnp.tanh(0.7978845608028654 * (x + 0.044715 * (x * x * x))))


def _norm_mod(x, g1s, shift):
    ms = jnp.mean(x * x, axis=-1, keepdims=True)
    return x * lax.rsqrt(ms + EPS) * g1s + shift


def _mod_parts(m):
    return m[:, 0:D_MODEL], m[:, D_MODEL:2 * D_MODEL], m[:, 2 * D_MODEL:3 * D_MODEL]


def _rope(x, cosf, sinf):
    n = x.shape[-1]
    lane = lax.broadcasted_iota(jnp.int32, x.shape, 1)
    first = (lane % HEAD_DIM) < (HEAD_DIM // 2)
    sw = jnp.where(first, pltpu.roll(x, n - HEAD_DIM // 2, 1), pltpu.roll(x, HEAD_DIM // 2, 1))
    return x * cosf + sw * sinf


def _dot(a, b):
    return jnp.dot(a, b, preferred_element_type=F32)


def _dot_t(a, b):
    return lax.dot_general(a, b, (((1,), (1,)), ((), ())), preferred_element_type=F32)


def _mod_kernel(c_ref, w_ref, b_ref, o_ref):
    s = _silu(c_ref[...])
    o_ref[...] = _dot(s.astype(BF16), w_ref[...].astype(BF16)) + b_ref[...]


def _modulation(cin, w_mod, b_mod):
    nj = 3
    return pl.pallas_call(
        _mod_kernel,
        grid=(DEPTH, nj),
        in_specs=[
            pl.BlockSpec((MOD_ROWS, D_MODEL), lambda l, j: (0, 0)),
            pl.BlockSpec((None, D_MODEL, D_MODEL), lambda l, j: (l, 0, j)),
            pl.BlockSpec((None, 1, D_MODEL), lambda l, j: (l, 0, j)),
        ],
        out_specs=pl.BlockSpec((None, MOD_ROWS, D_MODEL), lambda l, j: (l, 0, j)),
        out_shape=jax.ShapeDtypeStruct((DEPTH, MOD_ROWS, 3 * D_MODEL), F32),
        compiler_params=pltpu.CompilerParams(
            dimension_semantics=("arbitrary", "arbitrary"), vmem_limit_bytes=VMEM_LIMIT),
        name="modulation",
    )(cin, w_mod, b_mod.reshape(DEPTH, 1, 3 * D_MODEL))


def _ctx_kv_kernel(x_ref, m_ref, g_ref, w_ref, k_ref, v_ref):
    shift, scale, _ = _mod_parts(m_ref[...])
    h = _norm_mod(x_ref[...], g_ref[...] * (1.0 + scale), shift).astype(BF16)
    kv = _dot(h, w_ref[...])
    k_ref[...] = kv[:, 0:KV_DUP].astype(BF16)
    v_ref[...] = kv[:, KV_DUP:2 * KV_DUP].astype(BF16)


def _ctx_kv_project(xc, mod_l, mod_row, g, w_kv):
    bsz, lc, _ = xc.shape
    return pl.pallas_call(
        _ctx_kv_kernel,
        grid=(bsz,),
        in_specs=[
            pl.BlockSpec((None, lc, D_MODEL), lambda b: (b, 0, 0)),
            pl.BlockSpec((None, 1, 3 * D_MODEL), lambda b: (mod_row, 0, 0)),
            pl.BlockSpec((1, D_MODEL), lambda b: (0, 0)),
            pl.BlockSpec((D_MODEL, 2 * KV_DUP), lambda b: (0, 0)),
        ],
        out_specs=[
            pl.BlockSpec((None, lc, KV_DUP), lambda b: (b, 0, 0)),
            pl.BlockSpec((None, lc, KV_DUP), lambda b: (b, 0, 0)),
        ],
        out_shape=[jax.ShapeDtypeStruct((bsz, lc, KV_DUP), BF16)] * 2,
        compiler_params=pltpu.CompilerParams(
            dimension_semantics=("parallel",), vmem_limit_bytes=VMEM_LIMIT),
        name="even_ctx_kv",
    )(xc, mod_l, g, w_kv)


def _even_kernel(*refs, tile, n_blocks_total, latent):
    if latent:
        (x_ref, m_ref, g_ref, win_ref, wout_ref, lng_ref, lnb_ref, ws_ref, bs_ref, sink_ref,
         xp_ref, xn_ref, wkv_ref, cosp_ref, cos_ref, cosn_ref, sinp_ref, sin_ref, sinn_ref,
         kctx_ref, vctx_ref, o_ref, ga_scr, gb_scr, vln_scr, q_scr, mix_scr, kbuf, vbuf) = refs
    else:
        (x_ref, m_ref, g_ref, win_ref, wout_ref, lng_ref, lnb_ref, ws_ref, bs_ref, sink_ref,
         kctx_ref, vctx_ref, o_ref, ga_scr, gb_scr, vln_scr, q_scr, mix_scr) = refs
    nb = tile // BLOCK
    i = pl.program_id(1)
    rows4 = B_GQA * BLOCK

    x = x_ref[...]
    shift, scale, gate = _mod_parts(m_ref[...])
    g1s = g_ref[...] * (1.0 + scale)
    if latent:
        he = _norm_mod(jnp.concatenate([xp_ref[...], x, xn_ref[...]], axis=0), g1s, shift).astype(BF16)
        h = he[BLOCK:BLOCK + tile]
        kv = _dot(he, wkv_ref[...])
        cos_e = jnp.concatenate([cosp_ref[...], cos_ref[...], cosn_ref[...]], axis=0)
        sin_e = jnp.concatenate([sinp_ref[...], sin_ref[...], sinn_ref[...]], axis=0)
        for s in range(KV_DUP // LANES):
            sl = slice(s * LANES, (s + 1) * LANES)
            kbuf[:, sl] = _rope(kv[:, sl], cos_e, sin_e).astype(BF16)
        vbuf[...] = kv[:, KV_DUP:2 * KV_DUP].astype(BF16)
    else:
        h = _norm_mod(x, g1s, shift).astype(BF16)

    u = _dot(h, win_ref[:, 0:A_WIDTH])
    gate_a = _dot(h, win_ref[:, 2 * A_WIDTH:3 * A_WIDTH])
    ga_scr[...] = _gelu(u) * _silu(gate_a)
    v = _gelu(_dot(h, win_ref[:, A_WIDTH:2 * A_WIDTH]))
    for g in range(A_GROUPS):
        sl = slice(g * LANES, (g + 1) * LANES)
        vg = v[:, sl]
        mu = jnp.mean(vg, axis=-1, keepdims=True)
        vc = vg - mu
        var = jnp.mean(vc * vc, axis=-1, keepdims=True)
        vln_scr[:, sl] = (vc * lax.rsqrt(var + EPS) * lng_ref[:, sl] + lnb_ref[:, sl]).astype(BF16)
    q = _dot(h, win_ref[:, AB_Q_OFF:AB_Q_OFF + B_WIDTH])
    low_half = lax.broadcasted_iota(jnp.int32, (tile, LANES), 1) < HEAD_DIM
    for pair in range(B_Q_HEADS // 2):
        qs = q[:, pair * LANES:(pair + 1) * LANES]
        if latent:
            qs = _rope(qs, cos_ref[...], sin_ref[...])
        qs = qs * (HEAD_DIM ** -0.5)
        q_lo = jnp.where(low_half, qs, 0.0).astype(BF16)
        q_hi = jnp.where(low_half, 0.0, qs).astype(BF16)
        for jb in range(nb):
            rows = slice(jb * BLOCK, (jb + 1) * BLOCK)
            base = (jb * B_Q_HEADS + 2 * pair) * BLOCK
            q_scr[base:base + BLOCK, :] = q_lo[rows]
            q_scr[base + BLOCK:base + 2 * BLOCK, :] = q_hi[rows]
    gb_scr[...] = _silu(_dot(h, win_ref[:, AB_GB_OFF:AB_GB_OFF + B_WIDTH]))

    if latent:
        qi = lax.broadcasted_iota(jnp.int32, (rows4, 3 * BLOCK), 0) & (BLOCK - 1)
        kj = lax.broadcasted_iota(jnp.int32, (rows4, 3 * BLOCK), 1)
        band = (kj >= qi) & (kj <= qi + 2 * BLOCK)
    out_low = lax.broadcasted_iota(jnp.int32, (BLOCK, LANES), 1) < HEAD_DIM

    for jb in range(nb):
        rows = slice(jb * BLOCK, (jb + 1) * BLOCK)
        for g in range(A_GROUPS):
            sl = slice(g * LANES, (g + 1) * LANES)
            sv = _dot(ws_ref[g], vln_scr[rows, sl]) + bs_ref[g]
            mix_scr[rows, sl] = (ga_scr[rows, sl] * sv).astype(BF16)
        if latent:
            mask = band
            if jb == 0:
                mask = mask & (kj >= jnp.where(i > 0, 0, BLOCK))
            if jb == nb - 1:
                mask = mask & (kj < jnp.where(i < n_blocks_total // nb - 1, 3 * BLOCK, 2 * BLOCK))
        for kvh in range(B_KV_HEADS):
            ks = slice(kvh * LANES, (kvh + 1) * LANES)
            base = (jb * B_Q_HEADS + kvh * B_GQA) * BLOCK
            q4 = q_scr[base:base + rows4, :]
            sink = jnp.concatenate(
                [jnp.full((BLOCK, 1), sink_ref[kvh * B_GQA + j], F32) for j in range(B_GQA)], axis=0)
            s_c = _dot_t(q4, kctx_ref[:, ks])
            m = jnp.maximum(sink, jnp.max(s_c, axis=-1, keepdims=True))
            if latent:
                s_l = jnp.where(mask, _dot_t(q4, kbuf[jb * BLOCK:(jb + 3) * BLOCK, ks]), MASKED)
                m = jnp.maximum(m, jnp.max(s_l, axis=-1, keepdims=True))
            p_c = jnp.exp(s_c - m)
            den = jnp.exp(sink - m) + jnp.sum(p_c, axis=-1, keepdims=True)
            o = _dot(p_c.astype(BF16), vctx_ref[:, ks])
            if latent:
                p_l = jnp.exp(s_l - m)
                den = den + jnp.sum(p_l, axis=-1, keepdims=True)
                o = o + _dot(p_l.astype(BF16), vbuf[jb * BLOCK:(jb + 3) * BLOCK, ks])
            o = o * (1.0 / den)
            for hp in range(B_GQA // 2):
                pair_out = jnp.where(out_low, o[2 * hp * BLOCK:(2 * hp + 1) * BLOCK],
                                     o[(2 * hp + 1) * BLOCK:(2 * hp + 2) * BLOCK])
                col = (kvh * (B_GQA // 2) + hp) * LANES
                yb = pair_out * gb_scr[rows, col:col + LANES]
                mix_scr[rows, A_WIDTH + col:A_WIDTH + col + LANES] = yb.astype(BF16)

    y = _dot(mix_scr[...], wout_ref[...])
    o_ref[...] = x + gate * y


def _even_mix(x, mod_l, mod_row, g, w_in, w_out, lng, lnb, ws, bs, sink, kctx, vctx,
              cosf=None, sinf=None, w_kv=None, *, tile):
    latent = w_kv is not None
    bsz, L, _ = x.shape
    n = L // tile
    nb = tile // BLOCK
    nblk = L // BLOCK
    row = (lambda b: b) if mod_row is None else (lambda b: mod_row)
    full = lambda shape: pl.BlockSpec(shape, lambda b, i: (0,) * len(shape), pipeline_mode=pl.Buffered(1))
    in_specs = [
        pl.BlockSpec((None, tile, D_MODEL), lambda b, i: (b, i, 0)),
        pl.BlockSpec((None, 1, 3 * D_MODEL), lambda b, i: (row(b), 0, 0)),
        full((1, D_MODEL)),
        full(w_in.shape),
        full(w_out.shape),
        full((1, A_WIDTH)),
        full((1, A_WIDTH)),
        full(ws.shape),
        full(bs.shape),
        pl.BlockSpec(memory_space=pltpu.SMEM),
    ]
    args = [x, mod_l, g, w_in, w_out, lng, lnb, ws, bs, sink]
    scratch = [
        pltpu.VMEM((tile, A_WIDTH), F32),
        pltpu.VMEM((tile, B_WIDTH), F32),
        pltpu.VMEM((tile, A_WIDTH), BF16),
        pltpu.VMEM((nb * B_Q_HEADS * BLOCK, LANES), BF16),
        pltpu.VMEM((tile, A_WIDTH + B_WIDTH), BF16),
    ]
    if latent:
        prev_blk = lambda i: jnp.maximum(i * nb - 1, 0)
        next_blk = lambda i: jnp.minimum((i + 1) * nb, nblk - 1)
        table = [pl.BlockSpec((BLOCK, LANES), lambda b, i: (prev_blk(i), 0)),
                 pl.BlockSpec((tile, LANES), lambda b, i: (i, 0)),
                 pl.BlockSpec((BLOCK, LANES), lambda b, i: (next_blk(i), 0))]
        in_specs += [
            pl.BlockSpec((None, BLOCK, D_MODEL), lambda b, i: (b, prev_blk(i), 0)),
            pl.BlockSpec((None, BLOCK, D_MODEL), lambda b, i: (b, next_blk(i), 0)),
            full(w_kv.shape),
        ] + table + table
        args += [x, x, w_kv, cosf, cosf, cosf, sinf, sinf, sinf]
        scratch += [pltpu.VMEM((tile + 2 * BLOCK, KV_DUP), BF16)] * 2
    ctx_len = kctx.shape[1]
    in_specs += [pl.BlockSpec((None, ctx_len, KV_DUP), lambda b, i: (b, 0, 0))] * 2
    args += [kctx, vctx]
    return pl.pallas_call(
        functools.partial(_even_kernel, tile=tile, n_blocks_total=nblk, latent=latent),
        grid=(bsz, n),
        in_specs=in_specs,
        out_specs=pl.BlockSpec((None, tile, D_MODEL), lambda b, i: (b, i, 0)),
        out_shape=jax.ShapeDtypeStruct(x.shape, F32),
        scratch_shapes=scratch,
        compiler_params=pltpu.CompilerParams(
            dimension_semantics=("parallel", "parallel"), vmem_limit_bytes=VMEM_LIMIT),
        name="even_mix" if latent else "even_mix_ctx",
    )(*args)


def _mod_row_slices(m, g, bsz, mod_row):
    out = []
    for b in range(bsz):
        r = b if mod_row is None else mod_row
        out.append((g * (1.0 + m[r:r + 1, D_MODEL:2 * D_MODEL]), m[r:r + 1, 0:D_MODEL]))
    return out


def _half_decay_rate(lam_ref):
    neg = -lam_ref[...]
    return (0.5 * LRU_C) * (jnp.maximum(neg, 0.0) + jnp.log1p(jnp.exp(-jnp.abs(neg))))


def _gates(hz, hzb, hd, wa_ref, wi_ref, ba_ref, bi_ref, hsp, a_scr, bx_ref):
    cols = slice(hd * C_BLOCK, (hd + 1) * C_BLOCK)
    tr = jnp.tanh(_dot(hzb, wa_ref[hd]) + ba_ref[:, cols])
    ti = jnp.tanh(_dot(hzb, wi_ref[hd]) + bi_ref[:, cols])
    hs = hsp[:, cols]
    nla = tr * hs + hs
    a = jnp.exp2(nla * (-LOG2_E))
    y = jnp.tanh(nla) * (a * a + 1.0)
    root = jnp.where(y > 0.0, y * lax.rsqrt(y), 0.0)
    bx = root * (hz * ti + hz)
    for half in range(C_BLOCK // LANES):
        slab = hd * (C_BLOCK // LANES) + half
        ls = slice(half * LANES, (half + 1) * LANES)
        a_scr[slab] = a[:, ls]
        bx_ref[slab] = bx[:, ls]


def _scan_tile(a_scr, bx_scr, out_ref, h_scr, *, tile, reverse, add_ref=None):
    def body(kk, hs):
        t = (tile - 1 - kk) if reverse else kk
        r = pl.multiple_of(t * SUBLANES, SUBLANES)
        new = []
        for c in range(N_SLABS):
            hc = a_scr[c, pl.ds(r, SUBLANES), :] * hs[c] + bx_scr[c, pl.ds(r, SUBLANES), :]
            if add_ref is None:
                out_ref[c, pl.ds(r, SUBLANES), :] = hc
            else:
                out_ref[c, pl.ds(r, SUBLANES), :] = hc + add_ref[c, pl.ds(r, SUBLANES), :]
            new.append(hc)
        return tuple(new)

    hs = lax.fori_loop(0, tile, body, tuple(h_scr[c] for c in range(N_SLABS)), unroll=4)
    for c in range(N_SLABS):
        h_scr[c] = hs[c]


def _scan_rev_kernel(xh_ref, x_ref, m_ref, g_ref, wx_ref, cw_ref, cb_ref, wa_ref, wi_ref, ba_ref, bi_ref,
                     lam_ref, h0_ref, z_ref, s_ref, hfin_ref, ext_scr, a_scr, bx_scr, h_scr,
                     *, bsz, tile, n_tiles, mod_row):
    step = pl.program_id(0)
    tidx = n_tiles - 1 - step
    rows = tile * bsz
    left = CONV_LEFT * bsz
    right = (CONV_W - 1 - CONV_LEFT) * bsz

    @pl.when(step == 0)
    def _():
        h_scr[...] = h0_ref[...]
        ext_scr[:, left + rows:left + rows + right, :] = jnp.zeros((N_SLABS, right, LANES), F32)

    @pl.when(step > 0)
    def _():
        ext_scr[:, left + rows:left + rows + right, :] = ext_scr[:, left:left + right, :]

    mods = _mod_row_slices(m_ref[...], g_ref[...], bsz, mod_row)
    hs = [_norm_mod(x_ref[b], *mods[b]) for b in range(bsz)]
    for j in range(CONV_LEFT):
        r = SUBLANES - CONV_LEFT + j
        hs += [_norm_mod(xh_ref[b, r:r + 1, :], *mods[b]) for b in range(bsz)]
    xr = _dot(jnp.concatenate(hs, axis=0).astype(BF16), wx_ref[...])
    for c in range(N_SLABS):
        ls = slice(c * LANES, (c + 1) * LANES)
        for b in range(bsz):
            ext_scr[c, pl.ds(left + b, tile, stride=bsz), :] = xr[b * tile:(b + 1) * tile, ls]
        ext_scr[c, 0:left, :] = jnp.where(tidx > 0, xr[rows:rows + left, ls], 0.0)

    hsp = _half_decay_rate(lam_ref)
    for hd in range(C_HEADS):
        zs = []
        for half in range(C_BLOCK // LANES):
            c = hd * (C_BLOCK // LANES) + half
            ls = slice(c * LANES, (c + 1) * LANES)
            ext = ext_scr[c]
            z = cb_ref[:, ls]
            for j in range(CONV_W):
                z = z + cw_ref[j:j + 1, ls] * ext[j * bsz:j * bsz + rows]
            zs.append(z)
        hz = jnp.concatenate(zs, axis=1)
        hzb = hz.astype(BF16)
        z_ref[:, hd * C_BLOCK:(hd + 1) * C_BLOCK] = hzb
        _gates(hz, hzb, hd, wa_ref, wi_ref, ba_ref, bi_ref, hsp, a_scr, bx_scr)
    _scan_tile(a_scr, bx_scr, s_ref, h_scr, tile=tile, reverse=True)
    hfin_ref[...] = h_scr[...]


def _scan_fwd_kernel(z_ref, srev_ref, x_ref, m_ref, g_ref, wg_ref, wa_ref, wi_ref, ba_ref, bi_ref, lam_ref,
                     h0_ref, wout_ref, fg_ref, o_ref, hfin_ref, a_scr, bx_scr, hs_scr, h_scr, mix_scr,
                     sg_scr, *, bsz, tile, mod_row, final_norm):
    @pl.when(pl.program_id(0) == 0)
    def _():
        h_scr[...] = h0_ref[...]

    m = m_ref[...]
    mods = _mod_row_slices(m, g_ref[...], bsz, mod_row)
    he = jnp.concatenate([_norm_mod(x_ref[b], *mods[b]) for b in range(bsz)], axis=0).astype(BF16)
    sg_scr[...] = _silu(_dot(he, wg_ref[...]))

    hsp = _half_decay_rate(lam_ref)
    for hd in range(C_HEADS):
        hzb = z_ref[:, hd * C_BLOCK:(hd + 1) * C_BLOCK]
        _gates(hzb.astype(F32), hzb, hd, wa_ref, wi_ref, ba_ref, bi_ref, hsp, a_scr, bx_scr)
    _scan_tile(a_scr, bx_scr, hs_scr, h_scr, tile=tile, reverse=False, add_ref=srev_ref)
    hfin_ref[...] = h_scr[...]

    for b in range(bsz):
        rs = slice(b * tile, (b + 1) * tile)
        for c in range(N_SLABS):
            ls = slice(c * LANES, (c + 1) * LANES)
            hsum = hs_scr[c, pl.ds(b, tile, stride=bsz), :]
            mix_scr[rs, ls] = (hsum * sg_scr[rs, ls]).astype(BF16)
    y = _dot(mix_scr[...], wout_ref[...])
    for b in range(bsz):
        r = b if mod_row is None else mod_row
        gate = m[r:r + 1, 2 * D_MODEL:3 * D_MODEL]
        xn = x_ref[b] + gate * y[b * tile:(b + 1) * tile]
        if final_norm:
            ms = jnp.mean(xn * xn, axis=-1, keepdims=True)
            xn = xn * lax.rsqrt(ms + EPS) * fg_ref[...]
        o_ref[b] = xn


def _const_spec(shape):
    return pl.BlockSpec(shape, lambda i: (0,) * len(shape), pipeline_mode=pl.Buffered(1))


def _gate_specs():
    return [
        _const_spec((C_HEADS, C_BLOCK, C_BLOCK)),
        _const_spec((C_HEADS, C_BLOCK, C_BLOCK)),
        _const_spec((1, C_WIDTH)),
        _const_spec((1, C_WIDTH)),
        _const_spec((1, C_WIDTH)),
        _const_spec((N_SLABS, SUBLANES, LANES)),
    ]


def _scan_reverse(x, mod_l, mod_row, g, w_x, conv_w, conv_b, w_a, w_i, b_a, b_i, lam, h0):
    bsz, L, _ = x.shape
    tile = T_SCAN
    n = L // tile
    rev = lambda i: (0, n - 1 - i, 0)
    left = CONV_LEFT * bsz
    right = (CONV_W - 1 - CONV_LEFT) * bsz
    rows = tile * bsz
    return pl.pallas_call(
        functools.partial(_scan_rev_kernel, bsz=bsz, tile=tile, n_tiles=n, mod_row=mod_row),
        grid=(n,),
        in_specs=[
            pl.BlockSpec((bsz, SUBLANES, D_MODEL),
                         lambda i: (0, jnp.maximum((n - 1 - i) * (tile // SUBLANES) - 1, 0), 0)),
            pl.BlockSpec((bsz, tile, D_MODEL), rev),
            _const_spec((MOD_ROWS, 3 * D_MODEL)),
            _const_spec((1, D_MODEL)),
            _const_spec((D_MODEL, C_WIDTH)),
            _const_spec((CONV_W, C_WIDTH)),
            _const_spec((1, C_WIDTH)),
        ] + _gate_specs(),
        out_specs=[
            pl.BlockSpec((rows, C_WIDTH), lambda i: (n - 1 - i, 0)),
            pl.BlockSpec((N_SLABS, rows, LANES), rev),
            pl.BlockSpec((N_SLABS, SUBLANES, LANES), lambda i: (0, 0, 0)),
        ],
        out_shape=[
            jax.ShapeDtypeStruct((L * bsz, C_WIDTH), BF16),
            jax.ShapeDtypeStruct((N_SLABS, L * bsz, LANES), F32),
            jax.ShapeDtypeStruct((N_SLABS, SUBLANES, LANES), F32),
        ],
        scratch_shapes=[
            pltpu.VMEM((N_SLABS, left + rows + right, LANES), F32),
            pltpu.VMEM((N_SLABS, rows, LANES), F32),
            pltpu.VMEM((N_SLABS, rows, LANES), F32),
            pltpu.VMEM((N_SLABS, SUBLANES, LANES), F32),
        ],
        compiler_params=pltpu.CompilerParams(
            dimension_semantics=("arbitrary",), vmem_limit_bytes=VMEM_LIMIT),
        name="odd_scan_rev",
    )(x, x, mod_l, g, w_x, conv_w, conv_b, w_a, w_i, b_a, b_i, lam, h0)


def _scan_forward(z, s_rev, x, mod_l, mod_row, g, w_g, w_a, w_i, b_a, b_i, lam, h0, w_out, final_g):
    bsz, L, _ = x.shape
    tile = T_SCAN
    n = L // tile
    fwd = lambda i: (0, i, 0)
    final_norm = final_g is not None
    fg = final_g if final_norm else jnp.ones((1, D_MODEL), F32)
    return pl.pallas_call(
        functools.partial(_scan_fwd_kernel, bsz=bsz, tile=tile, mod_row=mod_row, final_norm=final_norm),
        grid=(n,),
        in_specs=[
            pl.BlockSpec((tile * bsz, C_WIDTH), lambda i: (i, 0)),
            pl.BlockSpec((N_SLABS, tile * bsz, LANES), fwd),
            pl.BlockSpec((bsz, tile, D_MODEL), fwd),
            _const_spec((MOD_ROWS, 3 * D_MODEL)),
            _const_spec((1, D_MODEL)),
            _const_spec((D_MODEL, C_WIDTH)),
        ] + _gate_specs() + [_const_spec((C_WIDTH, D_MODEL)), _const_spec((1, D_MODEL))],
        out_specs=[
            pl.BlockSpec((bsz, tile, D_MODEL), fwd),
            pl.BlockSpec((N_SLABS, SUBLANES, LANES), lambda i: (0, 0, 0)),
        ],
        out_shape=[
            jax.ShapeDtypeStruct(x.shape, F32),
            jax.ShapeDtypeStruct((N_SLABS, SUBLANES, LANES), F32),
        ],
        scratch_shapes=[
            pltpu.VMEM((N_SLABS, tile * bsz, LANES), F32),
            pltpu.VMEM((N_SLABS, tile * bsz, LANES), F32),
            pltpu.VMEM((N_SLABS, tile * bsz, LANES), F32),
            pltpu.VMEM((N_SLABS, SUBLANES, LANES), F32),
            pltpu.VMEM((bsz * tile, C_WIDTH), BF16),
            pltpu.VMEM((bsz * tile, C_WIDTH), F32),
        ],
        compiler_params=pltpu.CompilerParams(
            dimension_semantics=("arbitrary",), vmem_limit_bytes=VMEM_LIMIT),
        name="odd_scan_fwd",
    )(z, s_rev, x, mod_l, g, w_g, w_a, w_i, b_a, b_i, lam, h0, w_out, fg)


def _rope_tables(L):
    rows = L // GRID_W
    r, col = jnp.meshgrid(jnp.arange(rows), jnp.arange(GRID_W), indexing="ij")
    r = r.reshape(-1).astype(F32)
    col = col.reshape(-1).astype(F32)
    n_freq = HEAD_DIM // 4
    inv_freq = ROPE_THETA ** (-jnp.arange(n_freq, dtype=F32) / n_freq)
    ang = jnp.concatenate([r[:, None] * inv_freq, col[:, None] * inv_freq], axis=-1)
    cos, sin = jnp.cos(ang), jnp.sin(ang)
    cosf = jnp.tile(jnp.concatenate([cos, cos], axis=-1), (1, LANES // HEAD_DIM))
    sinf = jnp.tile(jnp.concatenate([-sin, sin], axis=-1), (1, LANES // HEAD_DIM))
    return cosf, sinf


def _dup_heads(w):
    return jnp.concatenate([w[:, :HEAD_DIM], w[:, :HEAD_DIM], w[:, HEAD_DIM:], w[:, HEAD_DIM:]], axis=1)


def kernel(x, c, ctx, c_ctx, norm_g, w_mod, b_mod, ab_w_in, a_ln_g, a_ln_b, a_w_s, a_b_s, b_sink,
           ab_w_out, c_w_in, c_conv_w, c_conv_b, c_w_a, c_b_a, c_w_i, c_b_i, c_lam, c_w_out, final_g):
    bsz, L, d = x.shape
    ctx_len = ctx.shape[1]
    assert d == D_MODEL and bsz == SUBLANES and L % T_SCAN == 0 and L % T_EVEN == 0
    assert ctx_len % BLOCK == 0 and ctx_len % T_SCAN == 0 and DEPTH % 2 == 0
    ctx_row = bsz

    cin = jnp.concatenate([c, c_ctx[None, :], jnp.zeros((MOD_ROWS - bsz - 1, d), F32)], axis=0)
    mod = _modulation(cin, w_mod, b_mod).reshape(DEPTH, MOD_ROWS, 1, 3 * d)
    cosf, sinf = _rope_tables(L)
    zero_state = jnp.zeros((N_SLABS, SUBLANES, LANES), F32)

    xc = ctx
    for layer in range(DEPTH):
        need_ctx = layer < DEPTH - 1
        i = layer // 2
        mod_l = mod[layer]
        g = norm_g[layer][None, :]
        if layer % 2 == 0:
            w_in = ab_w_in[i].astype(BF16)
            w_kv = jnp.concatenate([_dup_heads(w_in[:, AB_K_OFF:AB_K_OFF + KV_WIDTH]),
                                    _dup_heads(w_in[:, AB_K_OFF + KV_WIDTH:AB_GB_OFF])], axis=1)
            w_out = ab_w_out[i].astype(BF16)
            lng = a_ln_g[i][None, :]
            lnb = a_ln_b[i][None, :]
            ws = a_w_s[i].astype(BF16)
            bs = jnp.broadcast_to(a_b_s[i][:, :, None], (A_GROUPS, CHUNK, LANES))
            sink = b_sink[i]
            kc, vc = _ctx_kv_project(xc, mod_l, ctx_row, g, w_kv)
            x = _even_mix(x, mod_l, None, g, w_in, w_out, lng, lnb, ws, bs, sink, kc, vc,
                          cosf=cosf, sinf=sinf, w_kv=w_kv, tile=T_EVEN)
            if need_ctx:
                xc = _even_mix(xc, mod_l, ctx_row, g, w_in, w_out, lng, lnb, ws, bs, sink, kc, vc,
                               tile=ctx_len)
        else:
            w_in = c_w_in[i].astype(BF16)
            w_x, w_g = w_in[:, :C_WIDTH], w_in[:, C_WIDTH:]
            w_out = c_w_out[i].astype(BF16)
            conv_w, conv_b = 0.5 * c_conv_w[i], 0.5 * c_conv_b[i][None, :]
            mod_flat = mod_l.reshape(MOD_ROWS, 3 * d)
            gate_args = [(c_w_a[i, dd].astype(BF16), c_w_i[i, dd].astype(BF16),
                          0.5 * c_b_a[i, dd][None, :], 0.5 * c_b_i[i, dd][None, :], c_lam[i, dd][None, :])
                         for dd in range(2)]
            z_c, s_rev_c, h0_rev = _scan_reverse(xc, mod_flat, ctx_row, g, w_x, conv_w, conv_b,
                                                 *gate_args[1], zero_state)
            xc_new, h0_fwd = _scan_forward(z_c, s_rev_c, xc, mod_flat, ctx_row, g, w_g, *gate_args[0],
                                           zero_state, w_out, None)
            z, s_rev, _ = _scan_reverse(x, mod_flat, None, g, w_x, conv_w, conv_b, *gate_args[1], h0_rev)
            fin = final_g[None, :] if layer == DEPTH - 1 else None
            x, _ = _scan_forward(z, s_rev, x, mod_flat, None, g, w_g, *gate_args[0], h0_fwd, w_out, fin)
            if need_ctx:
                xc = xc_new
    return x
```

```python
import functools

import jax
import jax.numpy as jnp
from jax import lax
from jax.experimental import pallas as pl
from jax.experimental.pallas import tpu as pltpu

F32 = jnp.float32
BF16 = jnp.bfloat16

D_MODEL = 1024
DEPTH = 4
GRID_W = 64
EPS = 1e-6
CHUNK = 128
A_GROUPS = 4
A_WIDTH = 512
HEAD_DIM = 64
B_Q_HEADS = 8
B_KV_HEADS = 2
B_GQA = B_Q_HEADS // B_KV_HEADS
B_WIDTH = 512
KV_WIDTH = 128
BLOCK = 128
ROPE_THETA = 10000.0
AB_Q_OFF = 3 * A_WIDTH
AB_K_OFF = 3 * A_WIDTH + B_WIDTH
AB_GB_OFF = AB_K_OFF + 2 * KV_WIDTH
C_WIDTH = 1024
C_HEADS = 4
C_BLOCK = 256
CONV_W = 4
CONV_LEFT = 2
LRU_C = 8.0
LOG2_E = 1.4426950408889634
MASKED = -1e30

LANES = 128
SUBLANES = 8
N_SLABS = C_WIDTH // LANES
MOD_ROWS = 16
VMEM_LIMIT = 56 * 1024 * 1024
KV_DUP = 2 * KV_WIDTH

T_EVEN = 512
T_SCAN_REV = 64
T_SCAN_FWD = 64


def _sigmoid(x):
    return 0.5 * jnp.tanh(0.5 * x) + 0.5


def _silu(x):
    return x * _sigmoid(x)


def _gelu(x):
    return 0.5 * x * (1.0 + jnp.tanh(0.7978845608028654 * (x + 0.044715 * (x * x * x))))


def _norm_mod(x, g1s, shift):
    ms = jnp.mean(x * x, axis=-1, keepdims=True)
    return x * lax.rsqrt(ms + EPS) * g1s + shift


def _mod_parts(m):
    return m[:, 0:D_MODEL], m[:, D_MODEL:2 * D_MODEL], m[:, 2 * D_MODEL:3 * D_MODEL]


def _rope(x, cosf, sinf):
    n = x.shape[-1]
    lane = lax.broadcasted_iota(jnp.int32, x.shape, 1)
    first = (lane % HEAD_DIM) < (HEAD_DIM // 2)
    sw = jnp.where(first, pltpu.roll(x, n - HEAD_DIM // 2, 1), pltpu.roll(x, HEAD_DIM // 2, 1))
    return x * cosf + sw * sinf


def _dot(a, b):
    return jnp.dot(a, b, preferred_element_type=F32)


def _dot_t(a, b):
    return lax.dot_general(a, b, (((1,), (1,)), ((), ())), preferred_element_type=F32)


def _mod_kernel(c_ref, w_ref, b_ref, o_ref):
    s = _silu(c_ref[...])
    o_ref[...] = _dot(s.astype(BF16), w_ref[...].astype(BF16)) + b_ref[...]


def _modulation(cin, w_mod, b_mod):
    nj = 3
    return pl.pallas_call(
        _mod_kernel,
        grid=(DEPTH, nj),
        in_specs=[
            pl.BlockSpec((MOD_ROWS, D_MODEL), lambda l, j: (0, 0)),
            pl.BlockSpec((None, D_MODEL, D_MODEL), lambda l, j: (l, 0, j)),
            pl.BlockSpec((None, 1, D_MODEL), lambda l, j: (l, 0, j)),
        ],
        out_specs=pl.BlockSpec((None, MOD_ROWS, D_MODEL), lambda l, j: (l, 0, j)),
        out_shape=jax.ShapeDtypeStruct((DEPTH, MOD_ROWS, 3 * D_MODEL), F32),
        compiler_params=pltpu.CompilerParams(
            dimension_semantics=("arbitrary", "arbitrary"), vmem_limit_bytes=VMEM_LIMIT),
        name="modulation",
    )(cin, w_mod, b_mod.reshape(DEPTH, 1, 3 * D_MODEL))


def _ctx_kv_kernel(x_ref, m_ref, g_ref, w_ref, k_ref, v_ref):
    shift, scale, _ = _mod_parts(m_ref[...])
    h = _norm_mod(x_ref[...], g_ref[...] * (1.0 + scale), shift).astype(BF16)
    kv = _dot(h, w_ref[...])
    k_ref[...] = kv[:, 0:KV_DUP].astype(BF16)
    v_ref[...] = kv[:, KV_DUP:2 * KV_DUP].astype(BF16)


def _ctx_kv_project(xc, mod_l, mod_row, g, w_kv):
    bsz, lc, _ = xc.shape
    return pl.pallas_call(
        _ctx_kv_kernel,
        grid=(bsz,),
        in_specs=[
            pl.BlockSpec((None, lc, D_MODEL), lambda b: (b, 0, 0)),
            pl.BlockSpec((None, 1, 3 * D_MODEL), lambda b: (mod_row, 0, 0)),
            pl.BlockSpec((1, D_MODEL), lambda b: (0, 0)),
            pl.BlockSpec((D_MODEL, 2 * KV_DUP), lambda b: (0, 0)),
        ],
        out_specs=[
            pl.BlockSpec((None, lc, KV_DUP), lambda b: (b, 0, 0)),
            pl.BlockSpec((None, lc, KV_DUP), lambda b: (b, 0, 0)),
        ],
        out_shape=[jax.ShapeDtypeStruct((bsz, lc, KV_DUP), BF16)] * 2,
        compiler_params=pltpu.CompilerParams(
            dimension_semantics=("parallel",), vmem_limit_bytes=VMEM_LIMIT),
        name="even_ctx_kv",
    )(xc, mod_l, g, w_kv)


def _even_kernel(*refs, tile, n_blocks_total, latent):
    if latent:
        (x_ref, m_ref, g_ref, win_ref, wout_ref, lng_ref, lnb_ref, ws_ref, bs_ref, sink_ref,
         xp_ref, xn_ref, wkv_ref, cosp_ref, cos_ref, cosn_ref, sinp_ref, sin_ref, sinn_ref,
         kctx_ref, vctx_ref, o_ref, ga_scr, gb_scr, vln_scr, q_scr, mix_scr, kbuf, vbuf) = refs
    else:
        (x_ref, m_ref, g_ref, win_ref, wout_ref, lng_ref, lnb_ref, ws_ref, bs_ref, sink_ref,
         kctx_ref, vctx_ref, o_ref, ga_scr, gb_scr, vln_scr, q_scr, mix_scr) = refs
    nb = tile // BLOCK
    i = pl.program_id(1)
    rows4 = B_GQA * BLOCK

    x = x_ref[...]
    shift, scale, gate = _mod_parts(m_ref[...])
    g1s = g_ref[...] * (1.0 + scale)
    if latent:
        he = _norm_mod(jnp.concatenate([xp_ref[...], x, xn_ref[...]], axis=0), g1s, shift).astype(BF16)
        h = he[BLOCK:BLOCK + tile]
        kv = _dot(he, wkv_ref[...])
        cos_e = jnp.concatenate([cosp_ref[...], cos_ref[...], cosn_ref[...]], axis=0)
        sin_e = jnp.concatenate([sinp_ref[...], sin_ref[...], sinn_ref[...]], axis=0)
        for s in range(KV_DUP // LANES):
            sl = slice(s * LANES, (s + 1) * LANES)
            kbuf[:, sl] = _rope(kv[:, sl], cos_e, sin_e).astype(BF16)
        vbuf[...] = kv[:, KV_DUP:2 * KV_DUP].astype(BF16)
    else:
        h = _norm_mod(x, g1s, shift).astype(BF16)

    u = _dot(h, win_ref[:, 0:A_WIDTH])
    gate_a = _dot(h, win_ref[:, 2 * A_WIDTH:3 * A_WIDTH])
    ga_scr[...] = _gelu(u) * _silu(gate_a)
    v = _gelu(_dot(h, win_ref[:, A_WIDTH:2 * A_WIDTH]))
    for g in range(A_GROUPS):
        sl = slice(g * LANES, (g + 1) * LANES)
        vg = v[:, sl]
        mu = jnp.mean(vg, axis=-1, keepdims=True)
        vc = vg - mu
        var = jnp.mean(vc * vc, axis=-1, keepdims=True)
        vln_scr[:, sl] = (vc * lax.rsqrt(var + EPS) * lng_ref[:, sl] + lnb_ref[:, sl]).astype(BF16)
    q = _dot(h, win_ref[:, AB_Q_OFF:AB_Q_OFF + B_WIDTH])
    low_half = lax.broadcasted_iota(jnp.int32, (tile, LANES), 1) < HEAD_DIM
    for pair in range(B_Q_HEADS // 2):
        qs = q[:, pair * LANES:(pair + 1) * LANES]
        if latent:
            qs = _rope(qs, cos_ref[...], sin_ref[...])
        qs = qs * (HEAD_DIM ** -0.5)
        q_lo = jnp.where(low_half, qs, 0.0).astype(BF16)
        q_hi = jnp.where(low_half, 0.0, qs).astype(BF16)
        for jb in range(nb):
            rows = slice(jb * BLOCK, (jb + 1) * BLOCK)
            base = (jb * B_Q_HEADS + 2 * pair) * BLOCK
            q_scr[base:base + BLOCK, :] = q_lo[rows]
            q_scr[base + BLOCK:base + 2 * BLOCK, :] = q_hi[rows]
    gb_scr[...] = _silu(_dot(h, win_ref[:, AB_GB_OFF:AB_GB_OFF + B_WIDTH]))

    if latent:
        qi = lax.broadcasted_iota(jnp.int32, (rows4, 3 * BLOCK), 0) & (BLOCK - 1)
        kj = lax.broadcasted_iota(jnp.int32, (rows4, 3 * BLOCK), 1)
        band = (kj >= qi) & (kj <= qi + 2 * BLOCK)
    out_low = lax.broadcasted_iota(jnp.int32, (BLOCK, LANES), 1) < HEAD_DIM

    for jb in range(nb):
        rows = slice(jb * BLOCK, (jb + 1) * BLOCK)
        for g in range(A_GROUPS):
            sl = slice(g * LANES, (g + 1) * LANES)
            sv = _dot(ws_ref[g], vln_scr[rows, sl]) + bs_ref[g]
            mix_scr[rows, sl] = (ga_scr[rows, sl] * sv).astype(BF16)
        if latent:
            mask = band
            if jb == 0:
                mask = mask & (kj >= jnp.where(i > 0, 0, BLOCK))
            if jb == nb - 1:
                mask = mask & (kj < jnp.where(i < n_blocks_total // nb - 1, 3 * BLOCK, 2 * BLOCK))
        for kvh in range(B_KV_HEADS):
            ks = slice(kvh * LANES, (kvh + 1) * LANES)
            base = (jb * B_Q_HEADS + kvh * B_GQA) * BLOCK
            q4 = q_scr[base:base + rows4, :]
            sink = jnp.concatenate(
                [jnp.full((BLOCK, 1), sink_ref[kvh * B_GQA + j], F32) for j in range(B_GQA)], axis=0)
            s_c = _dot_t(q4, kctx_ref[:, ks])
            m = jnp.maximum(sink, jnp.max(s_c, axis=-1, keepdims=True))
            if latent:
                s_l = jnp.where(mask, _dot_t(q4, kbuf[jb * BLOCK:(jb + 3) * BLOCK, ks]), MASKED)
                m = jnp.maximum(m, jnp.max(s_l, axis=-1, keepdims=True))
            p_c = jnp.exp(s_c - m)
            den = jnp.exp(sink - m) + jnp.sum(p_c, axis=-1, keepdims=True)
            o = _dot(p_c.astype(BF16), vctx_ref[:, ks])
            if latent:
                p_l = jnp.exp(s_l - m)
                den = den + jnp.sum(p_l, axis=-1, keepdims=True)
                o = o + _dot(p_l.astype(BF16), vbuf[jb * BLOCK:(jb + 3) * BLOCK, ks])
            o = o * (1.0 / den)
            for hp in range(B_GQA // 2):
                pair_out = jnp.where(out_low, o[2 * hp * BLOCK:(2 * hp + 1) * BLOCK],
                                     o[(2 * hp + 1) * BLOCK:(2 * hp + 2) * BLOCK])
                col = (kvh * (B_GQA // 2) + hp) * LANES
                yb = pair_out * gb_scr[rows, col:col + LANES]
                mix_scr[rows, A_WIDTH + col:A_WIDTH + col + LANES] = yb.astype(BF16)

    y = _dot(mix_scr[...], wout_ref[...])
    o_ref[...] = x + gate * y


def _even_mix(x, mod_l, mod_row, g, w_in, w_out, lng, lnb, ws, bs, sink, kctx, vctx,
              cosf=None, sinf=None, w_kv=None, *, tile):
    latent = w_kv is not None
    bsz, L, _ = x.shape
    n = L // tile
    nb = tile // BLOCK
    nblk = L // BLOCK
    row = (lambda b: b) if mod_row is None else (lambda b: mod_row)
    full = lambda shape: pl.BlockSpec(shape, lambda b, i: (0,) * len(shape), pipeline_mode=pl.Buffered(1))
    in_specs = [
        pl.BlockSpec((None, tile, D_MODEL), lambda b, i: (b, i, 0)),
        pl.BlockSpec((None, 1, 3 * D_MODEL), lambda b, i: (row(b), 0, 0)),
        full((1, D_MODEL)),
        full(w_in.shape),
        full(w_out.shape),
        full((1, A_WIDTH)),
        full((1, A_WIDTH)),
        full(ws.shape),
        full(bs.shape),
        pl.BlockSpec(memory_space=pltpu.SMEM),
    ]
    args = [x, mod_l, g, w_in, w_out, lng, lnb, ws, bs, sink]
    scratch = [
        pltpu.VMEM((tile, A_WIDTH), F32),
        pltpu.VMEM((tile, B_WIDTH), F32),
        pltpu.VMEM((tile, A_WIDTH), BF16),
        pltpu.VMEM((nb * B_Q_HEADS * BLOCK, LANES), BF16),
        pltpu.VMEM((tile, A_WIDTH + B_WIDTH), BF16),
    ]
    if latent:
        prev_blk = lambda i: jnp.maximum(i * nb - 1, 0)
        next_blk = lambda i: jnp.minimum((i + 1) * nb, nblk - 1)
        table = [pl.BlockSpec((BLOCK, LANES), lambda b, i: (prev_blk(i), 0)),
                 pl.BlockSpec((tile, LANES), lambda b, i: (i, 0)),
                 pl.BlockSpec((BLOCK, LANES), lambda b, i: (next_blk(i), 0))]
        in_specs += [
            pl.BlockSpec((None, BLOCK, D_MODEL), lambda b, i: (b, prev_blk(i), 0)),
            pl.BlockSpec((None, BLOCK, D_MODEL), lambda b, i: (b, next_blk(i), 0)),
            full(w_kv.shape),
        ] + table + table
        args += [x, x, w_kv, cosf, cosf, cosf, sinf, sinf, sinf]
        scratch += [pltpu.VMEM((tile + 2 * BLOCK, KV_DUP), BF16)] * 2
    ctx_len = kctx.shape[1]
    in_specs += [pl.BlockSpec((None, ctx_len, KV_DUP), lambda b, i: (b, 0, 0))] * 2
    args += [kctx, vctx]
    return pl.pallas_call(
        functools.partial(_even_kernel, tile=tile, n_blocks_total=nblk, latent=latent),
        grid=(bsz, n),
        in_specs=in_specs,
        out_specs=pl.BlockSpec((None, tile, D_MODEL), lambda b, i: (b, i, 0)),
        out_shape=jax.ShapeDtypeStruct(x.shape, F32),
        scratch_shapes=scratch,
        compiler_params=pltpu.CompilerParams(
            dimension_semantics=("parallel", "parallel"), vmem_limit_bytes=VMEM_LIMIT),
        name="even_mix" if latent else "even_mix_ctx",
    )(*args)


def _mod_row_slices(m, g, bsz, mod_row):
    out = []
    for b in range(bsz):
        r = b if mod_row is None else mod_row
        out.append((g * (1.0 + m[r:r + 1, D_MODEL:2 * D_MODEL]), m[r:r + 1, 0:D_MODEL]))
    return out


def _half_decay_rate(lam_ref):
    neg = -lam_ref[...]
    return (0.5 * LRU_C) * (jnp.maximum(neg, 0.0) + jnp.log1p(jnp.exp(-jnp.abs(neg))))


def _gates(hz, hzb, hd, wa_ref, wi_ref, ba_ref, bi_ref, hsp, a_scr, bx_ref):
    cols = slice(hd * C_BLOCK, (hd + 1) * C_BLOCK)
    tr = jnp.tanh(_dot(hzb, wa_ref[hd]) + ba_ref[:, cols])
    ti = jnp.tanh(_dot(hzb, wi_ref[hd]) + bi_ref[:, cols])
    hs = hsp[:, cols]
    nla = tr * hs + hs
    a = jnp.exp2(nla * (-LOG2_E))
    y = jnp.tanh(nla) * (a * a + 1.0)
    root = jnp.where(y > 0.0, y * lax.rsqrt(y), 0.0)
    bx = root * (hz * ti + hz)
    for half in range(C_BLOCK // LANES):
        slab = hd * (C_BLOCK // LANES) + half
        ls = slice(half * LANES, (half + 1) * LANES)
        a_scr[slab] = a[:, ls]
        bx_ref[slab] = bx[:, ls]


def _scan_tile(a_scr, bx_scr, out_ref, h_scr, *, tile, reverse, add_ref=None):
    def body(kk, hs):
        t = (tile - 1 - kk) if reverse else kk
        r = pl.multiple_of(t * SUBLANES, SUBLANES)
        new = []
        for c in range(N_SLABS):
            hc = a_scr[c, pl.ds(r, SUBLANES), :] * hs[c] + bx_scr[c, pl.ds(r, SUBLANES), :]
            if add_ref is None:
                out_ref[c, pl.ds(r, SUBLANES), :] = hc
            else:
                out_ref[c, pl.ds(r, SUBLANES), :] = hc + add_ref[c, pl.ds(r, SUBLANES), :]
            new.append(hc)
        return tuple(new)

    hs = lax.fori_loop(0, tile, body, tuple(h_scr[c] for c in range(N_SLABS)), unroll=4)
    for c in range(N_SLABS):
        h_scr[c] = hs[c]


def _rev_project(xh_ref, x_ref, m_ref, g_ref, wx_ref, ext_ref, has_prev, *, bsz, tile, mod_row,
                 between=None):
    rows = tile * bsz
    left = CONV_LEFT * bsz
    mods = _mod_row_slices(m_ref[...], g_ref[...], bsz, mod_row)
    hs = [_norm_mod(x_ref[b], *mods[b]) for b in range(bsz)]
    for j in range(CONV_LEFT):
        r = SUBLANES - CONV_LEFT + j
        hs += [_norm_mod(xh_ref[b, r:r + 1, :], *mods[b]) for b in range(bsz)]
    he = jnp.concatenate(hs, axis=0).astype(BF16)
    for hd in range(C_HEADS):
        xr = _dot(he, wx_ref[:, hd * C_BLOCK:(hd + 1) * C_BLOCK])
        for half in range(C_BLOCK // LANES):
            c = hd * (C_BLOCK // LANES) + half
            ls = slice(half * LANES, (half + 1) * LANES)
            for b in range(bsz):
                ext_ref[c, pl.ds(left + b, tile, stride=bsz), :] = xr[b * tile:(b + 1) * tile, ls]
            ext_ref[c, 0:left, :] = jnp.where(has_prev, xr[rows:rows + left, ls], 0.0)
        if between is not None:
            between(hd)


def _rev_head(hd, ext_ref, nxt_scr, cw_ref, cb_ref, wa_ref, wi_ref, ba_ref, bi_ref, hsp,
              z_ref, a_scr, bx_scr, *, bsz, tile):
    rows = tile * bsz
    zs = []
    for half in range(C_BLOCK // LANES):
        c = hd * (C_BLOCK // LANES) + half
        ls = slice(c * LANES, (c + 1) * LANES)
        ext = jnp.concatenate([ext_ref[c], nxt_scr[c]], axis=0)
        z = cb_ref[:, ls]
        for j in range(CONV_W):
            z = z + cw_ref[j:j + 1, ls] * ext[j * bsz:j * bsz + rows]
        zs.append(z)
    hz = jnp.concatenate(zs, axis=1)
    hzb = hz.astype(BF16)
    z_ref[:, hd * C_BLOCK:(hd + 1) * C_BLOCK] = hzb
    _gates(hz, hzb, hd, wa_ref, wi_ref, ba_ref, bi_ref, hsp, a_scr, bx_scr)


def _scan_rev_kernel(xh0_ref, x0_ref, xh_ref, x_ref, m_ref, g_ref, wx_ref, cw_ref, cb_ref, wa_ref, wi_ref,
                     ba_ref, bi_ref, lam_ref, h0_ref, z_ref, s_ref, hfin_ref, ext0, ext1, nxt_scr,
                     a_scr, bx_scr, h_scr, *, bsz, tile, n_tiles, mod_row):
    step = pl.program_id(0)
    left = CONV_LEFT * bsz
    right = (CONV_W - 1 - CONV_LEFT) * bsz
    project = functools.partial(_rev_project, m_ref=m_ref, g_ref=g_ref, wx_ref=wx_ref,
                                bsz=bsz, tile=tile, mod_row=mod_row)

    @pl.when(step == 0)
    def _():
        h_scr[...] = h0_ref[...]
        nxt_scr[...] = jnp.zeros(nxt_scr.shape, F32)
        project(xh0_ref, x0_ref, ext_ref=ext0, has_prev=n_tiles > 1)

    def both(cur, nxt_buf):
        hsp = _half_decay_rate(lam_ref)
        head = functools.partial(_rev_head, ext_ref=cur, nxt_scr=nxt_scr, cw_ref=cw_ref, cb_ref=cb_ref,
                                 wa_ref=wa_ref, wi_ref=wi_ref, ba_ref=ba_ref, bi_ref=bi_ref, hsp=hsp,
                                 z_ref=z_ref, a_scr=a_scr, bx_scr=bx_scr, bsz=bsz, tile=tile)
        project(xh_ref, x_ref, ext_ref=nxt_buf, has_prev=n_tiles - 2 - step > 0, between=head)
        for c in range(N_SLABS):
            nxt_scr[c] = cur[c, left:left + right, :]
        _scan_tile(a_scr, bx_scr, s_ref, h_scr, tile=tile, reverse=True)

    @pl.when(step % 2 == 0)
    def _():
        both(ext0, ext1)

    @pl.when(step % 2 == 1)
    def _():
        both(ext1, ext0)

    hfin_ref[...] = h_scr[...]


def _fwd_gate_branch(x_ref, m_ref, g_ref, wg_ref, sg_ref, *, bsz, mod_row, between=None):
    mods = _mod_row_slices(m_ref[...], g_ref[...], bsz, mod_row)
    he = jnp.concatenate([_norm_mod(x_ref[b], *mods[b]) for b in range(bsz)], axis=0).astype(BF16)
    for hd in range(C_HEADS):
        cols = slice(hd * C_BLOCK, (hd + 1) * C_BLOCK)
        sg_ref[:, cols] = _silu(_dot(he, wg_ref[:, cols]))
        if between is not None:
            between(hd)


def _fwd_head(hd, z_ref, wa_ref, wi_ref, ba_ref, bi_ref, hsp, a_scr, bx_scr):
    hzb = z_ref[:, hd * C_BLOCK:(hd + 1) * C_BLOCK]
    _gates(hzb.astype(F32), hzb, hd, wa_ref, wi_ref, ba_ref, bi_ref, hsp, a_scr, bx_scr)


def _fwd_finish(sg_ref, srev_ref, x_ref, m_ref, wout_ref, fg_ref, o_ref, a_scr, bx_scr, hs_scr, h_scr,
                mix_scr, *, bsz, tile, mod_row, final_norm):
    _scan_tile(a_scr, bx_scr, hs_scr, h_scr, tile=tile, reverse=False, add_ref=srev_ref)

    for b in range(bsz):
        rs = slice(b * tile, (b + 1) * tile)
        for c in range(N_SLABS):
            ls = slice(c * LANES, (c + 1) * LANES)
            hsum = hs_scr[c, pl.ds(b, tile, stride=bsz), :]
            mix_scr[rs, ls] = (hsum * sg_ref[rs, ls]).astype(BF16)
    y = _dot(mix_scr[...], wout_ref[...])
    m = m_ref[...]
    for b in range(bsz):
        r = b if mod_row is None else mod_row
        gate = m[r:r + 1, 2 * D_MODEL:3 * D_MODEL]
        xn = x_ref[b] + gate * y[b * tile:(b + 1) * tile]
        if final_norm:
            ms = jnp.mean(xn * xn, axis=-1, keepdims=True)
            xn = xn * lax.rsqrt(ms + EPS) * fg_ref[...]
        o_ref[b] = xn


def _scan_fwd_kernel(z_ref, srev_ref, x_ref, xn_ref, m_ref, g_ref, wg_ref, wa_ref, wi_ref, ba_ref, bi_ref,
                     lam_ref, h0_ref, wout_ref, fg_ref, o_ref, hfin_ref, sg0, sg1, a_scr, bx_scr, hs_scr,
                     h_scr, mix_scr, *, bsz, tile, mod_row, final_norm):
    step = pl.program_id(0)
    gate_branch = functools.partial(_fwd_gate_branch, m_ref=m_ref, g_ref=g_ref, wg_ref=wg_ref,
                                    bsz=bsz, mod_row=mod_row)

    @pl.when(step == 0)
    def _():
        h_scr[...] = h0_ref[...]
        gate_branch(x_ref, sg_ref=sg0)

    def both(cur, nxt_buf):
        hsp = _half_decay_rate(lam_ref)
        head = functools.partial(_fwd_head, z_ref=z_ref, wa_ref=wa_ref, wi_ref=wi_ref, ba_ref=ba_ref,
                                 bi_ref=bi_ref, hsp=hsp, a_scr=a_scr, bx_scr=bx_scr)
        gate_branch(xn_ref, sg_ref=nxt_buf, between=head)
        _fwd_finish(cur, srev_ref, x_ref, m_ref, wout_ref, fg_ref, o_ref, a_scr, bx_scr, hs_scr, h_scr,
                    mix_scr, bsz=bsz, tile=tile, mod_row=mod_row, final_norm=final_norm)

    @pl.when(step % 2 == 0)
    def _():
        both(sg0, sg1)

    @pl.when(step % 2 == 1)
    def _():
        both(sg1, sg0)

    hfin_ref[...] = h_scr[...]


def _const_spec(shape):
    return pl.BlockSpec(shape, lambda i: (0,) * len(shape), pipeline_mode=pl.Buffered(1))


def _gate_specs():
    return [
        _const_spec((C_HEADS, C_BLOCK, C_BLOCK)),
        _const_spec((C_HEADS, C_BLOCK, C_BLOCK)),
        _const_spec((1, C_WIDTH)),
        _const_spec((1, C_WIDTH)),
        _const_spec((1, C_WIDTH)),
        _const_spec((N_SLABS, SUBLANES, LANES)),
    ]


def _scan_reverse(x, mod_l, mod_row, g, w_x, conv_w, conv_b, w_a, w_i, b_a, b_i, lam, h0):
    bsz, L, _ = x.shape
    tile = T_SCAN_REV
    n = L // tile
    rev = lambda i: (0, n - 1 - i, 0)
    left = CONV_LEFT * bsz
    right = (CONV_W - 1 - CONV_LEFT) * bsz
    rows = tile * bsz
    ahead = lambda i: jnp.maximum(n - 2 - i, 0)
    before = lambda t: jnp.maximum(t * (tile // SUBLANES) - 1, 0)
    return pl.pallas_call(
        functools.partial(_scan_rev_kernel, bsz=bsz, tile=tile, n_tiles=n, mod_row=mod_row),
        grid=(n,),
        in_specs=[
            pl.BlockSpec((bsz, SUBLANES, D_MODEL), lambda i: (0, before(n - 1), 0),
                         pipeline_mode=pl.Buffered(1)),
            pl.BlockSpec((bsz, tile, D_MODEL), lambda i: (0, n - 1, 0), pipeline_mode=pl.Buffered(1)),
            pl.BlockSpec((bsz, SUBLANES, D_MODEL), lambda i: (0, before(ahead(i)), 0)),
            pl.BlockSpec((bsz, tile, D_MODEL), lambda i: (0, ahead(i), 0)),
            _const_spec((MOD_ROWS, 3 * D_MODEL)),
            _const_spec((1, D_MODEL)),
            _const_spec((D_MODEL, C_WIDTH)),
            _const_spec((CONV_W, C_WIDTH)),
            _const_spec((1, C_WIDTH)),
        ] + _gate_specs(),
        out_specs=[
            pl.BlockSpec((rows, C_WIDTH), lambda i: (n - 1 - i, 0)),
            pl.BlockSpec((N_SLABS, rows, LANES), rev),
            pl.BlockSpec((N_SLABS, SUBLANES, LANES), lambda i: (0, 0, 0)),
        ],
        out_shape=[
            jax.ShapeDtypeStruct((L * bsz, C_WIDTH), BF16),
            jax.ShapeDtypeStruct((N_SLABS, L * bsz, LANES), F32),
            jax.ShapeDtypeStruct((N_SLABS, SUBLANES, LANES), F32),
        ],
        scratch_shapes=[
            pltpu.VMEM((N_SLABS, left + rows, LANES), F32),
            pltpu.VMEM((N_SLABS, left + rows, LANES), F32),
            pltpu.VMEM((N_SLABS, right, LANES), F32),
            pltpu.VMEM((N_SLABS, rows, LANES), F32),
            pltpu.VMEM((N_SLABS, rows, LANES), F32),
            pltpu.VMEM((N_SLABS, SUBLANES, LANES), F32),
        ],
        compiler_params=pltpu.CompilerParams(
            dimension_semantics=("arbitrary",), vmem_limit_bytes=VMEM_LIMIT),
        name="odd_scan_rev",
    )(x, x, x, x, mod_l, g, w_x, conv_w, conv_b, w_a, w_i, b_a, b_i, lam, h0)


def _scan_forward(z, s_rev, x, mod_l, mod_row, g, w_g, w_a, w_i, b_a, b_i, lam, h0, w_out, final_g):
    bsz, L, _ = x.shape
    tile = T_SCAN_FWD
    n = L // tile
    fwd = lambda i: (0, i, 0)
    final_norm = final_g is not None
    fg = final_g if final_norm else jnp.ones((1, D_MODEL), F32)
    return pl.pallas_call(
        functools.partial(_scan_fwd_kernel, bsz=bsz, tile=tile, mod_row=mod_row, final_norm=final_norm),
        grid=(n,),
        in_specs=[
            pl.BlockSpec((tile * bsz, C_WIDTH), lambda i: (i, 0)),
            pl.BlockSpec((N_SLABS, tile * bsz, LANES), fwd),
            pl.BlockSpec((bsz, tile, D_MODEL), fwd),
            pl.BlockSpec((bsz, tile, D_MODEL), lambda i: (0, jnp.minimum(i + 1, n - 1), 0)),
            _const_spec((MOD_ROWS, 3 * D_MODEL)),
            _const_spec((1, D_MODEL)),
            _const_spec((D_MODEL, C_WIDTH)),
        ] + _gate_specs() + [_const_spec((C_WIDTH, D_MODEL)), _const_spec((1, D_MODEL))],
        out_specs=[
            pl.BlockSpec((bsz, tile, D_MODEL), fwd),
            pl.BlockSpec((N_SLABS, SUBLANES, LANES), lambda i: (0, 0, 0)),
        ],
        out_shape=[
            jax.ShapeDtypeStruct(x.shape, F32),
            jax.ShapeDtypeStruct((N_SLABS, SUBLANES, LANES), F32),
        ],
        scratch_shapes=[
            pltpu.VMEM((bsz * tile, C_WIDTH), F32),
            pltpu.VMEM((bsz * tile, C_WIDTH), F32),
            pltpu.VMEM((N_SLABS, tile * bsz, LANES), F32),
            pltpu.VMEM((N_SLABS, tile * bsz, LANES), F32),
            pltpu.VMEM((N_SLABS, tile * bsz, LANES), F32),
            pltpu.VMEM((N_SLABS, SUBLANES, LANES), F32),
            pltpu.VMEM((bsz * tile, C_WIDTH), BF16),
        ],
        compiler_params=pltpu.CompilerParams(
            dimension_semantics=("arbitrary",), vmem_limit_bytes=VMEM_LIMIT),
        name="odd_scan_fwd",
    )(z, s_rev, x, x, mod_l, g, w_g, w_a, w_i, b_a, b_i, lam, h0, w_out, fg)


def _rope_tables(L):
    rows = L // GRID_W
    r, col = jnp.meshgrid(jnp.arange(rows), jnp.arange(GRID_W), indexing="ij")
    r = r.reshape(-1).astype(F32)
    col = col.reshape(-1).astype(F32)
    n_freq = HEAD_DIM // 4
    inv_freq = ROPE_THETA ** (-jnp.arange(n_freq, dtype=F32) / n_freq)
    ang = jnp.concatenate([r[:, None] * inv_freq, col[:, None] * inv_freq], axis=-1)
    cos, sin = jnp.cos(ang), jnp.sin(ang)
    cosf = jnp.tile(jnp.concatenate([cos, cos], axis=-1), (1, LANES // HEAD_DIM))
    sinf = jnp.tile(jnp.concatenate([-sin, sin], axis=-1), (1, LANES // HEAD_DIM))
    return cosf, sinf


def _dup_heads(w):
    return jnp.concatenate([w[:, :HEAD_DIM], w[:, :HEAD_DIM], w[:, HEAD_DIM:], w[:, HEAD_DIM:]], axis=1)


def kernel(x, c, ctx, c_ctx, norm_g, w_mod, b_mod, ab_w_in, a_ln_g, a_ln_b, a_w_s, a_b_s, b_sink,
           ab_w_out, c_w_in, c_conv_w, c_conv_b, c_w_a, c_b_a, c_w_i, c_b_i, c_lam, c_w_out, final_g):
    bsz, L, d = x.shape
    ctx_len = ctx.shape[1]
    assert d == D_MODEL and bsz == SUBLANES and L % T_EVEN == 0 and DEPTH % 2 == 0
    assert all(n % t == 0 for n in (L, ctx_len) for t in (T_SCAN_REV, T_SCAN_FWD, BLOCK))
    ctx_row = bsz

    cin = jnp.concatenate([c, c_ctx[None, :], jnp.zeros((MOD_ROWS - bsz - 1, d), F32)], axis=0)
    mod = _modulation(cin, w_mod, b_mod).reshape(DEPTH, MOD_ROWS, 1, 3 * d)
    cosf, sinf = _rope_tables(L)
    zero_state = jnp.zeros((N_SLABS, SUBLANES, LANES), F32)

    xc = ctx
    for layer in range(DEPTH):
        need_ctx = layer < DEPTH - 1
        i = layer // 2
        mod_l = mod[layer]
        g = norm_g[layer][None, :]
        if layer % 2 == 0:
            w_in = ab_w_in[i].astype(BF16)
            w_kv = jnp.concatenate([_dup_heads(w_in[:, AB_K_OFF:AB_K_OFF + KV_WIDTH]),
                                    _dup_heads(w_in[:, AB_K_OFF + KV_WIDTH:AB_GB_OFF])], axis=1)
            w_out = ab_w_out[i].astype(BF16)
            lng = a_ln_g[i][None, :]
            lnb = a_ln_b[i][None, :]
            ws = a_w_s[i].astype(BF16)
            bs = jnp.broadcast_to(a_b_s[i][:, :, None], (A_GROUPS, CHUNK, LANES))
            sink = b_sink[i]
            kc, vc = _ctx_kv_project(xc, mod_l, ctx_row, g, w_kv)
            x = _even_mix(x, mod_l, None, g, w_in, w_out, lng, lnb, ws, bs, sink, kc, vc,
                          cosf=cosf, sinf=sinf, w_kv=w_kv, tile=T_EVEN)
            if need_ctx:
                xc = _even_mix(xc, mod_l, ctx_row, g, w_in, w_out, lng, lnb, ws, bs, sink, kc, vc,
                               tile=ctx_len)
        else:
            w_in = c_w_in[i].astype(BF16)
            w_x, w_g = w_in[:, :C_WIDTH], w_in[:, C_WIDTH:]
            w_out = c_w_out[i].astype(BF16)
            conv_w, conv_b = 0.5 * c_conv_w[i], 0.5 * c_conv_b[i][None, :]
            mod_flat = mod_l.reshape(MOD_ROWS, 3 * d)
            gate_args = [(c_w_a[i, dd].astype(BF16), c_w_i[i, dd].astype(BF16),
                          0.5 * c_b_a[i, dd][None, :], 0.5 * c_b_i[i, dd][None, :], c_lam[i, dd][None, :])
                         for dd in range(2)]
            z_c, s_rev_c, h0_rev = _scan_reverse(xc, mod_flat, ctx_row, g, w_x, conv_w, conv_b,
                                                 *gate_args[1], zero_state)
            xc_new, h0_fwd = _scan_forward(z_c, s_rev_c, xc, mod_flat, ctx_row, g, w_g, *gate_args[0],
                                           zero_state, w_out, None)
            z, s_rev, _ = _scan_reverse(x, mod_flat, None, g, w_x, conv_w, conv_b, *gate_args[1], h0_rev)
            fin = final_g[None, :] if layer == DEPTH - 1 else None
            x, _ = _scan_forward(z, s_rev, x, mod_flat, None, g, w_g, *gate_args[0], h0_fwd, w_out, fin)
            if need_ctx:
                xc = xc_new
    return x
```

```python
import functools

import jax
import jax.numpy as jnp
from jax import lax
from jax.experimental import pallas as pl
from jax.experimental.pallas import tpu as pltpu

F32 = jnp.float32
BF16 = jnp.bfloat16

D_MODEL = 1024
DEPTH = 4
GRID_W = 64
EPS = 1e-6
CHUNK = 128
A_GROUPS = 4
A_WIDTH = 512
HEAD_DIM = 64
B_Q_HEADS = 8
B_KV_HEADS = 2
B_GQA = B_Q_HEADS // B_KV_HEADS
B_WIDTH = 512
KV_WIDTH = 128
BLOCK = 128
ROPE_THETA = 10000.0
AB_Q_OFF = 3 * A_WIDTH
AB_K_OFF = 3 * A_WIDTH + B_WIDTH
AB_GB_OFF = AB_K_OFF + 2 * KV_WIDTH
C_WIDTH = 1024
C_HEADS = 4
C_BLOCK = 256
CONV_W = 4
CONV_LEFT = 2
LRU_C = 8.0
LOG2_E = 1.4426950408889634
MASKED = -1e30

LANES = 128
SUBLANES = 8
N_SLABS = C_WIDTH // LANES
MOD_ROWS = 16
VMEM_LIMIT = 56 * 1024 * 1024
KV_DUP = 2 * KV_WIDTH

T_EVEN = 512
T_SCAN = 128


def _sigmoid(x):
    return 0.5 * jnp.tanh(0.5 * x) + 0.5


def _silu(x):
    return x * _sigmoid(x)


def _gelu(x):
    return 0.5 * x * (1.0 + jnp.tanh(0.7978845608028654 * (x + 0.044715 * (x * x * x))))


def _norm_mod(x, g1s, shift):
    ms = jnp.mean(x * x, axis=-1, keepdims=True)
    return x * lax.rsqrt(ms + EPS) * g1s + shift


def _mod_parts(m):
    return m[:, 0:D_MODEL], m[:, D_MODEL:2 * D_MODEL], m[:, 2 * D_MODEL:3 * D_MODEL]


def _rope(x, cosf, sinf):
    n = x.shape[-1]
    lane = lax.broadcasted_iota(jnp.int32, x.shape, 1)
    first = (lane % HEAD_DIM) < (HEAD_DIM // 2)
    sw = jnp.where(first, pltpu.roll(x, n - HEAD_DIM // 2, 1), pltpu.roll(x, HEAD_DIM // 2, 1))
    return x * cosf + sw * sinf


def _dot(a, b):
    return jnp.dot(a, b, preferred_element_type=F32)


def _dot_t(a, b):
    return lax.dot_general(a, b, (((1,), (1,)), ((), ())), preferred_element_type=F32)


def _mod_kernel(c_ref, w_ref, b_ref, o_ref):
    s = _silu(c_ref[...])
    o_ref[...] = _dot(s.astype(BF16), w_ref[...].astype(BF16)) + b_ref[...]


def _modulation(cin, w_mod, b_mod):
    nj = 3
    return pl.pallas_call(
        _mod_kernel,
        grid=(DEPTH, nj),
        in_specs=[
            pl.BlockSpec((MOD_ROWS, D_MODEL), lambda l, j: (0, 0)),
            pl.BlockSpec((None, D_MODEL, D_MODEL), lambda l, j: (l, 0, j)),
            pl.BlockSpec((None, 1, D_MODEL), lambda l, j: (l, 0, j)),
        ],
        out_specs=pl.BlockSpec((None, MOD_ROWS, D_MODEL), lambda l, j: (l, 0, j)),
        out_shape=jax.ShapeDtypeStruct((DEPTH, MOD_ROWS, 3 * D_MODEL), F32),
        compiler_params=pltpu.CompilerParams(
            dimension_semantics=("arbitrary", "arbitrary"), vmem_limit_bytes=VMEM_LIMIT),
        name="modulation",
    )(cin, w_mod, b_mod.reshape(DEPTH, 1, 3 * D_MODEL))


def _ctx_kv_kernel(x_ref, m_ref, g_ref, w_ref, k_ref, v_ref):
    shift, scale, _ = _mod_parts(m_ref[...])
    h = _norm_mod(x_ref[...], g_ref[...] * (1.0 + scale), shift).astype(BF16)
    kv = _dot(h, w_ref[...])
    k_ref[...] = kv[:, 0:KV_DUP].astype(BF16)
    v_ref[...] = kv[:, KV_DUP:2 * KV_DUP].astype(BF16)


def _ctx_kv_project(xc, mod_l, mod_row, g, w_kv):
    bsz, lc, _ = xc.shape
    return pl.pallas_call(
        _ctx_kv_kernel,
        grid=(bsz,),
        in_specs=[
            pl.BlockSpec((None, lc, D_MODEL), lambda b: (b, 0, 0)),
            pl.BlockSpec((None, 1, 3 * D_MODEL), lambda b: (mod_row, 0, 0)),
            pl.BlockSpec((1, D_MODEL), lambda b: (0, 0)),
            pl.BlockSpec((D_MODEL, 2 * KV_DUP), lambda b: (0, 0)),
        ],
        out_specs=[
            pl.BlockSpec((None, lc, KV_DUP), lambda b: (b, 0, 0)),
            pl.BlockSpec((None, lc, KV_DUP), lambda b: (b, 0, 0)),
        ],
        out_shape=[jax.ShapeDtypeStruct((bsz, lc, KV_DUP), BF16)] * 2,
        compiler_params=pltpu.CompilerParams(
            dimension_semantics=("parallel",), vmem_limit_bytes=VMEM_LIMIT),
        name="even_ctx_kv",
    )(xc, mod_l, g, w_kv)


def _even_kernel(*refs, tile, n_blocks_total, latent):
    if latent:
        (x_ref, m_ref, g_ref, win_ref, wout_ref, lng_ref, lnb_ref, ws_ref, bs_ref, sink_ref,
         xn_ref, wkv_ref, cos_ref, cosn_ref, sin_ref, sinn_ref,
         kctx_ref, vctx_ref, o_ref, ga_scr, gb_scr, vln_scr, q_scr, mix_scr, kbuf, vbuf) = refs
    else:
        (x_ref, m_ref, g_ref, win_ref, wout_ref, lng_ref, lnb_ref, ws_ref, bs_ref, sink_ref,
         kctx_ref, vctx_ref, o_ref, ga_scr, gb_scr, vln_scr, q_scr, mix_scr) = refs
    nb = tile // BLOCK
    i = pl.program_id(1)
    rows4 = B_GQA * BLOCK

    x = x_ref[...]
    shift, scale, gate = _mod_parts(m_ref[...])
    g1s = g_ref[...] * (1.0 + scale)
    if latent:
        @pl.when(i == 0)
        def _():
            kbuf[0:BLOCK, :] = jnp.zeros((BLOCK, KV_DUP), BF16)
            vbuf[0:BLOCK, :] = jnp.zeros((BLOCK, KV_DUP), BF16)

        @pl.when(i > 0)
        def _():
            kbuf[0:BLOCK, :] = kbuf[tile:tile + BLOCK, :]
            vbuf[0:BLOCK, :] = vbuf[tile:tile + BLOCK, :]

        he = _norm_mod(jnp.concatenate([x, xn_ref[...]], axis=0), g1s, shift).astype(BF16)
        h = he[0:tile]
        kv = _dot(he, wkv_ref[...])
        cos_e = jnp.concatenate([cos_ref[...], cosn_ref[...]], axis=0)
        sin_e = jnp.concatenate([sin_ref[...], sinn_ref[...]], axis=0)
        for s in range(KV_DUP // LANES):
            sl = slice(s * LANES, (s + 1) * LANES)
            kbuf[BLOCK:, sl] = _rope(kv[:, sl], cos_e, sin_e).astype(BF16)
        vbuf[BLOCK:, :] = kv[:, KV_DUP:2 * KV_DUP].astype(BF16)
    else:
        h = _norm_mod(x, g1s, shift).astype(BF16)

    u = _dot(h, win_ref[:, 0:A_WIDTH])
    gate_a = _dot(h, win_ref[:, 2 * A_WIDTH:3 * A_WIDTH])
    ga_scr[...] = _gelu(u) * _silu(gate_a)
    v = _gelu(_dot(h, win_ref[:, A_WIDTH:2 * A_WIDTH]))
    for g in range(A_GROUPS):
        sl = slice(g * LANES, (g + 1) * LANES)
        vg = v[:, sl]
        mu = jnp.mean(vg, axis=-1, keepdims=True)
        vc = vg - mu
        var = jnp.mean(vc * vc, axis=-1, keepdims=True)
        vln_scr[:, sl] = (vc * lax.rsqrt(var + EPS) * lng_ref[:, sl] + lnb_ref[:, sl]).astype(BF16)
    q = _dot(h, win_ref[:, AB_Q_OFF:AB_Q_OFF + B_WIDTH])
    low_half = lax.broadcasted_iota(jnp.int32, (tile, LANES), 1) < HEAD_DIM
    for pair in range(B_Q_HEADS // 2):
        qs = q[:, pair * LANES:(pair + 1) * LANES]
        if latent:
            qs = _rope(qs, cos_ref[...], sin_ref[...])
        qs = qs * (HEAD_DIM ** -0.5 * LOG2_E)
        q_lo = jnp.where(low_half, qs, 0.0).astype(BF16)
        q_hi = jnp.where(low_half, 0.0, qs).astype(BF16)
        for jb in range(nb):
            rows = slice(jb * BLOCK, (jb + 1) * BLOCK)
            base = (jb * B_Q_HEADS + 2 * pair) * BLOCK
            q_scr[base:base + BLOCK, :] = q_lo[rows]
            q_scr[base + BLOCK:base + 2 * BLOCK, :] = q_hi[rows]
    gb_scr[...] = _silu(_dot(h, win_ref[:, AB_GB_OFF:AB_GB_OFF + B_WIDTH]))

    if latent:
        qi = lax.broadcasted_iota(jnp.int32, (rows4, 3 * BLOCK), 0) & (BLOCK - 1)
        kj = lax.broadcasted_iota(jnp.int32, (rows4, 3 * BLOCK), 1)
        band = (kj >= qi) & (kj <= qi + 2 * BLOCK)
    out_low = lax.broadcasted_iota(jnp.int32, (BLOCK, LANES), 1) < HEAD_DIM

    for jb in range(nb):
        rows = slice(jb * BLOCK, (jb + 1) * BLOCK)
        for g in range(A_GROUPS):
            sl = slice(g * LANES, (g + 1) * LANES)
            sv = _dot(ws_ref[g], vln_scr[rows, sl]) + bs_ref[g]
            mix_scr[rows, sl] = (ga_scr[rows, sl] * sv).astype(BF16)
        if latent:
            mask = band
            if jb == 0:
                mask = mask & (kj >= jnp.where(i > 0, 0, BLOCK))
            if jb == nb - 1:
                mask = mask & (kj < jnp.where(i < n_blocks_total // nb - 1, 3 * BLOCK, 2 * BLOCK))
        for kvh in range(B_KV_HEADS):
            ks = slice(kvh * LANES, (kvh + 1) * LANES)
            base = (jb * B_Q_HEADS + kvh * B_GQA) * BLOCK
            q4 = q_scr[base:base + rows4, :]
            sink = jnp.concatenate(
                [jnp.full((BLOCK, 1), sink_ref[kvh * B_GQA + j] * LOG2_E, F32) for j in range(B_GQA)],
                axis=0)
            s_c = _dot_t(q4, kctx_ref[:, ks])
            m = jnp.maximum(sink, jnp.max(s_c, axis=-1, keepdims=True))
            if latent:
                s_l = jnp.where(mask, _dot_t(q4, kbuf[jb * BLOCK:(jb + 3) * BLOCK, ks]), MASKED)
                m = jnp.maximum(m, jnp.max(s_l, axis=-1, keepdims=True))
            p_c = jnp.exp2(s_c - m)
            den = jnp.exp2(sink - m) + jnp.sum(p_c, axis=-1, keepdims=True)
            o = _dot(p_c.astype(BF16), vctx_ref[:, ks])
            if latent:
                p_l = jnp.exp2(s_l - m)
                den = den + jnp.sum(p_l, axis=-1, keepdims=True)
                o = o + _dot(p_l.astype(BF16), vbuf[jb * BLOCK:(jb + 3) * BLOCK, ks])
            o = o * (1.0 / den)
            for hp in range(B_GQA // 2):
                pair_out = jnp.where(out_low, o[2 * hp * BLOCK:(2 * hp + 1) * BLOCK],
                                     o[(2 * hp + 1) * BLOCK:(2 * hp + 2) * BLOCK])
                col = (kvh * (B_GQA // 2) + hp) * LANES
                yb = pair_out * gb_scr[rows, col:col + LANES]
                mix_scr[rows, A_WIDTH + col:A_WIDTH + col + LANES] = yb.astype(BF16)

    y = _dot(mix_scr[...], wout_ref[...])
    o_ref[...] = x + gate * y


def _even_mix(x, mod_l, mod_row, g, w_in, w_out, lng, lnb, ws, bs, sink, kctx, vctx,
              cosf=None, sinf=None, w_kv=None, *, tile):
    latent = w_kv is not None
    bsz, L, _ = x.shape
    n = L // tile
    nb = tile // BLOCK
    nblk = L // BLOCK
    row = (lambda b: b) if mod_row is None else (lambda b: mod_row)
    full = lambda shape: pl.BlockSpec(shape, lambda b, i: (0,) * len(shape), pipeline_mode=pl.Buffered(1))
    in_specs = [
        pl.BlockSpec((None, tile, D_MODEL), lambda b, i: (b, i, 0)),
        pl.BlockSpec((None, 1, 3 * D_MODEL), lambda b, i: (row(b), 0, 0)),
        full((1, D_MODEL)),
        full(w_in.shape),
        full(w_out.shape),
        full((1, A_WIDTH)),
        full((1, A_WIDTH)),
        full(ws.shape),
        full(bs.shape),
        pl.BlockSpec(memory_space=pltpu.SMEM),
    ]
    args = [x, mod_l, g, w_in, w_out, lng, lnb, ws, bs, sink]
    scratch = [
        pltpu.VMEM((tile, A_WIDTH), F32),
        pltpu.VMEM((tile, B_WIDTH), F32),
        pltpu.VMEM((tile, A_WIDTH), BF16),
        pltpu.VMEM((nb * B_Q_HEADS * BLOCK, LANES), BF16),
        pltpu.VMEM((tile, A_WIDTH + B_WIDTH), BF16),
    ]
    if latent:
        next_blk = lambda i: jnp.minimum((i + 1) * nb, nblk - 1)
        table = [pl.BlockSpec((tile, LANES), lambda b, i: (i, 0)),
                 pl.BlockSpec((BLOCK, LANES), lambda b, i: (next_blk(i), 0))]
        in_specs += [
            pl.BlockSpec((None, BLOCK, D_MODEL), lambda b, i: (b, next_blk(i), 0)),
            full(w_kv.shape),
        ] + table + table
        args += [x, w_kv, cosf, cosf, sinf, sinf]
        scratch += [pltpu.VMEM((tile + 2 * BLOCK, KV_DUP), BF16)] * 2
    ctx_len = kctx.shape[1]
    in_specs += [pl.BlockSpec((None, ctx_len, KV_DUP), lambda b, i: (b, 0, 0))] * 2
    args += [kctx, vctx]
    return pl.pallas_call(
        functools.partial(_even_kernel, tile=tile, n_blocks_total=nblk, latent=latent),
        grid=(bsz, n),
        in_specs=in_specs,
        out_specs=pl.BlockSpec((None, tile, D_MODEL), lambda b, i: (b, i, 0)),
        out_shape=jax.ShapeDtypeStruct(x.shape, F32),
        scratch_shapes=scratch,
        compiler_params=pltpu.CompilerParams(
            dimension_semantics=("arbitrary", "arbitrary"), vmem_limit_bytes=VMEM_LIMIT),
        name="even_mix" if latent else "even_mix_ctx",
    )(*args)


def _mod_row_slices(m, g, bsz, mod_row):
    out = []
    for b in range(bsz):
        r = b if mod_row is None else mod_row
        out.append((g * (1.0 + m[r:r + 1, D_MODEL:2 * D_MODEL]), m[r:r + 1, 0:D_MODEL]))
    return out


def _half_decay_rate(lam_ref):
    neg = -lam_ref[...]
    return (0.5 * LRU_C) * (jnp.maximum(neg, 0.0) + jnp.log1p(jnp.exp(-jnp.abs(neg))))


def _gates(hz, hzb, hd, wa_ref, wi_ref, ba_ref, bi_ref, hsp, a_scr, bx_ref):
    cols = slice(hd * C_BLOCK, (hd + 1) * C_BLOCK)
    tr = jnp.tanh(_dot(hzb, wa_ref[hd]) + ba_ref[:, cols])
    ti = jnp.tanh(_dot(hzb, wi_ref[hd]) + bi_ref[:, cols])
    hs = hsp[:, cols]
    nla = tr * hs + hs
    a = jnp.exp2(nla * (-LOG2_E))
    y = jnp.tanh(nla) * (a * a + 1.0)
    root = jnp.where(y > 0.0, y * lax.rsqrt(y), 0.0)
    bx = root * (hz * ti + hz)
    for half in range(C_BLOCK // LANES):
        slab = hd * (C_BLOCK // LANES) + half
        ls = slice(half * LANES, (half + 1) * LANES)
        a_scr[slab] = a[:, ls]
        bx_ref[slab] = bx[:, ls]


def _scan_tile(a_scr, bx_scr, out_ref, h_scr, *, tile, reverse, add_ref=None):
    def body(kk, hs):
        t = (tile - 1 - kk) if reverse else kk
        r = pl.multiple_of(t * SUBLANES, SUBLANES)
        new = []
        for c in range(N_SLABS):
            hc = a_scr[c, pl.ds(r, SUBLANES), :] * hs[c] + bx_scr[c, pl.ds(r, SUBLANES), :]
            if add_ref is None:
                out_ref[c, pl.ds(r, SUBLANES), :] = hc
            else:
                out_ref[c, pl.ds(r, SUBLANES), :] = hc + add_ref[c, pl.ds(r, SUBLANES), :]
            new.append(hc)
        return tuple(new)

    hs = lax.fori_loop(0, tile, body, tuple(h_scr[c] for c in range(N_SLABS)), unroll=4)
    for c in range(N_SLABS):
        h_scr[c] = hs[c]


def _scan_rev_kernel(xh_ref, x_ref, m_ref, g_ref, wx_ref, cw_ref, cb_ref, wa_ref, wi_ref, ba_ref, bi_ref,
                     lam_ref, h0_ref, z_ref, s_ref, hfin_ref, ext_scr, a_scr, bx_scr, h_scr,
                     *, bsz, tile, n_tiles, mod_row):
    step = pl.program_id(0)
    tidx = n_tiles - 1 - step
    rows = tile * bsz
    left = CONV_LEFT * bsz
    right = (CONV_W - 1 - CONV_LEFT) * bsz

    @pl.when(step == 0)
    def _():
        h_scr[...] = h0_ref[...]
        ext_scr[:, left + rows:left + rows + right, :] = jnp.zeros((N_SLABS, right, LANES), F32)

    @pl.when(step > 0)
    def _():
        ext_scr[:, left + rows:left + rows + right, :] = ext_scr[:, left:left + right, :]

    mods = _mod_row_slices(m_ref[...], g_ref[...], bsz, mod_row)
    hs = [_norm_mod(x_ref[b], *mods[b]) for b in range(bsz)]
    for j in range(CONV_LEFT):
        r = SUBLANES - CONV_LEFT + j
        hs += [_norm_mod(xh_ref[b, r:r + 1, :], *mods[b]) for b in range(bsz)]
    xr = _dot(jnp.concatenate(hs, axis=0).astype(BF16), wx_ref[...])
    for c in range(N_SLABS):
        ls = slice(c * LANES, (c + 1) * LANES)
        for b in range(bsz):
            ext_scr[c, pl.ds(left + b, tile, stride=bsz), :] = xr[b * tile:(b + 1) * tile, ls]
        ext_scr[c, 0:left, :] = jnp.where(tidx > 0, xr[rows:rows + left, ls], 0.0)

    hsp = _half_decay_rate(lam_ref)
    for hd in range(C_HEADS):
        zs = []
        for half in range(C_BLOCK // LANES):
            c = hd * (C_BLOCK // LANES) + half
            ls = slice(c * LANES, (c + 1) * LANES)
            ext = ext_scr[c]
            z = cb_ref[:, ls]
            for j in range(CONV_W):
                z = z + cw_ref[j:j + 1, ls] * ext[j * bsz:j * bsz + rows]
            zs.append(z)
        hz = jnp.concatenate(zs, axis=1)
        hzb = hz.astype(BF16)
        z_ref[:, hd * C_BLOCK:(hd + 1) * C_BLOCK] = hzb
        _gates(hz, hzb, hd, wa_ref, wi_ref, ba_ref, bi_ref, hsp, a_scr, bx_scr)
    _scan_tile(a_scr, bx_scr, s_ref, h_scr, tile=tile, reverse=True)
    hfin_ref[...] = h_scr[...]


def _scan_fwd_kernel(z_ref, srev_ref, x_ref, m_ref, g_ref, wg_ref, wa_ref, wi_ref, ba_ref, bi_ref, lam_ref,
                     h0_ref, wout_ref, fg_ref, o_ref, hfin_ref, a_scr, bx_scr, h_scr, mix_scr,
                     sg_scr, *, bsz, tile, mod_row, final_norm):
    @pl.when(pl.program_id(0) == 0)
    def _():
        h_scr[...] = h0_ref[...]

    m = m_ref[...]
    mods = _mod_row_slices(m, g_ref[...], bsz, mod_row)
    he = jnp.concatenate([_norm_mod(x_ref[b], *mods[b]) for b in range(bsz)], axis=0).astype(BF16)
    sg_scr[...] = _silu(_dot(he, wg_ref[...])).astype(BF16)

    hsp = _half_decay_rate(lam_ref)
    for hd in range(C_HEADS):
        hzb = z_ref[:, hd * C_BLOCK:(hd + 1) * C_BLOCK]
        _gates(hzb.astype(F32), hzb, hd, wa_ref, wi_ref, ba_ref, bi_ref, hsp, a_scr, bx_scr)
    _scan_tile(a_scr, bx_scr, a_scr, h_scr, tile=tile, reverse=False, add_ref=srev_ref)
    hfin_ref[...] = h_scr[...]

    for b in range(bsz):
        rs = slice(b * tile, (b + 1) * tile)
        for c in range(N_SLABS):
            ls = slice(c * LANES, (c + 1) * LANES)
            hsum = a_scr[c, pl.ds(b, tile, stride=bsz), :]
            mix_scr[rs, ls] = (hsum * sg_scr[rs, ls]).astype(BF16)
    y = _dot(mix_scr[...], wout_ref[...])
    for b in range(bsz):
        r = b if mod_row is None else mod_row
        gate = m[r:r + 1, 2 * D_MODEL:3 * D_MODEL]
        xn = x_ref[b] + gate * y[b * tile:(b + 1) * tile]
        if final_norm:
            ms = jnp.mean(xn * xn, axis=-1, keepdims=True)
            xn = xn * lax.rsqrt(ms + EPS) * fg_ref[...]
        o_ref[b] = xn


def _const_spec(shape):
    return pl.BlockSpec(shape, lambda i: (0,) * len(shape), pipeline_mode=pl.Buffered(1))


def _gate_specs():
    return [
        _const_spec((C_HEADS, C_BLOCK, C_BLOCK)),
        _const_spec((C_HEADS, C_BLOCK, C_BLOCK)),
        _const_spec((1, C_WIDTH)),
        _const_spec((1, C_WIDTH)),
        _const_spec((1, C_WIDTH)),
        _const_spec((N_SLABS, SUBLANES, LANES)),
    ]


def _scan_reverse(x, mod_l, mod_row, g, w_x, conv_w, conv_b, w_a, w_i, b_a, b_i, lam, h0):
    bsz, L, _ = x.shape
    tile = T_SCAN
    n = L // tile
    rev = lambda i: (0, n - 1 - i, 0)
    left = CONV_LEFT * bsz
    right = (CONV_W - 1 - CONV_LEFT) * bsz
    rows = tile * bsz
    return pl.pallas_call(
        functools.partial(_scan_rev_kernel, bsz=bsz, tile=tile, n_tiles=n, mod_row=mod_row),
        grid=(n,),
        in_specs=[
            pl.BlockSpec((bsz, SUBLANES, D_MODEL),
                         lambda i: (0, jnp.maximum((n - 1 - i) * (tile // SUBLANES) - 1, 0), 0)),
            pl.BlockSpec((bsz, tile, D_MODEL), rev),
            _const_spec((MOD_ROWS, 3 * D_MODEL)),
            _const_spec((1, D_MODEL)),
            _const_spec((D_MODEL, C_WIDTH)),
            _const_spec((CONV_W, C_WIDTH)),
            _const_spec((1, C_WIDTH)),
        ] + _gate_specs(),
        out_specs=[
            pl.BlockSpec((rows, C_WIDTH), lambda i: (n - 1 - i, 0)),
            pl.BlockSpec((N_SLABS, rows, LANES), rev),
            pl.BlockSpec((N_SLABS, SUBLANES, LANES), lambda i: (0, 0, 0)),
        ],
        out_shape=[
            jax.ShapeDtypeStruct((L * bsz, C_WIDTH), BF16),
            jax.ShapeDtypeStruct((N_SLABS, L * bsz, LANES), F32),
            jax.ShapeDtypeStruct((N_SLABS, SUBLANES, LANES), F32),
        ],
        scratch_shapes=[
            pltpu.VMEM((N_SLABS, left + rows + right, LANES), F32),
            pltpu.VMEM((N_SLABS, rows, LANES), F32),
            pltpu.VMEM((N_SLABS, rows, LANES), F32),
            pltpu.VMEM((N_SLABS, SUBLANES, LANES), F32),
        ],
        compiler_params=pltpu.CompilerParams(
            dimension_semantics=("arbitrary",), vmem_limit_bytes=VMEM_LIMIT),
        name="odd_scan_rev",
    )(x, x, mod_l, g, w_x, conv_w, conv_b, w_a, w_i, b_a, b_i, lam, h0)


def _scan_forward(z, s_rev, x, mod_l, mod_row, g, w_g, w_a, w_i, b_a, b_i, lam, h0, w_out, final_g):
    bsz, L, _ = x.shape
    tile = T_SCAN
    n = L // tile
    fwd = lambda i: (0, i, 0)
    final_norm = final_g is not None
    fg = final_g if final_norm else jnp.ones((1, D_MODEL), F32)
    return pl.pallas_call(
        functools.partial(_scan_fwd_kernel, bsz=bsz, tile=tile, mod_row=mod_row, final_norm=final_norm),
        grid=(n,),
        in_specs=[
            pl.BlockSpec((tile * bsz, C_WIDTH), lambda i: (i, 0)),
            pl.BlockSpec((N_SLABS, tile * bsz, LANES), fwd),
            pl.BlockSpec((bsz, tile, D_MODEL), fwd),
            _const_spec((MOD_ROWS, 3 * D_MODEL)),
            _const_spec((1, D_MODEL)),
            _const_spec((D_MODEL, C_WIDTH)),
        ] + _gate_specs() + [_const_spec((C_WIDTH, D_MODEL)), _const_spec((1, D_MODEL))],
        out_specs=[
            pl.BlockSpec((bsz, tile, D_MODEL), fwd),
            pl.BlockSpec((N_SLABS, SUBLANES, LANES), lambda i: (0, 0, 0)),
        ],
        out_shape=[
            jax.ShapeDtypeStruct(x.shape, F32),
            jax.ShapeDtypeStruct((N_SLABS, SUBLANES, LANES), F32),
        ],
        scratch_shapes=[
            pltpu.VMEM((N_SLABS, tile * bsz, LANES), F32),
            pltpu.VMEM((N_SLABS, tile * bsz, LANES), F32),
            pltpu.VMEM((N_SLABS, SUBLANES, LANES), F32),
            pltpu.VMEM((bsz * tile, C_WIDTH), BF16),
            pltpu.VMEM((bsz * tile, C_WIDTH), BF16),
        ],
        compiler_params=pltpu.CompilerParams(
            dimension_semantics=("arbitrary",), vmem_limit_bytes=VMEM_LIMIT),
        name="odd_scan_fwd",
    )(z, s_rev, x, mod_l, g, w_g, w_a, w_i, b_a, b_i, lam, h0, w_out, fg)


def _rope_tables(L):
    rows = L // GRID_W
    r, col = jnp.meshgrid(jnp.arange(rows), jnp.arange(GRID_W), indexing="ij")
    r = r.reshape(-1).astype(F32)
    col = col.reshape(-1).astype(F32)
    n_freq = HEAD_DIM // 4
    inv_freq = ROPE_THETA ** (-jnp.arange(n_freq, dtype=F32) / n_freq)
    ang = jnp.concatenate([r[:, None] * inv_freq, col[:, None] * inv_freq], axis=-1)
    cos, sin = jnp.cos(ang), jnp.sin(ang)
    cosf = jnp.tile(jnp.concatenate([cos, cos], axis=-1), (1, LANES // HEAD_DIM))
    sinf = jnp.tile(jnp.concatenate([-sin, sin], axis=-1), (1, LANES // HEAD_DIM))
    return cosf, sinf


def _dup_heads(w):
    return jnp.concatenate([w[:, :HEAD_DIM], w[:, :HEAD_DIM], w[:, HEAD_DIM:], w[:, HEAD_DIM:]], axis=1)


def kernel(x, c, ctx, c_ctx, norm_g, w_mod, b_mod, ab_w_in, a_ln_g, a_ln_b, a_w_s, a_b_s, b_sink,
           ab_w_out, c_w_in, c_conv_w, c_conv_b, c_w_a, c_b_a, c_w_i, c_b_i, c_lam, c_w_out, final_g):
    bsz, L, d = x.shape
    ctx_len = ctx.shape[1]
    assert d == D_MODEL and bsz == SUBLANES and L % T_SCAN == 0 and L % T_EVEN == 0
    assert ctx_len % BLOCK == 0 and ctx_len % T_SCAN == 0 and DEPTH % 2 == 0
    ctx_row = bsz

    cin = jnp.concatenate([c, c_ctx[None, :], jnp.zeros((MOD_ROWS - bsz - 1, d), F32)], axis=0)
    mod = _modulation(cin, w_mod, b_mod).reshape(DEPTH, MOD_ROWS, 1, 3 * d)
    cosf, sinf = _rope_tables(L)
    zero_state = jnp.zeros((N_SLABS, SUBLANES, LANES), F32)

    xc = ctx
    for layer in range(DEPTH):
        need_ctx = layer < DEPTH - 1
        i = layer // 2
        mod_l = mod[layer]
        g = norm_g[layer][None, :]
        if layer % 2 == 0:
            w_in = ab_w_in[i].astype(BF16)
            w_kv = jnp.concatenate([_dup_heads(w_in[:, AB_K_OFF:AB_K_OFF + KV_WIDTH]),
                                    _dup_heads(w_in[:, AB_K_OFF + KV_WIDTH:AB_GB_OFF])], axis=1)
            w_out = ab_w_out[i].astype(BF16)
            lng = a_ln_g[i][None, :]
            lnb = a_ln_b[i][None, :]
            ws = a_w_s[i].astype(BF16)
            bs = jnp.broadcast_to(a_b_s[i][:, :, None], (A_GROUPS, CHUNK, LANES))
            sink = b_sink[i]
            kc, vc = _ctx_kv_project(xc, mod_l, ctx_row, g, w_kv)
            x = _even_mix(x, mod_l, None, g, w_in, w_out, lng, lnb, ws, bs, sink, kc, vc,
                          cosf=cosf, sinf=sinf, w_kv=w_kv, tile=T_EVEN)
            if need_ctx:
                xc = _even_mix(xc, mod_l, ctx_row, g, w_in, w_out, lng, lnb, ws, bs, sink, kc, vc,
                               tile=ctx_len)
        else:
            w_in = c_w_in[i].astype(BF16)
            w_x, w_g = w_in[:, :C_WIDTH], w_in[:, C_WIDTH:]
            w_out = c_w_out[i].astype(BF16)
            conv_w, conv_b = 0.5 * c_conv_w[i], 0.5 * c_conv_b[i][None, :]
            mod_flat = mod_l.reshape(MOD_ROWS, 3 * d)
            gate_args = [(c_w_a[i, dd].astype(BF16), c_w_i[i, dd].astype(BF16),
                          0.5 * c_b_a[i, dd][None, :], 0.5 * c_b_i[i, dd][None, :], c_lam[i, dd][None, :])
                         for dd in range(2)]
            z_c, s_rev_c, h0_rev = _scan_reverse(xc, mod_flat, ctx_row, g, w_x, conv_w, conv_b,
                                                 *gate_args[1], zero_state)
            xc_new, h0_fwd = _scan_forward(z_c, s_rev_c, xc, mod_flat, ctx_row, g, w_g, *gate_args[0],
                                           zero_state, w_out, None)
            z, s_rev, _ = _scan_reverse(x, mod_flat, None, g, w_x, conv_w, conv_b, *gate_args[1], h0_rev)
            fin = final_g[None, :] if layer == DEPTH - 1 else None
            x, _ = _scan_forward(z, s_rev, x, mod_flat, None, g, w_g, *gate_args[0], h0_fwd, w_out, fin)
            if need_ctx:
                xc = xc_new
    return x
```

```python
import functools

import jax
import jax.numpy as jnp
from jax import lax
from jax.experimental import pallas as pl
from jax.experimental.pallas import tpu as pltpu

F32 = jnp.float32
BF16 = jnp.bfloat16

D_MODEL = 1024
DEPTH = 4
GRID_W = 64
EPS = 1e-6
CHUNK = 128
A_GROUPS = 4
A_WIDTH = 512
HEAD_DIM = 64
B_Q_HEADS = 8
B_KV_HEADS = 2
B_GQA = B_Q_HEADS // B_KV_HEADS
B_WIDTH = 512
KV_WIDTH = 128
BLOCK = 128
ROPE_THETA = 10000.0
AB_Q_OFF = 3 * A_WIDTH
AB_K_OFF = 3 * A_WIDTH + B_WIDTH
AB_GB_OFF = AB_K_OFF + 2 * KV_WIDTH
C_WIDTH = 1024
C_HEADS = 4
C_BLOCK = 256
CONV_W = 4
CONV_LEFT = 2
LRU_C = 8.0
LOG2_E = 1.4426950408889634
MASKED = -1e30

LANES = 128
SUBLANES = 8
N_SLABS = C_WIDTH // LANES
MOD_ROWS = 16
VMEM_LIMIT = 56 * 1024 * 1024
KV_DUP = 2 * KV_WIDTH

T_EVEN = 512
T_SCAN = 128


def _sigmoid(x):
    return 0.5 * jnp.tanh(0.5 * x) + 0.5


def _silu(x):
    return x * _sigmoid(x)


def _gelu(x):
    return 0.5 * x * (1.0 + jnp.tanh(0.7978845608028654 * (x + 0.044715 * (x * x * x))))


def _norm_mod(x, g1s, shift):
    ms = jnp.mean(x * x, axis=-1, keepdims=True)
    return x * lax.rsqrt(ms + EPS) * g1s + shift


def _mod_parts(m):
    return m[:, 0:D_MODEL], m[:, D_MODEL:2 * D_MODEL], m[:, 2 * D_MODEL:3 * D_MODEL]


def _rope(x, cosf, sinf):
    n = x.shape[-1]
    lane = lax.broadcasted_iota(jnp.int32, x.shape, 1)
    first = (lane % HEAD_DIM) < (HEAD_DIM // 2)
    sw = jnp.where(first, pltpu.roll(x, n - HEAD_DIM // 2, 1), pltpu.roll(x, HEAD_DIM // 2, 1))
    return x * cosf + sw * sinf


def _dot(a, b):
    return jnp.dot(a, b, preferred_element_type=F32)


def _dot_t(a, b):
    return lax.dot_general(a, b, (((1,), (1,)), ((), ())), preferred_element_type=F32)


def _mod_kernel(c_ref, w_ref, b_ref, o_ref):
    s = _silu(c_ref[...])
    o_ref[...] = _dot(s.astype(BF16), w_ref[...].astype(BF16)) + b_ref[...]


def _modulation(cin, w_mod, b_mod):
    nj = 3
    return pl.pallas_call(
        _mod_kernel,
        grid=(DEPTH, nj),
        in_specs=[
            pl.BlockSpec((MOD_ROWS, D_MODEL), lambda l, j: (0, 0)),
            pl.BlockSpec((None, D_MODEL, D_MODEL), lambda l, j: (l, 0, j)),
            pl.BlockSpec((None, 1, D_MODEL), lambda l, j: (l, 0, j)),
        ],
        out_specs=pl.BlockSpec((None, MOD_ROWS, D_MODEL), lambda l, j: (l, 0, j)),
        out_shape=jax.ShapeDtypeStruct((DEPTH, MOD_ROWS, 3 * D_MODEL), F32),
        compiler_params=pltpu.CompilerParams(
            dimension_semantics=("arbitrary", "arbitrary"), vmem_limit_bytes=VMEM_LIMIT),
        name="modulation",
    )(cin, w_mod, b_mod.reshape(DEPTH, 1, 3 * D_MODEL))


def _ctx_kv_kernel(x_ref, m_ref, g_ref, w_ref, k_ref, v_ref):
    shift, scale, _ = _mod_parts(m_ref[...])
    h = _norm_mod(x_ref[...], g_ref[...] * (1.0 + scale), shift).astype(BF16)
    kv = _dot(h, w_ref[...])
    k_ref[...] = kv[:, 0:KV_DUP].astype(BF16)
    v_ref[...] = kv[:, KV_DUP:2 * KV_DUP].astype(BF16)


def _ctx_kv_project(xc, mod_l, mod_row, g, w_kv):
    bsz, lc, _ = xc.shape
    return pl.pallas_call(
        _ctx_kv_kernel,
        grid=(bsz,),
        in_specs=[
            pl.BlockSpec((None, lc, D_MODEL), lambda b: (b, 0, 0)),
            pl.BlockSpec((None, 1, 3 * D_MODEL), lambda b: (mod_row, 0, 0)),
            pl.BlockSpec((1, D_MODEL), lambda b: (0, 0)),
            pl.BlockSpec((D_MODEL, 2 * KV_DUP), lambda b: (0, 0)),
        ],
        out_specs=[
            pl.BlockSpec((None, lc, KV_DUP), lambda b: (b, 0, 0)),
            pl.BlockSpec((None, lc, KV_DUP), lambda b: (b, 0, 0)),
        ],
        out_shape=[jax.ShapeDtypeStruct((bsz, lc, KV_DUP), BF16)] * 2,
        compiler_params=pltpu.CompilerParams(
            dimension_semantics=("parallel",), vmem_limit_bytes=VMEM_LIMIT),
        name="even_ctx_kv",
    )(xc, mod_l, g, w_kv)


def _even_kernel(*refs, tile, n_blocks_total, latent):
    if latent:
        (x_ref, m_ref, g_ref, win_ref, wout_ref, lng_ref, lnb_ref, ws_ref, bs_ref, sink_ref,
         xn_ref, wkv_ref, cos_ref, cosn_ref, sin_ref, sinn_ref,
         kctx_ref, vctx_ref, o_ref, ga_scr, gb_scr, vln_scr, q_scr, mix_scr, kbuf, vbuf) = refs
    else:
        (x_ref, m_ref, g_ref, win_ref, wout_ref, lng_ref, lnb_ref, ws_ref, bs_ref, sink_ref,
         kctx_ref, vctx_ref, o_ref, ga_scr, gb_scr, vln_scr, q_scr, mix_scr) = refs
    nb = tile // BLOCK
    i = pl.program_id(1)
    rows4 = B_GQA * BLOCK

    x = x_ref[...]
    shift, scale, gate = _mod_parts(m_ref[...])
    g1s = g_ref[...] * (1.0 + scale)
    if latent:
        @pl.when(i == 0)
        def _():
            kbuf[0:BLOCK, :] = jnp.zeros((BLOCK, KV_DUP), BF16)
            vbuf[0:BLOCK, :] = jnp.zeros((BLOCK, KV_DUP), BF16)

        @pl.when(i > 0)
        def _():
            kbuf[0:BLOCK, :] = kbuf[tile:tile + BLOCK, :]
            vbuf[0:BLOCK, :] = vbuf[tile:tile + BLOCK, :]

        he = _norm_mod(jnp.concatenate([x, xn_ref[...]], axis=0), g1s, shift).astype(BF16)
        h = he[0:tile]
        kv = _dot(he, wkv_ref[...])
        cos_e = jnp.concatenate([cos_ref[...], cosn_ref[...]], axis=0)
        sin_e = jnp.concatenate([sin_ref[...], sinn_ref[...]], axis=0)
        for s in range(KV_DUP // LANES):
            sl = slice(s * LANES, (s + 1) * LANES)
            kbuf[BLOCK:, sl] = _rope(kv[:, sl], cos_e, sin_e).astype(BF16)
        vbuf[BLOCK:, :] = kv[:, KV_DUP:2 * KV_DUP].astype(BF16)
    else:
        h = _norm_mod(x, g1s, shift).astype(BF16)

    v = _gelu(_dot(h, win_ref[:, A_WIDTH:2 * A_WIDTH]))
    for g in range(A_GROUPS):
        sl = slice(g * LANES, (g + 1) * LANES)
        vg = v[:, sl]
        mu = jnp.mean(vg, axis=-1, keepdims=True)
        vc = vg - mu
        var = jnp.mean(vc * vc, axis=-1, keepdims=True)
        vln_scr[:, sl] = (vc * lax.rsqrt(var + EPS) * lng_ref[:, sl] + lnb_ref[:, sl]).astype(BF16)
    q = _dot(h, win_ref[:, AB_Q_OFF:AB_Q_OFF + B_WIDTH])
    low_half = lax.broadcasted_iota(jnp.int32, (tile, LANES), 1) < HEAD_DIM
    for pair in range(B_Q_HEADS // 2):
        qs = q[:, pair * LANES:(pair + 1) * LANES]
        if latent:
            qs = _rope(qs, cos_ref[...], sin_ref[...])
        qs = qs * (HEAD_DIM ** -0.5 * LOG2_E)
        q_lo = jnp.where(low_half, qs, 0.0).astype(BF16)
        q_hi = jnp.where(low_half, 0.0, qs).astype(BF16)
        for jb in range(nb):
            rows = slice(jb * BLOCK, (jb + 1) * BLOCK)
            base = (jb * B_Q_HEADS + 2 * pair) * BLOCK
            q_scr[base:base + BLOCK, :] = q_lo[rows]
            q_scr[base + BLOCK:base + 2 * BLOCK, :] = q_hi[rows]

    if latent:
        qi = lax.broadcasted_iota(jnp.int32, (rows4, 3 * BLOCK), 0) & (BLOCK - 1)
        kj = lax.broadcasted_iota(jnp.int32, (rows4, 3 * BLOCK), 1)
        band = (kj >= qi) & (kj <= qi + 2 * BLOCK)
    out_low = lax.broadcasted_iota(jnp.int32, (BLOCK, LANES), 1) < HEAD_DIM

    def spatial_gate(jb, part):
        rows = slice(jb * BLOCK, (jb + 1) * BLOCK)
        for g in range(part * A_GROUPS // B_KV_HEADS, (part + 1) * A_GROUPS // B_KV_HEADS):
            sl = slice(g * LANES, (g + 1) * LANES)
            sv = _dot(ws_ref[g], vln_scr[rows, sl]) + bs_ref[g]
            mix_scr[rows, sl] = (ga_scr[rows, sl] * sv).astype(BF16)

    def scores(jb, kvh):
        ks = slice(kvh * LANES, (kvh + 1) * LANES)
        base = (jb * B_Q_HEADS + kvh * B_GQA) * BLOCK
        q4 = q_scr[base:base + rows4, :]
        sink = jnp.concatenate(
            [jnp.full((BLOCK, 1), sink_ref[kvh * B_GQA + j] * LOG2_E, F32) for j in range(B_GQA)], axis=0)
        s_c = _dot_t(q4, kctx_ref[:, ks])
        s_l = None
        if latent:
            mask = band
            if jb == 0:
                mask = mask & (kj >= jnp.where(i > 0, 0, BLOCK))
            if jb == nb - 1:
                mask = mask & (kj < jnp.where(i < n_blocks_total // nb - 1, 3 * BLOCK, 2 * BLOCK))
            s_l = jnp.where(mask, _dot_t(q4, kbuf[jb * BLOCK:(jb + 3) * BLOCK, ks]), MASKED)
        return sink, s_c, s_l

    def weighted_values(jb, kvh, sink, s_c, s_l):
        rows = slice(jb * BLOCK, (jb + 1) * BLOCK)
        ks = slice(kvh * LANES, (kvh + 1) * LANES)
        m = jnp.maximum(sink, jnp.max(s_c, axis=-1, keepdims=True))
        if latent:
            m = jnp.maximum(m, jnp.max(s_l, axis=-1, keepdims=True))
        p_c = jnp.exp2(s_c - m)
        den = jnp.exp2(sink - m) + jnp.sum(p_c, axis=-1, keepdims=True)
        o = _dot(p_c.astype(BF16), vctx_ref[:, ks])
        if latent:
            p_l = jnp.exp2(s_l - m)
            den = den + jnp.sum(p_l, axis=-1, keepdims=True)
            o = o + _dot(p_l.astype(BF16), vbuf[jb * BLOCK:(jb + 3) * BLOCK, ks])
        o = o * (1.0 / den)
        for hp in range(B_GQA // 2):
            pair_out = jnp.where(out_low, o[2 * hp * BLOCK:(2 * hp + 1) * BLOCK],
                                 o[(2 * hp + 1) * BLOCK:(2 * hp + 2) * BLOCK])
            col = (kvh * (B_GQA // 2) + hp) * LANES
            yb = pair_out * gb_scr[rows, col:col + LANES]
            mix_scr[rows, A_WIDTH + col:A_WIDTH + col + LANES] = yb.astype(BF16)

    for jb in range(nb):
        for kvh in range(B_KV_HEADS):
            st = scores(jb, kvh)
            if jb == 0 and kvh == 0:
                gb_scr[...] = _silu(_dot(h, win_ref[:, AB_GB_OFF:AB_GB_OFF + B_WIDTH]))
                u = _dot(h, win_ref[:, 0:A_WIDTH])
                gate_a = _dot(h, win_ref[:, 2 * A_WIDTH:3 * A_WIDTH])
                ga_scr[...] = _gelu(u) * _silu(gate_a)
            spatial_gate(jb, kvh)
            weighted_values(jb, kvh, *st)

    y = _dot(mix_scr[...], wout_ref[...])
    o_ref[...] = x + gate * y


def _even_mix(x, mod_l, mod_row, g, w_in, w_out, lng, lnb, ws, bs, sink, kctx, vctx,
              cosf=None, sinf=None, w_kv=None, *, tile):
    latent = w_kv is not None
    bsz, L, _ = x.shape
    n = L // tile
    nb = tile // BLOCK
    nblk = L // BLOCK
    row = (lambda b: b) if mod_row is None else (lambda b: mod_row)
    full = lambda shape: pl.BlockSpec(shape, lambda b, i: (0,) * len(shape), pipeline_mode=pl.Buffered(1))
    in_specs = [
        pl.BlockSpec((None, tile, D_MODEL), lambda b, i: (b, i, 0)),
        pl.BlockSpec((None, 1, 3 * D_MODEL), lambda b, i: (row(b), 0, 0)),
        full((1, D_MODEL)),
        full(w_in.shape),
        full(w_out.shape),
        full((1, A_WIDTH)),
        full((1, A_WIDTH)),
        full(ws.shape),
        full(bs.shape),
        pl.BlockSpec(memory_space=pltpu.SMEM),
    ]
    args = [x, mod_l, g, w_in, w_out, lng, lnb, ws, bs, sink]
    scratch = [
        pltpu.VMEM((tile, A_WIDTH), F32),
        pltpu.VMEM((tile, B_WIDTH), F32),
        pltpu.VMEM((tile, A_WIDTH), BF16),
        pltpu.VMEM((nb * B_Q_HEADS * BLOCK, LANES), BF16),
        pltpu.VMEM((tile, A_WIDTH + B_WIDTH), BF16),
    ]
    if latent:
        next_blk = lambda i: jnp.minimum((i + 1) * nb, nblk - 1)
        table = [pl.BlockSpec((tile, LANES), lambda b, i: (i, 0)),
                 pl.BlockSpec((BLOCK, LANES), lambda b, i: (next_blk(i), 0))]
        in_specs += [
            pl.BlockSpec((None, BLOCK, D_MODEL), lambda b, i: (b, next_blk(i), 0)),
            full(w_kv.shape),
        ] + table + table
        args += [x, w_kv, cosf, cosf, sinf, sinf]
        scratch += [pltpu.VMEM((tile + 2 * BLOCK, KV_DUP), BF16)] * 2
    ctx_len = kctx.shape[1]
    in_specs += [pl.BlockSpec((None, ctx_len, KV_DUP), lambda b, i: (b, 0, 0))] * 2
    args += [kctx, vctx]
    return pl.pallas_call(
        functools.partial(_even_kernel, tile=tile, n_blocks_total=nblk, latent=latent),
        grid=(bsz, n),
        in_specs=in_specs,
        out_specs=pl.BlockSpec((None, tile, D_MODEL), lambda b, i: (b, i, 0)),
        out_shape=jax.ShapeDtypeStruct(x.shape, F32),
        scratch_shapes=scratch,
        compiler_params=pltpu.CompilerParams(
            dimension_semantics=("arbitrary", "arbitrary"), vmem_limit_bytes=VMEM_LIMIT),
        name="even_mix" if latent else "even_mix_ctx",
    )(*args)


def _mod_row_slices(m, g, bsz, mod_row):
    out = []
    for b in range(bsz):
        r = b if mod_row is None else mod_row
        out.append((g * (1.0 + m[r:r + 1, D_MODEL:2 * D_MODEL]), m[r:r + 1, 0:D_MODEL]))
    return out


def _half_decay_rate(lam_ref):
    neg = -lam_ref[...]
    return (0.5 * LRU_C) * (jnp.maximum(neg, 0.0) + jnp.log1p(jnp.exp(-jnp.abs(neg))))


def _gates(hz, hzb, hd, wa_ref, wi_ref, ba_ref, bi_ref, hsp, a_scr, bx_ref):
    cols = slice(hd * C_BLOCK, (hd + 1) * C_BLOCK)
    tr = jnp.tanh(_dot(hzb, wa_ref[hd]) + ba_ref[:, cols])
    ti = jnp.tanh(_dot(hzb, wi_ref[hd]) + bi_ref[:, cols])
    hs = hsp[:, cols]
    nla = tr * hs + hs
    a = jnp.exp2(nla * (-LOG2_E))
    y = jnp.tanh(nla) * (a * a + 1.0)
    root = jnp.where(y > 0.0, y * lax.rsqrt(y), 0.0)
    bx = root * (hz * ti + hz)
    for half in range(C_BLOCK // LANES):
        slab = hd * (C_BLOCK // LANES) + half
        ls = slice(half * LANES, (half + 1) * LANES)
        a_scr[slab] = a[:, ls]
        bx_ref[slab] = bx[:, ls]


def _scan_tile(a_scr, bx_scr, out_ref, h_scr, *, tile, reverse, add_ref=None):
    def body(kk, hs):
        t = (tile - 1 - kk) if reverse else kk
        r = pl.multiple_of(t * SUBLANES, SUBLANES)
        new = []
        for c in range(N_SLABS):
            hc = a_scr[c, pl.ds(r, SUBLANES), :] * hs[c] + bx_scr[c, pl.ds(r, SUBLANES), :]
            if add_ref is None:
                out_ref[c, pl.ds(r, SUBLANES), :] = hc
            else:
                out_ref[c, pl.ds(r, SUBLANES), :] = hc + add_ref[c, pl.ds(r, SUBLANES), :]
            new.append(hc)
        return tuple(new)

    hs = lax.fori_loop(0, tile, body, tuple(h_scr[c] for c in range(N_SLABS)), unroll=4)
    for c in range(N_SLABS):
        h_scr[c] = hs[c]


def _scan_rev_kernel(xh_ref, x_ref, m_ref, g_ref, wx_ref, cw_ref, cb_ref, wa_ref, wi_ref, ba_ref, bi_ref,
                     lam_ref, h0_ref, z_ref, s_ref, hfin_ref, ext_scr, a_scr, bx_scr, h_scr,
                     *, bsz, tile, n_tiles, mod_row):
    step = pl.program_id(0)
    tidx = n_tiles - 1 - step
    rows = tile * bsz
    left = CONV_LEFT * bsz
    right = (CONV_W - 1 - CONV_LEFT) * bsz

    @pl.when(step == 0)
    def _():
        h_scr[...] = h0_ref[...]
        ext_scr[:, left + rows:left + rows + right, :] = jnp.zeros((N_SLABS, right, LANES), F32)

    @pl.when(step > 0)
    def _():
        ext_scr[:, left + rows:left + rows + right, :] = ext_scr[:, left:left + right, :]

    mods = _mod_row_slices(m_ref[...], g_ref[...], bsz, mod_row)
    hs = [_norm_mod(x_ref[b], *mods[b]) for b in range(bsz)]
    for j in range(CONV_LEFT):
        r = SUBLANES - CONV_LEFT + j
        hs += [_norm_mod(xh_ref[b, r:r + 1, :], *mods[b]) for b in range(bsz)]
    xr = _dot(jnp.concatenate(hs, axis=0).astype(BF16), wx_ref[...])
    for c in range(N_SLABS):
        ls = slice(c * LANES, (c + 1) * LANES)
        for b in range(bsz):
            ext_scr[c, pl.ds(left + b, tile, stride=bsz), :] = xr[b * tile:(b + 1) * tile, ls]
        ext_scr[c, 0:left, :] = jnp.where(tidx > 0, xr[rows:rows + left, ls], 0.0)

    hsp = _half_decay_rate(lam_ref)
    for hd in range(C_HEADS):
        zs = []
        for half in range(C_BLOCK // LANES):
            c = hd * (C_BLOCK // LANES) + half
            ls = slice(c * LANES, (c + 1) * LANES)
            ext = ext_scr[c]
            z = cb_ref[:, ls]
            for j in range(CONV_W):
                z = z + cw_ref[j:j + 1, ls] * ext[j * bsz:j * bsz + rows]
            zs.append(z)
        hz = jnp.concatenate(zs, axis=1)
        hzb = hz.astype(BF16)
        z_ref[:, hd * C_BLOCK:(hd + 1) * C_BLOCK] = hzb
        _gates(hz, hzb, hd, wa_ref, wi_ref, ba_ref, bi_ref, hsp, a_scr, bx_scr)
    _scan_tile(a_scr, bx_scr, s_ref, h_scr, tile=tile, reverse=True)
    hfin_ref[...] = h_scr[...]


def _scan_fwd_kernel(z_ref, srev_ref, x_ref, m_ref, g_ref, wg_ref, wa_ref, wi_ref, ba_ref, bi_ref, lam_ref,
                     h0_ref, wout_ref, fg_ref, o_ref, hfin_ref, a_scr, bx_scr, h_scr, mix_scr,
                     sg_scr, *, bsz, tile, mod_row, final_norm):
    @pl.when(pl.program_id(0) == 0)
    def _():
        h_scr[...] = h0_ref[...]

    m = m_ref[...]
    mods = _mod_row_slices(m, g_ref[...], bsz, mod_row)
    he = jnp.concatenate([_norm_mod(x_ref[b], *mods[b]) for b in range(bsz)], axis=0).astype(BF16)
    sg_scr[...] = _silu(_dot(he, wg_ref[...])).astype(BF16)

    hsp = _half_decay_rate(lam_ref)
    for hd in range(C_HEADS):
        hzb = z_ref[:, hd * C_BLOCK:(hd + 1) * C_BLOCK]
        _gates(hzb.astype(F32), hzb, hd, wa_ref, wi_ref, ba_ref, bi_ref, hsp, a_scr, bx_scr)
    _scan_tile(a_scr, bx_scr, a_scr, h_scr, tile=tile, reverse=False, add_ref=srev_ref)
    hfin_ref[...] = h_scr[...]

    for b in range(bsz):
        rs = slice(b * tile, (b + 1) * tile)
        for c in range(N_SLABS):
            ls = slice(c * LANES, (c + 1) * LANES)
            hsum = a_scr[c, pl.ds(b, tile, stride=bsz), :]
            mix_scr[rs, ls] = (hsum * sg_scr[rs, ls]).astype(BF16)
    y = _dot(mix_scr[...], wout_ref[...])
    for b in range(bsz):
        r = b if mod_row is None else mod_row
        gate = m[r:r + 1, 2 * D_MODEL:3 * D_MODEL]
        xn = x_ref[b] + gate * y[b * tile:(b + 1) * tile]
        if final_norm:
            ms = jnp.mean(xn * xn, axis=-1, keepdims=True)
            xn = xn * lax.rsqrt(ms + EPS) * fg_ref[...]
        o_ref[b] = xn


def _const_spec(shape):
    return pl.BlockSpec(shape, lambda i: (0,) * len(shape), pipeline_mode=pl.Buffered(1))


def _gate_specs():
    return [
        _const_spec((C_HEADS, C_BLOCK, C_BLOCK)),
        _const_spec((C_HEADS, C_BLOCK, C_BLOCK)),
        _const_spec((1, C_WIDTH)),
        _const_spec((1, C_WIDTH)),
        _const_spec((1, C_WIDTH)),
        _const_spec((N_SLABS, SUBLANES, LANES)),
    ]


def _scan_reverse(x, mod_l, mod_row, g, w_x, conv_w, conv_b, w_a, w_i, b_a, b_i, lam, h0):
    bsz, L, _ = x.shape
    tile = T_SCAN
    n = L // tile
    rev = lambda i: (0, n - 1 - i, 0)
    left = CONV_LEFT * bsz
    right = (CONV_W - 1 - CONV_LEFT) * bsz
    rows = tile * bsz
    return pl.pallas_call(
        functools.partial(_scan_rev_kernel, bsz=bsz, tile=tile, n_tiles=n, mod_row=mod_row),
        grid=(n,),
        in_specs=[
            pl.BlockSpec((bsz, SUBLANES, D_MODEL),
                         lambda i: (0, jnp.maximum((n - 1 - i) * (tile // SUBLANES) - 1, 0), 0)),
            pl.BlockSpec((bsz, tile, D_MODEL), rev),
            _const_spec((MOD_ROWS, 3 * D_MODEL)),
            _const_spec((1, D_MODEL)),
            _const_spec((D_MODEL, C_WIDTH)),
            _const_spec((CONV_W, C_WIDTH)),
            _const_spec((1, C_WIDTH)),
        ] + _gate_specs(),
        out_specs=[
            pl.BlockSpec((rows, C_WIDTH), lambda i: (n - 1 - i, 0)),
            pl.BlockSpec((N_SLABS, rows, LANES), rev),
            pl.BlockSpec((N_SLABS, SUBLANES, LANES), lambda i: (0, 0, 0)),
        ],
        out_shape=[
            jax.ShapeDtypeStruct((L * bsz, C_WIDTH), BF16),
            jax.ShapeDtypeStruct((N_SLABS, L * bsz, LANES), F32),
            jax.ShapeDtypeStruct((N_SLABS, SUBLANES, LANES), F32),
        ],
        scratch_shapes=[
            pltpu.VMEM((N_SLABS, left + rows + right, LANES), F32),
            pltpu.VMEM((N_SLABS, rows, LANES), F32),
            pltpu.VMEM((N_SLABS, rows, LANES), F32),
            pltpu.VMEM((N_SLABS, SUBLANES, LANES), F32),
        ],
        compiler_params=pltpu.CompilerParams(
            dimension_semantics=("arbitrary",), vmem_limit_bytes=VMEM_LIMIT),
        name="odd_scan_rev",
    )(x, x, mod_l, g, w_x, conv_w, conv_b, w_a, w_i, b_a, b_i, lam, h0)


def _scan_forward(z, s_rev, x, mod_l, mod_row, g, w_g, w_a, w_i, b_a, b_i, lam, h0, w_out, final_g):
    bsz, L, _ = x.shape
    tile = T_SCAN
    n = L // tile
    fwd = lambda i: (0, i, 0)
    final_norm = final_g is not None
    fg = final_g if final_norm else jnp.ones((1, D_MODEL), F32)
    return pl.pallas_call(
        functools.partial(_scan_fwd_kernel, bsz=bsz, tile=tile, mod_row=mod_row, final_norm=final_norm),
        grid=(n,),
        in_specs=[
            pl.BlockSpec((tile * bsz, C_WIDTH), lambda i: (i, 0)),
            pl.BlockSpec((N_SLABS, tile * bsz, LANES), fwd),
            pl.BlockSpec((bsz, tile, D_MODEL), fwd),
            _const_spec((MOD_ROWS, 3 * D_MODEL)),
            _const_spec((1, D_MODEL)),
            _const_spec((D_MODEL, C_WIDTH)),
        ] + _gate_specs() + [_const_spec((C_WIDTH, D_MODEL)), _const_spec((1, D_MODEL))],
        out_specs=[
            pl.BlockSpec((bsz, tile, D_MODEL), fwd),
            pl.BlockSpec((N_SLABS, SUBLANES, LANES), lambda i: (0, 0, 0)),
        ],
        out_shape=[
            jax.ShapeDtypeStruct(x.shape, F32),
            jax.ShapeDtypeStruct((N_SLABS, SUBLANES, LANES), F32),
        ],
        scratch_shapes=[
            pltpu.VMEM((N_SLABS, tile * bsz, LANES), F32),
            pltpu.VMEM((N_SLABS, tile * bsz, LANES), F32),
            pltpu.VMEM((N_SLABS, SUBLANES, LANES), F32),
            pltpu.VMEM((bsz * tile, C_WIDTH), BF16),
            pltpu.VMEM((bsz * tile, C_WIDTH), BF16),
        ],
        compiler_params=pltpu.CompilerParams(
            dimension_semantics=("arbitrary",), vmem_limit_bytes=VMEM_LIMIT),
        name="odd_scan_fwd",
    )(z, s_rev, x, mod_l, g, w_g, w_a, w_i, b_a, b_i, lam, h0, w_out, fg)


def _rope_tables(L):
    rows = L // GRID_W
    r, col = jnp.meshgrid(jnp.arange(rows), jnp.arange(GRID_W), indexing="ij")
    r = r.reshape(-1).astype(F32)
    col = col.reshape(-1).astype(F32)
    n_freq = HEAD_DIM // 4
    inv_freq = ROPE_THETA ** (-jnp.arange(n_freq, dtype=F32) / n_freq)
    ang = jnp.concatenate([r[:, None] * inv_freq, col[:, None] * inv_freq], axis=-1)
    cos, sin = jnp.cos(ang), jnp.sin(ang)
    cosf = jnp.tile(jnp.concatenate([cos, cos], axis=-1), (1, LANES // HEAD_DIM))
    sinf = jnp.tile(jnp.concatenate([-sin, sin], axis=-1), (1, LANES // HEAD_DIM))
    return cosf, sinf


def _dup_heads(w):
    return jnp.concatenate([w[:, :HEAD_DIM], w[:, :HEAD_DIM], w[:, HEAD_DIM:], w[:, HEAD_DIM:]], axis=1)


def kernel(x, c, ctx, c_ctx, norm_g, w_mod, b_mod, ab_w_in, a_ln_g, a_ln_b, a_w_s, a_b_s, b_sink,
           ab_w_out, c_w_in, c_conv_w, c_conv_b, c_w_a, c_b_a, c_w_i, c_b_i, c_lam, c_w_out, final_g):
    bsz, L, d = x.shape
    ctx_len = ctx.shape[1]
    assert d == D_MODEL and bsz == SUBLANES and L % T_SCAN == 0 and L % T_EVEN == 0
    assert ctx_len % BLOCK == 0 and ctx_len % T_SCAN == 0 and DEPTH % 2 == 0
    ctx_row = bsz

    cin = jnp.concatenate([c, c_ctx[None, :], jnp.zeros((MOD_ROWS - bsz - 1, d), F32)], axis=0)
    mod = _modulation(cin, w_mod, b_mod).reshape(DEPTH, MOD_ROWS, 1, 3 * d)
    cosf, sinf = _rope_tables(L)
    zero_state = jnp.zeros((N_SLABS, SUBLANES, LANES), F32)

    xc = ctx
    for layer in range(DEPTH):
        need_ctx = layer < DEPTH - 1
        i = layer // 2
        mod_l = mod[layer]
        g = norm_g[layer][None, :]
        if layer % 2 == 0:
            w_in = ab_w_in[i].astype(BF16)
            w_kv = jnp.concatenate([_dup_heads(w_in[:, AB_K_OFF:AB_K_OFF + KV_WIDTH]),
                                    _dup_heads(w_in[:, AB_K_OFF + KV_WIDTH:AB_GB_OFF])], axis=1)
            w_out = ab_w_out[i].astype(BF16)
            lng = a_ln_g[i][None, :]
            lnb = a_ln_b[i][None, :]
            ws = a_w_s[i].astype(BF16)
            bs = jnp.broadcast_to(a_b_s[i][:, :, None], (A_GROUPS, CHUNK, LANES))
            sink = b_sink[i]
            kc, vc = _ctx_kv_project(xc, mod_l, ctx_row, g, w_kv)
            x = _even_mix(x, mod_l, None, g, w_in, w_out, lng, lnb, ws, bs, sink, kc, vc,
                          cosf=cosf, sinf=sinf, w_kv=w_kv, tile=T_EVEN)
            if need_ctx:
                xc = _even_mix(xc, mod_l, ctx_row, g, w_in, w_out, lng, lnb, ws, bs, sink, kc, vc,
                               tile=ctx_len)
        else:
            w_in = c_w_in[i].astype(BF16)
            w_x, w_g = w_in[:, :C_WIDTH], w_in[:, C_WIDTH:]
            w_out = c_w_out[i].astype(BF16)
            conv_w, conv_b = 0.5 * c_conv_w[i], 0.5 * c_conv_b[i][None, :]
            mod_flat = mod_l.reshape(MOD_ROWS, 3 * d)
            gate_args = [(c_w_a[i, dd].astype(BF16), c_w_i[i, dd].astype(BF16),
                          0.5 * c_b_a[i, dd][None, :], 0.5 * c_b_i[i, dd][None, :], c_lam[i, dd][None, :])
                         for dd in range(2)]
            z_c, s_rev_c, h0_rev = _scan_reverse(xc, mod_flat, ctx_row, g, w_x, conv_w, conv_b,
                                                 *gate_args[1], zero_state)
            xc_new, h0_fwd = _scan_forward(z_c, s_rev_c, xc, mod_flat, ctx_row, g, w_g, *gate_args[0],
                                           zero_state, w_out, None)
            z, s_rev, _ = _scan_reverse(x, mod_flat, None, g, w_x, conv_w, conv_b, *gate_args[1], h0_rev)
            fin = final_g[None, :] if layer == DEPTH - 1 else None
            x, _ = _scan_forward(z, s_rev, x, mod_flat, None, g, w_g, *gate_args[0], h0_fwd, w_out, fin)
            if need_ctx:
                xc = xc_new
    return x
```

```python
import functools

import jax
import jax.numpy as jnp
from jax import lax
from jax.experimental import pallas as pl
from jax.experimental.pallas import tpu as pltpu

F32 = jnp.float32
BF16 = jnp.bfloat16

D_MODEL = 1024
DEPTH = 4
GRID_W = 64
EPS = 1e-6
CHUNK = 128
A_GROUPS = 4
A_WIDTH = 512
HEAD_DIM = 64
B_Q_HEADS = 8
B_KV_HEADS = 2
B_GQA = B_Q_HEADS // B_KV_HEADS
B_WIDTH = 512
KV_WIDTH = 128
BLOCK = 128
ROPE_THETA = 10000.0
AB_Q_OFF = 3 * A_WIDTH
AB_K_OFF = 3 * A_WIDTH + B_WIDTH
AB_GB_OFF = AB_K_OFF + 2 * KV_WIDTH
C_WIDTH = 1024
C_HEADS = 4
C_BLOCK = 256
CONV_W = 4
CONV_LEFT = 2
LRU_C = 8.0
LOG2_E = 1.4426950408889634
MASKED = -1e30

LANES = 128
SUBLANES = 8
N_SLABS = C_WIDTH // LANES
MOD_ROWS = 16
VMEM_LIMIT = 56 * 1024 * 1024
KV_DUP = 2 * KV_WIDTH

T_EVEN = 512
T_SCAN = 128


def _sigmoid(x):
    return 0.5 * jnp.tanh(0.5 * x) + 0.5


def _silu(x):
    return x * _sigmoid(x)


def _gelu(x):
    return 0.5 * x * (1.0 + jnp.tanh(0.7978845608028654 * (x + 0.044715 * (x * x * x))))


def _norm_mod(x, g1s, shift):
    ms = jnp.mean(x * x, axis=-1, keepdims=True)
    return x * lax.rsqrt(ms + EPS) * g1s + shift


def _mod_parts(m):
    return m[:, 0:D_MODEL], m[:, D_MODEL:2 * D_MODEL], m[:, 2 * D_MODEL:3 * D_MODEL]


def _rope(x, cosf, sinf):
    n = x.shape[-1]
    lane = lax.broadcasted_iota(jnp.int32, x.shape, 1)
    first = (lane % HEAD_DIM) < (HEAD_DIM // 2)
    sw = jnp.where(first, pltpu.roll(x, n - HEAD_DIM // 2, 1), pltpu.roll(x, HEAD_DIM // 2, 1))
    return x * cosf + sw * sinf


def _dot(a, b):
    return jnp.dot(a, b, preferred_element_type=F32)


def _dot_t(a, b):
    return lax.dot_general(a, b, (((1,), (1,)), ((), ())), preferred_element_type=F32)


def _mod_kernel(c_ref, w_ref, b_ref, o_ref):
    s = _silu(c_ref[...])
    o_ref[...] = _dot(s.astype(BF16), w_ref[...].astype(BF16)) + b_ref[...]


def _modulation(cin, w_mod, b_mod):
    nj = 3
    return pl.pallas_call(
        _mod_kernel,
        grid=(DEPTH, nj),
        in_specs=[
            pl.BlockSpec((MOD_ROWS, D_MODEL), lambda l, j: (0, 0)),
            pl.BlockSpec((None, D_MODEL, D_MODEL), lambda l, j: (l, 0, j)),
            pl.BlockSpec((None, 1, D_MODEL), lambda l, j: (l, 0, j)),
        ],
        out_specs=pl.BlockSpec((None, MOD_ROWS, D_MODEL), lambda l, j: (l, 0, j)),
        out_shape=jax.ShapeDtypeStruct((DEPTH, MOD_ROWS, 3 * D_MODEL), F32),
        compiler_params=pltpu.CompilerParams(
            dimension_semantics=("arbitrary", "arbitrary"), vmem_limit_bytes=VMEM_LIMIT),
        name="modulation",
    )(cin, w_mod, b_mod.reshape(DEPTH, 1, 3 * D_MODEL))


def _ctx_kv_kernel(x_ref, m_ref, g_ref, w_ref, k_ref, v_ref):
    shift, scale, _ = _mod_parts(m_ref[...])
    h = _norm_mod(x_ref[...], g_ref[...] * (1.0 + scale), shift).astype(BF16)
    kv = _dot(h, w_ref[...])
    k_ref[...] = kv[:, 0:KV_DUP].astype(BF16)
    v_ref[...] = kv[:, KV_DUP:2 * KV_DUP].astype(BF16)


def _ctx_kv_project(xc, mod_l, mod_row, g, w_kv):
    bsz, lc, _ = xc.shape
    return pl.pallas_call(
        _ctx_kv_kernel,
        grid=(bsz,),
        in_specs=[
            pl.BlockSpec((None, lc, D_MODEL), lambda b: (b, 0, 0)),
            pl.BlockSpec((None, 1, 3 * D_MODEL), lambda b: (mod_row, 0, 0)),
            pl.BlockSpec((1, D_MODEL), lambda b: (0, 0)),
            pl.BlockSpec((D_MODEL, 2 * KV_DUP), lambda b: (0, 0)),
        ],
        out_specs=[
            pl.BlockSpec((None, lc, KV_DUP), lambda b: (b, 0, 0)),
            pl.BlockSpec((None, lc, KV_DUP), lambda b: (b, 0, 0)),
        ],
        out_shape=[jax.ShapeDtypeStruct((bsz, lc, KV_DUP), BF16)] * 2,
        compiler_params=pltpu.CompilerParams(
            dimension_semantics=("parallel",), vmem_limit_bytes=VMEM_LIMIT),
        name="even_ctx_kv",
    )(xc, mod_l, g, w_kv)


def _even_kernel(*refs, tile, n_blocks_total, latent):
    if latent:
        (x_ref, m_ref, g_ref, win_ref, wout_ref, lng_ref, lnb_ref, ws_ref, bs_ref, sink_ref,
         xn_ref, wkv_ref, cos_ref, cosn_ref, sin_ref, sinn_ref,
         kctx_ref, vctx_ref, o_ref, ga_scr, gb_scr, vln_scr, q_scr, mix_scr, kbuf, vbuf) = refs
    else:
        (x_ref, m_ref, g_ref, win_ref, wout_ref, lng_ref, lnb_ref, ws_ref, bs_ref, sink_ref,
         kctx_ref, vctx_ref, o_ref, ga_scr, gb_scr, vln_scr, q_scr, mix_scr) = refs
    nb = tile // BLOCK
    i = pl.program_id(1)
    rows4 = B_GQA * BLOCK

    x = x_ref[...]
    shift, scale, gate = _mod_parts(m_ref[...])
    g1s = g_ref[...] * (1.0 + scale)
    if latent:
        @pl.when(i == 0)
        def _():
            kbuf[0:BLOCK, :] = jnp.zeros((BLOCK, KV_DUP), BF16)
            vbuf[0:BLOCK, :] = jnp.zeros((BLOCK, KV_DUP), BF16)

        @pl.when(i > 0)
        def _():
            kbuf[0:BLOCK, :] = kbuf[tile:tile + BLOCK, :]
            vbuf[0:BLOCK, :] = vbuf[tile:tile + BLOCK, :]

        he = _norm_mod(jnp.concatenate([x, xn_ref[...]], axis=0), g1s, shift).astype(BF16)
        h = he[0:tile]
        kv = _dot(he, wkv_ref[...])
        cos_e = jnp.concatenate([cos_ref[...], cosn_ref[...]], axis=0)
        sin_e = jnp.concatenate([sin_ref[...], sinn_ref[...]], axis=0)
        for s in range(KV_DUP // LANES):
            sl = slice(s * LANES, (s + 1) * LANES)
            kbuf[BLOCK:, sl] = _rope(kv[:, sl], cos_e, sin_e).astype(BF16)
        vbuf[BLOCK:, :] = kv[:, KV_DUP:2 * KV_DUP].astype(BF16)
    else:
        h = _norm_mod(x, g1s, shift).astype(BF16)

    v = _gelu(_dot(h, win_ref[:, A_WIDTH:2 * A_WIDTH]))
    for g in range(A_GROUPS):
        sl = slice(g * LANES, (g + 1) * LANES)
        vg = v[:, sl]
        mu = jnp.mean(vg, axis=-1, keepdims=True)
        vc = vg - mu
        var = jnp.mean(vc * vc, axis=-1, keepdims=True)
        vln_scr[:, sl] = (vc * lax.rsqrt(var + EPS) * lng_ref[:, sl] + lnb_ref[:, sl]).astype(BF16)

    def gate_inputs():
        gb_scr[...] = _silu(_dot(h, win_ref[:, AB_GB_OFF:AB_GB_OFF + B_WIDTH]))
        u = _dot(h, win_ref[:, 0:A_WIDTH])
        gate_a = _dot(h, win_ref[:, 2 * A_WIDTH:3 * A_WIDTH])
        ga_scr[...] = _gelu(u) * _silu(gate_a)

    q = _dot(h, win_ref[:, AB_Q_OFF:AB_Q_OFF + B_WIDTH])
    low_half = lax.broadcasted_iota(jnp.int32, (tile, LANES), 1) < HEAD_DIM
    for pair in range(B_Q_HEADS // 2):
        qs = q[:, pair * LANES:(pair + 1) * LANES]
        if latent:
            qs = _rope(qs, cos_ref[...], sin_ref[...])
        qs = qs * (HEAD_DIM ** -0.5 * LOG2_E)
        q_lo = jnp.where(low_half, qs, 0.0).astype(BF16)
        q_hi = jnp.where(low_half, 0.0, qs).astype(BF16)
        for jb in range(nb):
            rows = slice(jb * BLOCK, (jb + 1) * BLOCK)
            base = (jb * B_Q_HEADS + 2 * pair) * BLOCK
            q_scr[base:base + BLOCK, :] = q_lo[rows]
            q_scr[base + BLOCK:base + 2 * BLOCK, :] = q_hi[rows]

    if latent:
        qi = lax.broadcasted_iota(jnp.int32, (rows4, 3 * BLOCK), 0) & (BLOCK - 1)
        kj = lax.broadcasted_iota(jnp.int32, (rows4, 3 * BLOCK), 1)
        band = (kj >= qi) & (kj <= qi + 2 * BLOCK)
    out_low = lax.broadcasted_iota(jnp.int32, (BLOCK, LANES), 1) < HEAD_DIM

    def spatial_gate(jb, part):
        rows = slice(jb * BLOCK, (jb + 1) * BLOCK)
        for g in range(part * A_GROUPS // B_KV_HEADS, (part + 1) * A_GROUPS // B_KV_HEADS):
            sl = slice(g * LANES, (g + 1) * LANES)
            sv = _dot(ws_ref[g], vln_scr[rows, sl]) + bs_ref[g]
            mix_scr[rows, sl] = (ga_scr[rows, sl] * sv).astype(BF16)

    def scores(jb, kvh):
        ks = slice(kvh * LANES, (kvh + 1) * LANES)
        base = (jb * B_Q_HEADS + kvh * B_GQA) * BLOCK
        q4 = q_scr[base:base + rows4, :]
        sink = jnp.concatenate(
            [jnp.full((BLOCK, 1), sink_ref[kvh * B_GQA + j] * LOG2_E, F32) for j in range(B_GQA)], axis=0)
        s_c = _dot_t(q4, kctx_ref[:, ks])
        s_l = None
        if latent:
            mask = band
            if jb == 0:
                mask = mask & (kj >= jnp.where(i > 0, 0, BLOCK))
            if jb == nb - 1:
                mask = mask & (kj < jnp.where(i < n_blocks_total // nb - 1, 3 * BLOCK, 2 * BLOCK))
            s_l = jnp.where(mask, _dot_t(q4, kbuf[jb * BLOCK:(jb + 3) * BLOCK, ks]), MASKED)
        return sink, s_c, s_l

    def weighted_values(jb, kvh, sink, s_c, s_l):
        rows = slice(jb * BLOCK, (jb + 1) * BLOCK)
        ks = slice(kvh * LANES, (kvh + 1) * LANES)
        m = jnp.maximum(sink, jnp.max(s_c, axis=-1, keepdims=True))
        if latent:
            m = jnp.maximum(m, jnp.max(s_l, axis=-1, keepdims=True))
        p_c = jnp.exp2(s_c - m)
        den = jnp.exp2(sink - m) + jnp.sum(p_c, axis=-1, keepdims=True)
        o = _dot(p_c.astype(BF16), vctx_ref[:, ks])
        if latent:
            p_l = jnp.exp2(s_l - m)
            den = den + jnp.sum(p_l, axis=-1, keepdims=True)
            o = o + _dot(p_l.astype(BF16), vbuf[jb * BLOCK:(jb + 3) * BLOCK, ks])
        o = o * (1.0 / den)
        for hp in range(B_GQA // 2):
            pair_out = jnp.where(out_low, o[2 * hp * BLOCK:(2 * hp + 1) * BLOCK],
                                 o[(2 * hp + 1) * BLOCK:(2 * hp + 2) * BLOCK])
            col = (kvh * (B_GQA // 2) + hp) * LANES
            yb = pair_out * gb_scr[rows, col:col + LANES]
            mix_scr[rows, A_WIDTH + col:A_WIDTH + col + LANES] = yb.astype(BF16)

    for jb in range(nb):
        for kvh in range(B_KV_HEADS):
            st = scores(jb, kvh)
            if jb == 0 and kvh == 0:
                gate_inputs()
            spatial_gate(jb, kvh)
            weighted_values(jb, kvh, *st)

    y = _dot(mix_scr[...], wout_ref[...])
    o_ref[...] = x + gate * y


def _even_mix(x, mod_l, mod_row, g, w_in, w_out, lng, lnb, ws, bs, sink, kctx, vctx,
              cosf=None, sinf=None, w_kv=None, *, tile):
    latent = w_kv is not None
    bsz, L, _ = x.shape
    n = L // tile
    nb = tile // BLOCK
    nblk = L // BLOCK
    row = (lambda b: b) if mod_row is None else (lambda b: mod_row)
    full = lambda shape: pl.BlockSpec(shape, lambda b, i: (0,) * len(shape), pipeline_mode=pl.Buffered(1))
    in_specs = [
        pl.BlockSpec((None, tile, D_MODEL), lambda b, i: (b, i, 0)),
        pl.BlockSpec((None, 1, 3 * D_MODEL), lambda b, i: (row(b), 0, 0)),
        full((1, D_MODEL)),
        full(w_in.shape),
        full(w_out.shape),
        full((1, A_WIDTH)),
        full((1, A_WIDTH)),
        full(ws.shape),
        full(bs.shape),
        pl.BlockSpec(memory_space=pltpu.SMEM),
    ]
    args = [x, mod_l, g, w_in, w_out, lng, lnb, ws, bs, sink]
    scratch = [
        pltpu.VMEM((tile, A_WIDTH), F32),
        pltpu.VMEM((tile, B_WIDTH), F32),
        pltpu.VMEM((tile, A_WIDTH), BF16),
        pltpu.VMEM((nb * B_Q_HEADS * BLOCK, LANES), BF16),
        pltpu.VMEM((tile, A_WIDTH + B_WIDTH), BF16),
    ]
    if latent:
        next_blk = lambda i: jnp.minimum((i + 1) * nb, nblk - 1)
        table = [pl.BlockSpec((tile, LANES), lambda b, i: (i, 0)),
                 pl.BlockSpec((BLOCK, LANES), lambda b, i: (next_blk(i), 0))]
        in_specs += [
            pl.BlockSpec((None, BLOCK, D_MODEL), lambda b, i: (b, next_blk(i), 0)),
            full(w_kv.shape),
        ] + table + table
        args += [x, w_kv, cosf, cosf, sinf, sinf]
        scratch += [pltpu.VMEM((tile + 2 * BLOCK, KV_DUP), BF16)] * 2
    ctx_len = kctx.shape[1]
    in_specs += [pl.BlockSpec((None, ctx_len, KV_DUP), lambda b, i: (b, 0, 0))] * 2
    args += [kctx, vctx]
    return pl.pallas_call(
        functools.partial(_even_kernel, tile=tile, n_blocks_total=nblk, latent=latent),
        grid=(bsz, n),
        in_specs=in_specs,
        out_specs=pl.BlockSpec((None, tile, D_MODEL), lambda b, i: (b, i, 0)),
        out_shape=jax.ShapeDtypeStruct(x.shape, F32),
        scratch_shapes=scratch,
        compiler_params=pltpu.CompilerParams(
            dimension_semantics=("arbitrary", "arbitrary"), vmem_limit_bytes=VMEM_LIMIT),
        name="even_mix" if latent else "even_mix_ctx",
    )(*args)


def _mod_row_slices(m, g, bsz, mod_row):
    out = []
    for b in range(bsz):
        r = b if mod_row is None else mod_row
        out.append((g * (1.0 + m[r:r + 1, D_MODEL:2 * D_MODEL]), m[r:r + 1, 0:D_MODEL]))
    return out


def _half_decay_rate(lam_ref):
    neg = -lam_ref[...]
    return (0.5 * LRU_C) * (jnp.maximum(neg, 0.0) + jnp.log1p(jnp.exp(-jnp.abs(neg))))


def _gates(hz, hzb, hd, wa_ref, wi_ref, ba_ref, bi_ref, hsp, a_scr, bx_ref):
    cols = slice(hd * C_BLOCK, (hd + 1) * C_BLOCK)
    tr = jnp.tanh(_dot(hzb, wa_ref[hd]) + ba_ref[:, cols])
    ti = jnp.tanh(_dot(hzb, wi_ref[hd]) + bi_ref[:, cols])
    hs = hsp[:, cols]
    nla = tr * hs + hs
    a = jnp.exp2(nla * (-LOG2_E))
    y = jnp.tanh(nla) * (a * a + 1.0)
    root = jnp.where(y > 0.0, y * lax.rsqrt(y), 0.0)
    bx = root * (hz * ti + hz)
    for half in range(C_BLOCK // LANES):
        slab = hd * (C_BLOCK // LANES) + half
        ls = slice(half * LANES, (half + 1) * LANES)
        a_scr[slab] = a[:, ls]
        bx_ref[slab] = bx[:, ls]


def _scan_tile(a_scr, bx_scr, out_ref, h_scr, *, tile, reverse, add_ref=None):
    def body(kk, hs):
        t = (tile - 1 - kk) if reverse else kk
        r = pl.multiple_of(t * SUBLANES, SUBLANES)
        new = []
        for c in range(N_SLABS):
            hc = a_scr[c, pl.ds(r, SUBLANES), :] * hs[c] + bx_scr[c, pl.ds(r, SUBLANES), :]
            if add_ref is None:
                out_ref[c, pl.ds(r, SUBLANES), :] = hc
            else:
                out_ref[c, pl.ds(r, SUBLANES), :] = hc + add_ref[c, pl.ds(r, SUBLANES), :]
            new.append(hc)
        return tuple(new)

    hs = lax.fori_loop(0, tile, body, tuple(h_scr[c] for c in range(N_SLABS)), unroll=4)
    for c in range(N_SLABS):
        h_scr[c] = hs[c]


def _scan_rev_kernel(xh_ref, x_ref, m_ref, g_ref, wx_ref, cw_ref, cb_ref, wa_ref, wi_ref, ba_ref, bi_ref,
                     lam_ref, h0_ref, z_ref, s_ref, he_ref, hfin_ref, ext_scr, a_scr, bx_scr, h_scr,
                     *, bsz, tile, n_tiles, mod_row):
    step = pl.program_id(0)
    tidx = n_tiles - 1 - step
    rows = tile * bsz
    left = CONV_LEFT * bsz
    right = (CONV_W - 1 - CONV_LEFT) * bsz

    @pl.when(step == 0)
    def _():
        h_scr[...] = h0_ref[...]
        ext_scr[:, left + rows:left + rows + right, :] = jnp.zeros((N_SLABS, right, LANES), F32)

    @pl.when(step > 0)
    def _():
        ext_scr[:, left + rows:left + rows + right, :] = ext_scr[:, left:left + right, :]

    mods = _mod_row_slices(m_ref[...], g_ref[...], bsz, mod_row)
    hs = [_norm_mod(x_ref[b], *mods[b]) for b in range(bsz)]
    for j in range(CONV_LEFT):
        r = SUBLANES - CONV_LEFT + j
        hs += [_norm_mod(xh_ref[b, r:r + 1, :], *mods[b]) for b in range(bsz)]
    he = jnp.concatenate(hs, axis=0).astype(BF16)
    he_ref[...] = he[0:rows]
    xr = _dot(he, wx_ref[...])
    for c in range(N_SLABS):
        ls = slice(c * LANES, (c + 1) * LANES)
        for b in range(bsz):
            ext_scr[c, pl.ds(left + b, tile, stride=bsz), :] = xr[b * tile:(b + 1) * tile, ls]
        ext_scr[c, 0:left, :] = jnp.where(tidx > 0, xr[rows:rows + left, ls], 0.0)

    hsp = _half_decay_rate(lam_ref)
    for hd in range(C_HEADS):
        zs = []
        for half in range(C_BLOCK // LANES):
            c = hd * (C_BLOCK // LANES) + half
            ls = slice(c * LANES, (c + 1) * LANES)
            ext = ext_scr[c]
            z = cb_ref[:, ls]
            for j in range(CONV_W):
                z = z + cw_ref[j:j + 1, ls] * ext[j * bsz:j * bsz + rows]
            zs.append(z)
        hz = jnp.concatenate(zs, axis=1)
        hzb = hz.astype(BF16)
        z_ref[:, hd * C_BLOCK:(hd + 1) * C_BLOCK] = hzb
        _gates(hz, hzb, hd, wa_ref, wi_ref, ba_ref, bi_ref, hsp, a_scr, bx_scr)
    _scan_tile(a_scr, bx_scr, s_ref, h_scr, tile=tile, reverse=True)
    hfin_ref[...] = h_scr[...]


def _scan_fwd_kernel(z_ref, srev_ref, he_ref, x_ref, m_ref, wg_ref, wa_ref, wi_ref, ba_ref, bi_ref, lam_ref,
                     h0_ref, wout_ref, fg_ref, o_ref, hfin_ref, a_scr, bx_scr, h_scr, mix_scr,
                     sg_scr, *, bsz, tile, mod_row, final_norm):
    @pl.when(pl.program_id(0) == 0)
    def _():
        h_scr[...] = h0_ref[...]

    m = m_ref[...]
    he = he_ref[...]

    hsp = _half_decay_rate(lam_ref)
    for hd in range(C_HEADS):
        cols = slice(hd * C_BLOCK, (hd + 1) * C_BLOCK)
        sg_scr[:, cols] = _silu(_dot(he, wg_ref[:, cols])).astype(BF16)
        hzb = z_ref[:, cols]
        _gates(hzb.astype(F32), hzb, hd, wa_ref, wi_ref, ba_ref, bi_ref, hsp, a_scr, bx_scr)
    _scan_tile(a_scr, bx_scr, a_scr, h_scr, tile=tile, reverse=False, add_ref=srev_ref)
    hfin_ref[...] = h_scr[...]

    for b in range(bsz):
        rs = slice(b * tile, (b + 1) * tile)
        for c in range(N_SLABS):
            ls = slice(c * LANES, (c + 1) * LANES)
            hsum = a_scr[c, pl.ds(b, tile, stride=bsz), :]
            mix_scr[rs, ls] = (hsum * sg_scr[rs, ls]).astype(BF16)
    y = _dot(mix_scr[...], wout_ref[...])
    for b in range(bsz):
        r = b if mod_row is None else mod_row
        gate = m[r:r + 1, 2 * D_MODEL:3 * D_MODEL]
        xn = x_ref[b] + gate * y[b * tile:(b + 1) * tile]
        if final_norm:
            ms = jnp.mean(xn * xn, axis=-1, keepdims=True)
            xn = xn * lax.rsqrt(ms + EPS) * fg_ref[...]
        o_ref[b] = xn


def _const_spec(shape):
    return pl.BlockSpec(shape, lambda i: (0,) * len(shape), pipeline_mode=pl.Buffered(1))


def _gate_specs():
    return [
        _const_spec((C_HEADS, C_BLOCK, C_BLOCK)),
        _const_spec((C_HEADS, C_BLOCK, C_BLOCK)),
        _const_spec((1, C_WIDTH)),
        _const_spec((1, C_WIDTH)),
        _const_spec((1, C_WIDTH)),
        _const_spec((N_SLABS, SUBLANES, LANES)),
    ]


def _scan_reverse(x, mod_l, mod_row, g, w_x, conv_w, conv_b, w_a, w_i, b_a, b_i, lam, h0):
    bsz, L, _ = x.shape
    tile = T_SCAN
    n = L // tile
    rev = lambda i: (0, n - 1 - i, 0)
    left = CONV_LEFT * bsz
    right = (CONV_W - 1 - CONV_LEFT) * bsz
    rows = tile * bsz
    return pl.pallas_call(
        functools.partial(_scan_rev_kernel, bsz=bsz, tile=tile, n_tiles=n, mod_row=mod_row),
        grid=(n,),
        in_specs=[
            pl.BlockSpec((bsz, SUBLANES, D_MODEL),
                         lambda i: (0, jnp.maximum((n - 1 - i) * (tile // SUBLANES) - 1, 0), 0)),
            pl.BlockSpec((bsz, tile, D_MODEL), rev),
            _const_spec((MOD_ROWS, 3 * D_MODEL)),
            _const_spec((1, D_MODEL)),
            _const_spec((D_MODEL, C_WIDTH)),
            _const_spec((CONV_W, C_WIDTH)),
            _const_spec((1, C_WIDTH)),
        ] + _gate_specs(),
        out_specs=[
            pl.BlockSpec((rows, C_WIDTH), lambda i: (n - 1 - i, 0)),
            pl.BlockSpec((N_SLABS, rows, LANES), rev),
            pl.BlockSpec((rows, D_MODEL), lambda i: (n - 1 - i, 0)),
            pl.BlockSpec((N_SLABS, SUBLANES, LANES), lambda i: (0, 0, 0)),
        ],
        out_shape=[
            jax.ShapeDtypeStruct((L * bsz, C_WIDTH), BF16),
            jax.ShapeDtypeStruct((N_SLABS, L * bsz, LANES), F32),
            jax.ShapeDtypeStruct((L * bsz, D_MODEL), BF16),
            jax.ShapeDtypeStruct((N_SLABS, SUBLANES, LANES), F32),
        ],
        scratch_shapes=[
            pltpu.VMEM((N_SLABS, left + rows + right, LANES), F32),
            pltpu.VMEM((N_SLABS, rows, LANES), F32),
            pltpu.VMEM((N_SLABS, rows, LANES), F32),
            pltpu.VMEM((N_SLABS, SUBLANES, LANES), F32),
        ],
        compiler_params=pltpu.CompilerParams(
            dimension_semantics=("arbitrary",), vmem_limit_bytes=VMEM_LIMIT),
        name="odd_scan_rev",
    )(x, x, mod_l, g, w_x, conv_w, conv_b, w_a, w_i, b_a, b_i, lam, h0)


def _scan_forward(z, s_rev, he, x, mod_l, mod_row, w_g, w_a, w_i, b_a, b_i, lam, h0, w_out, final_g):
    bsz, L, _ = x.shape
    tile = T_SCAN
    n = L // tile
    fwd = lambda i: (0, i, 0)
    final_norm = final_g is not None
    fg = final_g if final_norm else jnp.ones((1, D_MODEL), F32)
    return pl.pallas_call(
        functools.partial(_scan_fwd_kernel, bsz=bsz, tile=tile, mod_row=mod_row, final_norm=final_norm),
        grid=(n,),
        in_specs=[
            pl.BlockSpec((tile * bsz, C_WIDTH), lambda i: (i, 0)),
            pl.BlockSpec((N_SLABS, tile * bsz, LANES), fwd),
            pl.BlockSpec((tile * bsz, D_MODEL), lambda i: (i, 0)),
            pl.BlockSpec((bsz, tile, D_MODEL), fwd),
            _const_spec((MOD_ROWS, 3 * D_MODEL)),
            _const_spec((D_MODEL, C_WIDTH)),
        ] + _gate_specs() + [_const_spec((C_WIDTH, D_MODEL)), _const_spec((1, D_MODEL))],
        out_specs=[
            pl.BlockSpec((bsz, tile, D_MODEL), fwd),
            pl.BlockSpec((N_SLABS, SUBLANES, LANES), lambda i: (0, 0, 0)),
        ],
        out_shape=[
            jax.ShapeDtypeStruct(x.shape, F32),
            jax.ShapeDtypeStruct((N_SLABS, SUBLANES, LANES), F32),
        ],
        scratch_shapes=[
            pltpu.VMEM((N_SLABS, tile * bsz, LANES), F32),
            pltpu.VMEM((N_SLABS, tile * bsz, LANES), F32),
            pltpu.VMEM((N_SLABS, SUBLANES, LANES), F32),
            pltpu.VMEM((bsz * tile, C_WIDTH), BF16),
            pltpu.VMEM((bsz * tile, C_WIDTH), BF16),
        ],
        compiler_params=pltpu.CompilerParams(
            dimension_semantics=("arbitrary",), vmem_limit_bytes=VMEM_LIMIT),
        name="odd_scan_fwd",
    )(z, s_rev, he, x, mod_l, w_g, w_a, w_i, b_a, b_i, lam, h0, w_out, fg)


def _rope_tables(L):
    rows = L // GRID_W
    r, col = jnp.meshgrid(jnp.arange(rows), jnp.arange(GRID_W), indexing="ij")
    r = r.reshape(-1).astype(F32)
    col = col.reshape(-1).astype(F32)
    n_freq = HEAD_DIM // 4
    inv_freq = ROPE_THETA ** (-jnp.arange(n_freq, dtype=F32) / n_freq)
    ang = jnp.concatenate([r[:, None] * inv_freq, col[:, None] * inv_freq], axis=-1)
    cos, sin = jnp.cos(ang), jnp.sin(ang)
    cosf = jnp.tile(jnp.concatenate([cos, cos], axis=-1), (1, LANES // HEAD_DIM))
    sinf = jnp.tile(jnp.concatenate([-sin, sin], axis=-1), (1, LANES // HEAD_DIM))
    return cosf, sinf


def _dup_heads(w):
    return jnp.concatenate([w[:, :HEAD_DIM], w[:, :HEAD_DIM], w[:, HEAD_DIM:], w[:, HEAD_DIM:]], axis=1)


def kernel(x, c, ctx, c_ctx, norm_g, w_mod, b_mod, ab_w_in, a_ln_g, a_ln_b, a_w_s, a_b_s, b_sink,
           ab_w_out, c_w_in, c_conv_w, c_conv_b, c_w_a, c_b_a, c_w_i, c_b_i, c_lam, c_w_out, final_g):
    bsz, L, d = x.shape
    ctx_len = ctx.shape[1]
    assert d == D_MODEL and bsz == SUBLANES and L % T_SCAN == 0 and L % T_EVEN == 0
    assert ctx_len % BLOCK == 0 and ctx_len % T_SCAN == 0 and DEPTH % 2 == 0
    ctx_row = bsz

    cin = jnp.concatenate([c, c_ctx[None, :], jnp.zeros((MOD_ROWS - bsz - 1, d), F32)], axis=0)
    mod = _modulation(cin, w_mod, b_mod).reshape(DEPTH, MOD_ROWS, 1, 3 * d)
    cosf, sinf = _rope_tables(L)
    zero_state = jnp.zeros((N_SLABS, SUBLANES, LANES), F32)

    xc = ctx
    for layer in range(DEPTH):
        need_ctx = layer < DEPTH - 1
        i = layer // 2
        mod_l = mod[layer]
        g = norm_g[layer][None, :]
        if layer % 2 == 0:
            w_in = ab_w_in[i].astype(BF16)
            w_kv = jnp.concatenate([_dup_heads(w_in[:, AB_K_OFF:AB_K_OFF + KV_WIDTH]),
                                    _dup_heads(w_in[:, AB_K_OFF + KV_WIDTH:AB_GB_OFF])], axis=1)
            w_out = ab_w_out[i].astype(BF16)
            lng = a_ln_g[i][None, :]
            lnb = a_ln_b[i][None, :]
            ws = a_w_s[i].astype(BF16)
            bs = jnp.broadcast_to(a_b_s[i][:, :, None], (A_GROUPS, CHUNK, LANES))
            sink = b_sink[i]
            kc, vc = _ctx_kv_project(xc, mod_l, ctx_row, g, w_kv)
            x = _even_mix(x, mod_l, None, g, w_in, w_out, lng, lnb, ws, bs, sink, kc, vc,
                          cosf=cosf, sinf=sinf, w_kv=w_kv, tile=T_EVEN)
            if need_ctx:
                xc = _even_mix(xc, mod_l, ctx_row, g, w_in, w_out, lng, lnb, ws, bs, sink, kc, vc,
                               tile=ctx_len)
        else:
            w_in = c_w_in[i].astype(BF16)
            w_x, w_g = w_in[:, :C_WIDTH], w_in[:, C_WIDTH:]
            w_out = c_w_out[i].astype(BF16)
            conv_w, conv_b = 0.5 * c_conv_w[i], 0.5 * c_conv_b[i][None, :]
            mod_flat = mod_l.reshape(MOD_ROWS, 3 * d)
            gate_args = [(c_w_a[i, dd].astype(BF16), c_w_i[i, dd].astype(BF16),
                          0.5 * c_b_a[i, dd][None, :], 0.5 * c_b_i[i, dd][None, :], c_lam[i, dd][None, :])
                         for dd in range(2)]
            z_c, s_rev_c, he_c, h0_rev = _scan_reverse(xc, mod_flat, ctx_row, g, w_x, conv_w, conv_b,
                                                       *gate_args[1], zero_state)
            xc_new, h0_fwd = _scan_forward(z_c, s_rev_c, he_c, xc, mod_flat, ctx_row, w_g, *gate_args[0],
                                           zero_state, w_out, None)
            z, s_rev, he, _ = _scan_reverse(x, mod_flat, None, g, w_x, conv_w, conv_b, *gate_args[1], h0_rev)
            fin = final_g[None, :] if layer == DEPTH - 1 else None
            x, _ = _scan_forward(z, s_rev, he, x, mod_flat, None, w_g, *gate_args[0], h0_fwd, w_out, fin)
            if need_ctx:
                xc = xc_new
    return x
```

```python
import functools

import jax
import jax.numpy as jnp
from jax import lax
from jax.experimental import pallas as pl
from jax.experimental.pallas import tpu as pltpu

F32 = jnp.float32
BF16 = jnp.bfloat16

D_MODEL = 1024
DEPTH = 4
GRID_W = 64
EPS = 1e-6
CHUNK = 128
A_GROUPS = 4
A_WIDTH = 512
HEAD_DIM = 64
B_Q_HEADS = 8
B_KV_HEADS = 2
B_GQA = B_Q_HEADS // B_KV_HEADS
B_WIDTH = 512
KV_WIDTH = 128
BLOCK = 128
ROPE_THETA = 10000.0
AB_Q_OFF = 3 * A_WIDTH
AB_K_OFF = 3 * A_WIDTH + B_WIDTH
AB_GB_OFF = AB_K_OFF + 2 * KV_WIDTH
C_WIDTH = 1024
C_HEADS = 4
C_BLOCK = 256
CONV_W = 4
CONV_LEFT = 2
LRU_C = 8.0
LOG2_E = 1.4426950408889634
MASKED = -1e30

LANES = 128
SUBLANES = 8
N_SLABS = C_WIDTH // LANES
MOD_ROWS = 16
VMEM_LIMIT = 56 * 1024 * 1024
KV_DUP = 2 * KV_WIDTH

T_EVEN = 512
T_SCAN = 128


def _sigmoid(x):
    return 0.5 * jnp.tanh(0.5 * x) + 0.5


def _silu(x):
    return x * _sigmoid(x)


def _gelu(x):
    return 0.5 * x * (1.0 + jnp.tanh(0.7978845608028654 * (x + 0.044715 * (x * x * x))))


def _norm_mod(x, g1s, shift):
    ms = jnp.mean(x * x, axis=-1, keepdims=True)
    return x * lax.rsqrt(ms + EPS) * g1s + shift


def _mod_parts(m):
    return m[:, 0:D_MODEL], m[:, D_MODEL:2 * D_MODEL], m[:, 2 * D_MODEL:3 * D_MODEL]


def _rope(x, cosf, sinf):
    n = x.shape[-1]
    lane = lax.broadcasted_iota(jnp.int32, x.shape, 1)
    first = (lane % HEAD_DIM) < (HEAD_DIM // 2)
    sw = jnp.where(first, pltpu.roll(x, n - HEAD_DIM // 2, 1), pltpu.roll(x, HEAD_DIM // 2, 1))
    return x * cosf + sw * sinf


def _dot(a, b):
    return jnp.dot(a, b, preferred_element_type=F32)


def _dot_t(a, b):
    return lax.dot_general(a, b, (((1,), (1,)), ((), ())), preferred_element_type=F32)


def _mod_kernel(c_ref, w_ref, b_ref, o_ref):
    s = _silu(c_ref[...])
    o_ref[...] = _dot(s.astype(BF16), w_ref[...].astype(BF16)) + b_ref[...]


def _modulation(cin, w_mod, b_mod):
    nj = 3
    return pl.pallas_call(
        _mod_kernel,
        grid=(DEPTH, nj),
        in_specs=[
            pl.BlockSpec((MOD_ROWS, D_MODEL), lambda l, j: (0, 0)),
            pl.BlockSpec((None, D_MODEL, D_MODEL), lambda l, j: (l, 0, j)),
            pl.BlockSpec((None, 1, D_MODEL), lambda l, j: (l, 0, j)),
        ],
        out_specs=pl.BlockSpec((None, MOD_ROWS, D_MODEL), lambda l, j: (l, 0, j)),
        out_shape=jax.ShapeDtypeStruct((DEPTH, MOD_ROWS, 3 * D_MODEL), F32),
        compiler_params=pltpu.CompilerParams(
            dimension_semantics=("arbitrary", "arbitrary"), vmem_limit_bytes=VMEM_LIMIT),
        name="modulation",
    )(cin, w_mod, b_mod.reshape(DEPTH, 1, 3 * D_MODEL))


def _ctx_kv_kernel(x_ref, m_ref, g_ref, w_ref, k_ref, v_ref):
    shift, scale, _ = _mod_parts(m_ref[...])
    h = _norm_mod(x_ref[...], g_ref[...] * (1.0 + scale), shift).astype(BF16)
    kv = _dot(h, w_ref[...])
    k_ref[...] = kv[:, 0:KV_DUP].astype(BF16)
    v_ref[...] = kv[:, KV_DUP:2 * KV_DUP].astype(BF16)


def _ctx_kv_project(xc, mod_l, mod_row, g, w_kv):
    bsz, lc, _ = xc.shape
    return pl.pallas_call(
        _ctx_kv_kernel,
        grid=(bsz,),
        in_specs=[
            pl.BlockSpec((None, lc, D_MODEL), lambda b: (b, 0, 0)),
            pl.BlockSpec((None, 1, 3 * D_MODEL), lambda b: (mod_row, 0, 0)),
            pl.BlockSpec((1, D_MODEL), lambda b: (0, 0)),
            pl.BlockSpec((D_MODEL, 2 * KV_DUP), lambda b: (0, 0)),
        ],
        out_specs=[
            pl.BlockSpec((None, lc, KV_DUP), lambda b: (b, 0, 0)),
            pl.BlockSpec((None, lc, KV_DUP), lambda b: (b, 0, 0)),
        ],
        out_shape=[jax.ShapeDtypeStruct((bsz, lc, KV_DUP), BF16)] * 2,
        compiler_params=pltpu.CompilerParams(
            dimension_semantics=("parallel",), vmem_limit_bytes=VMEM_LIMIT),
        name="even_ctx_kv",
    )(xc, mod_l, g, w_kv)


def _even_kernel(*refs, tile, n_blocks_total, latent):
    if latent:
        (x_ref, m_ref, g_ref, win_ref, wout_ref, lng_ref, lnb_ref, ws_ref, bs_ref, sink_ref,
         xn_ref, wkv_ref, cos_ref, cosn_ref, sin_ref, sinn_ref,
         kctx_ref, vctx_ref, o_ref, ga_scr, gb_scr, vln_scr, q_scr, mix_scr, kbuf, vbuf) = refs
    else:
        (x_ref, m_ref, g_ref, win_ref, wout_ref, lng_ref, lnb_ref, ws_ref, bs_ref, sink_ref,
         kctx_ref, vctx_ref, o_ref, ga_scr, gb_scr, vln_scr, q_scr, mix_scr) = refs
    nb = tile // BLOCK
    i = pl.program_id(1)
    rows4 = B_GQA * BLOCK

    x = x_ref[...]
    shift, scale, gate = _mod_parts(m_ref[...])
    g1s = g_ref[...] * (1.0 + scale)
    if latent:
        @pl.when(i == 0)
        def _():
            kbuf[0:BLOCK, :] = jnp.zeros((BLOCK, KV_DUP), BF16)
            vbuf[0:BLOCK, :] = jnp.zeros((BLOCK, KV_DUP), BF16)

        @pl.when(i > 0)
        def _():
            kbuf[0:BLOCK, :] = kbuf[tile:tile + BLOCK, :]
            vbuf[0:BLOCK, :] = vbuf[tile:tile + BLOCK, :]

        he = _norm_mod(jnp.concatenate([x, xn_ref[...]], axis=0), g1s, shift).astype(BF16)
        h = he[0:tile]
        kv = _dot(he, wkv_ref[...])
        cos_e = jnp.concatenate([cos_ref[...], cosn_ref[...]], axis=0)
        sin_e = jnp.concatenate([sin_ref[...], sinn_ref[...]], axis=0)
        for s in range(KV_DUP // LANES):
            sl = slice(s * LANES, (s + 1) * LANES)
            kbuf[BLOCK:, sl] = _rope(kv[:, sl], cos_e, sin_e).astype(BF16)
        vbuf[BLOCK:, :] = kv[:, KV_DUP:2 * KV_DUP].astype(BF16)
    else:
        h = _norm_mod(x, g1s, shift).astype(BF16)

    v = _gelu(_dot(h, win_ref[:, A_WIDTH:2 * A_WIDTH]))
    for g in range(A_GROUPS):
        sl = slice(g * LANES, (g + 1) * LANES)
        vg = v[:, sl]
        mu = jnp.mean(vg, axis=-1, keepdims=True)
        vc = vg - mu
        var = jnp.mean(vc * vc, axis=-1, keepdims=True)
        vln_scr[:, sl] = (vc * lax.rsqrt(var + EPS) * lng_ref[:, sl] + lnb_ref[:, sl]).astype(BF16)

    def gate_inputs():
        gb_scr[...] = _silu(_dot(h, win_ref[:, AB_GB_OFF:AB_GB_OFF + B_WIDTH]))
        u = _dot(h, win_ref[:, 0:A_WIDTH])
        gate_a = _dot(h, win_ref[:, 2 * A_WIDTH:3 * A_WIDTH])
        ga_scr[...] = _gelu(u) * _silu(gate_a)

    q = _dot(h, win_ref[:, AB_Q_OFF:AB_Q_OFF + B_WIDTH])
    low_half = lax.broadcasted_iota(jnp.int32, (tile, LANES), 1) < HEAD_DIM
    for pair in range(B_Q_HEADS // 2):
        qs = q[:, pair * LANES:(pair + 1) * LANES]
        if latent:
            qs = _rope(qs, cos_ref[...], sin_ref[...])
        qs = qs * (HEAD_DIM ** -0.5 * LOG2_E)
        q_lo = jnp.where(low_half, qs, 0.0).astype(BF16)
        q_hi = jnp.where(low_half, 0.0, qs).astype(BF16)
        for jb in range(nb):
            rows = slice(jb * BLOCK, (jb + 1) * BLOCK)
            base = (jb * B_Q_HEADS + 2 * pair) * BLOCK
            q_scr[base:base + BLOCK, :] = q_lo[rows]
            q_scr[base + BLOCK:base + 2 * BLOCK, :] = q_hi[rows]

    if latent:
        qi = lax.broadcasted_iota(jnp.int32, (rows4, 3 * BLOCK), 0) & (BLOCK - 1)
        kj = lax.broadcasted_iota(jnp.int32, (rows4, 3 * BLOCK), 1)
        band = (kj >= qi) & (kj <= qi + 2 * BLOCK)
    out_low = lax.broadcasted_iota(jnp.int32, (BLOCK, LANES), 1) < HEAD_DIM

    def spatial_gate(jb, part):
        rows = slice(jb * BLOCK, (jb + 1) * BLOCK)
        for g in range(part * A_GROUPS // B_KV_HEADS, (part + 1) * A_GROUPS // B_KV_HEADS):
            sl = slice(g * LANES, (g + 1) * LANES)
            sv = _dot(ws_ref[g], vln_scr[rows, sl]) + bs_ref[g]
            mix_scr[rows, sl] = (ga_scr[rows, sl] * sv).astype(BF16)

    def scores(jb, kvh):
        ks = slice(kvh * LANES, (kvh + 1) * LANES)
        base = (jb * B_Q_HEADS + kvh * B_GQA) * BLOCK
        q4 = q_scr[base:base + rows4, :]
        sink = jnp.concatenate(
            [jnp.full((BLOCK, 1), sink_ref[kvh * B_GQA + j] * LOG2_E, F32) for j in range(B_GQA)], axis=0)
        s_c = _dot_t(q4, kctx_ref[:, ks])
        s_l = None
        if latent:
            mask = band
            if jb == 0:
                mask = mask & (kj >= jnp.where(i > 0, 0, BLOCK))
            if jb == nb - 1:
                mask = mask & (kj < jnp.where(i < n_blocks_total // nb - 1, 3 * BLOCK, 2 * BLOCK))
            s_l = jnp.where(mask, _dot_t(q4, kbuf[jb * BLOCK:(jb + 3) * BLOCK, ks]), MASKED)
        return sink, s_c, s_l

    def weighted_values(jb, kvh, sink, s_c, s_l):
        rows = slice(jb * BLOCK, (jb + 1) * BLOCK)
        ks = slice(kvh * LANES, (kvh + 1) * LANES)
        m = jnp.maximum(sink, jnp.max(s_c, axis=-1, keepdims=True))
        if latent:
            m = jnp.maximum(m, jnp.max(s_l, axis=-1, keepdims=True))
        p_c = jnp.exp2(s_c - m)
        den = jnp.exp2(sink - m) + jnp.sum(p_c, axis=-1, keepdims=True)
        o = _dot(p_c.astype(BF16), vctx_ref[:, ks])
        if latent:
            p_l = jnp.exp2(s_l - m)
            den = den + jnp.sum(p_l, axis=-1, keepdims=True)
            o = o + _dot(p_l.astype(BF16), vbuf[jb * BLOCK:(jb + 3) * BLOCK, ks])
        o = o * (1.0 / den)
        for hp in range(B_GQA // 2):
            pair_out = jnp.where(out_low, o[2 * hp * BLOCK:(2 * hp + 1) * BLOCK],
                                 o[(2 * hp + 1) * BLOCK:(2 * hp + 2) * BLOCK])
            col = (kvh * (B_GQA // 2) + hp) * LANES
            yb = pair_out * gb_scr[rows, col:col + LANES]
            mix_scr[rows, A_WIDTH + col:A_WIDTH + col + LANES] = yb.astype(BF16)

    for kvh in range(B_KV_HEADS):
        for jb in range(nb):
            st = scores(jb, kvh)
            if jb == 0 and kvh == 0:
                gate_inputs()
            spatial_gate(jb, kvh)
            weighted_values(jb, kvh, *st)

    y = _dot(mix_scr[...], wout_ref[...])
    o_ref[...] = x + gate * y


def _even_mix(x, mod_l, mod_row, g, w_in, w_out, lng, lnb, ws, bs, sink, kctx, vctx,
              cosf=None, sinf=None, w_kv=None, *, tile):
    latent = w_kv is not None
    bsz, L, _ = x.shape
    n = L // tile
    nb = tile // BLOCK
    nblk = L // BLOCK
    row = (lambda b: b) if mod_row is None else (lambda b: mod_row)
    full = lambda shape: pl.BlockSpec(shape, lambda b, i: (0,) * len(shape), pipeline_mode=pl.Buffered(1))
    in_specs = [
        pl.BlockSpec((None, tile, D_MODEL), lambda b, i: (b, i, 0)),
        pl.BlockSpec((None, 1, 3 * D_MODEL), lambda b, i: (row(b), 0, 0)),
        full((1, D_MODEL)),
        full(w_in.shape),
        full(w_out.shape),
        full((1, A_WIDTH)),
        full((1, A_WIDTH)),
        full(ws.shape),
        full(bs.shape),
        pl.BlockSpec(memory_space=pltpu.SMEM),
    ]
    args = [x, mod_l, g, w_in, w_out, lng, lnb, ws, bs, sink]
    scratch = [
        pltpu.VMEM((tile, A_WIDTH), F32),
        pltpu.VMEM((tile, B_WIDTH), F32),
        pltpu.VMEM((tile, A_WIDTH), BF16),
        pltpu.VMEM((nb * B_Q_HEADS * BLOCK, LANES), BF16),
        pltpu.VMEM((tile, A_WIDTH + B_WIDTH), BF16),
    ]
    if latent:
        next_blk = lambda i: jnp.minimum((i + 1) * nb, nblk - 1)
        table = [pl.BlockSpec((tile, LANES), lambda b, i: (i, 0)),
                 pl.BlockSpec((BLOCK, LANES), lambda b, i: (next_blk(i), 0))]
        in_specs += [
            pl.BlockSpec((None, BLOCK, D_MODEL), lambda b, i: (b, next_blk(i), 0)),
            full(w_kv.shape),
        ] + table + table
        args += [x, w_kv, cosf, cosf, sinf, sinf]
        scratch += [pltpu.VMEM((tile + 2 * BLOCK, KV_DUP), BF16)] * 2
    ctx_len = kctx.shape[1]
    in_specs += [pl.BlockSpec((None, ctx_len, KV_DUP), lambda b, i: (b, 0, 0))] * 2
    args += [kctx, vctx]
    return pl.pallas_call(
        functools.partial(_even_kernel, tile=tile, n_blocks_total=nblk, latent=latent),
        grid=(bsz, n),
        in_specs=in_specs,
        out_specs=pl.BlockSpec((None, tile, D_MODEL), lambda b, i: (b, i, 0)),
        out_shape=jax.ShapeDtypeStruct(x.shape, F32),
        scratch_shapes=scratch,
        compiler_params=pltpu.CompilerParams(
            dimension_semantics=("arbitrary", "arbitrary"), vmem_limit_bytes=VMEM_LIMIT),
        name="even_mix" if latent else "even_mix_ctx",
    )(*args)


def _mod_row_slices(m, g, bsz, mod_row):
    out = []
    for b in range(bsz):
        r = b if mod_row is None else mod_row
        out.append((g * (1.0 + m[r:r + 1, D_MODEL:2 * D_MODEL]), m[r:r + 1, 0:D_MODEL]))
    return out


def _half_decay_rate(lam_ref):
    neg = -lam_ref[...]
    return (0.5 * LRU_C) * (jnp.maximum(neg, 0.0) + jnp.log1p(jnp.exp(-jnp.abs(neg))))


def _gates(hz, hzb, hd, wa_ref, wi_ref, ba_ref, bi_ref, hsp, a_scr, bx_ref):
    cols = slice(hd * C_BLOCK, (hd + 1) * C_BLOCK)
    tr = jnp.tanh(_dot(hzb, wa_ref[hd]) + ba_ref[:, cols])
    ti = jnp.tanh(_dot(hzb, wi_ref[hd]) + bi_ref[:, cols])
    hs = hsp[:, cols]
    nla = tr * hs + hs
    a = jnp.exp2(nla * (-LOG2_E))
    y = jnp.tanh(nla) * (a * a + 1.0)
    root = jnp.where(y > 0.0, y * lax.rsqrt(y), 0.0)
    bx = root * (hz * ti + hz)
    for half in range(C_BLOCK // LANES):
        slab = hd * (C_BLOCK // LANES) + half
        ls = slice(half * LANES, (half + 1) * LANES)
        a_scr[slab] = a[:, ls]
        bx_ref[slab] = bx[:, ls]


def _scan_tile(a_scr, bx_scr, out_ref, h_scr, *, tile, reverse, add_ref=None):
    def body(kk, hs):
        t = (tile - 1 - kk) if reverse else kk
        r = pl.multiple_of(t * SUBLANES, SUBLANES)
        new = []
        for c in range(N_SLABS):
            hc = a_scr[c, pl.ds(r, SUBLANES), :] * hs[c] + bx_scr[c, pl.ds(r, SUBLANES), :]
            if add_ref is None:
                out_ref[c, pl.ds(r, SUBLANES), :] = hc
            else:
                out_ref[c, pl.ds(r, SUBLANES), :] = hc + add_ref[c, pl.ds(r, SUBLANES), :]
            new.append(hc)
        return tuple(new)

    hs = lax.fori_loop(0, tile, body, tuple(h_scr[c] for c in range(N_SLABS)), unroll=8)
    for c in range(N_SLABS):
        h_scr[c] = hs[c]


def _scan_rev_kernel(xh_ref, x_ref, m_ref, g_ref, wx_ref, cw_ref, cb_ref, wa_ref, wi_ref, ba_ref, bi_ref,
                     lam_ref, h0_ref, z_ref, s_ref, he_ref, hfin_ref, ext_scr, a_scr, bx_scr, h_scr,
                     *, bsz, tile, n_tiles, mod_row):
    step = pl.program_id(0)
    tidx = n_tiles - 1 - step
    rows = tile * bsz
    left = CONV_LEFT * bsz
    right = (CONV_W - 1 - CONV_LEFT) * bsz

    @pl.when(step == 0)
    def _():
        h_scr[...] = h0_ref[...]
        ext_scr[:, left + rows:left + rows + right, :] = jnp.zeros((N_SLABS, right, LANES), F32)

    @pl.when(step > 0)
    def _():
        ext_scr[:, left + rows:left + rows + right, :] = ext_scr[:, left:left + right, :]

    mods = _mod_row_slices(m_ref[...], g_ref[...], bsz, mod_row)
    hs = [_norm_mod(x_ref[b], *mods[b]) for b in range(bsz)]
    for j in range(CONV_LEFT):
        r = SUBLANES - CONV_LEFT + j
        hs += [_norm_mod(xh_ref[b, r:r + 1, :], *mods[b]) for b in range(bsz)]
    he = jnp.concatenate(hs, axis=0).astype(BF16)
    he_ref[...] = he[0:rows]
    xr = _dot(he, wx_ref[...])
    for c in range(N_SLABS):
        ls = slice(c * LANES, (c + 1) * LANES)
        for b in range(bsz):
            ext_scr[c, pl.ds(left + b, tile, stride=bsz), :] = xr[b * tile:(b + 1) * tile, ls]
        ext_scr[c, 0:left, :] = jnp.where(tidx > 0, xr[rows:rows + left, ls], 0.0)

    hsp = _half_decay_rate(lam_ref)
    for hd in range(C_HEADS):
        zs = []
        for half in range(C_BLOCK // LANES):
            c = hd * (C_BLOCK // LANES) + half
            ls = slice(c * LANES, (c + 1) * LANES)
            ext = ext_scr[c]
            z = cb_ref[:, ls]
            for j in range(CONV_W):
                z = z + cw_ref[j:j + 1, ls] * ext[j * bsz:j * bsz + rows]
            zs.append(z)
        hz = jnp.concatenate(zs, axis=1)
        hzb = hz.astype(BF16)
        z_ref[:, hd * C_BLOCK:(hd + 1) * C_BLOCK] = hzb
        _gates(hz, hzb, hd, wa_ref, wi_ref, ba_ref, bi_ref, hsp, a_scr, bx_scr)
    _scan_tile(a_scr, bx_scr, s_ref, h_scr, tile=tile, reverse=True)
    hfin_ref[...] = h_scr[...]


def _scan_fwd_kernel(z_ref, srev_ref, he_ref, x_ref, m_ref, wg_ref, wa_ref, wi_ref, ba_ref, bi_ref, lam_ref,
                     h0_ref, wout_ref, fg_ref, o_ref, hfin_ref, a_scr, bx_scr, h_scr, mix_scr,
                     sg_scr, *, bsz, tile, mod_row, final_norm):
    @pl.when(pl.program_id(0) == 0)
    def _():
        h_scr[...] = h0_ref[...]

    m = m_ref[...]
    he = he_ref[...]

    hsp = _half_decay_rate(lam_ref)
    for hd in range(C_HEADS):
        cols = slice(hd * C_BLOCK, (hd + 1) * C_BLOCK)
        sg_scr[:, cols] = _silu(_dot(he, wg_ref[:, cols])).astype(BF16)
        hzb = z_ref[:, cols]
        _gates(hzb.astype(F32), hzb, hd, wa_ref, wi_ref, ba_ref, bi_ref, hsp, a_scr, bx_scr)
    _scan_tile(a_scr, bx_scr, a_scr, h_scr, tile=tile, reverse=False, add_ref=srev_ref)
    hfin_ref[...] = h_scr[...]

    for b in range(bsz):
        rs = slice(b * tile, (b + 1) * tile)
        for c in range(N_SLABS):
            ls = slice(c * LANES, (c + 1) * LANES)
            hsum = a_scr[c, pl.ds(b, tile, stride=bsz), :]
            mix_scr[rs, ls] = (hsum * sg_scr[rs, ls]).astype(BF16)
    y = _dot(mix_scr[...], wout_ref[...])
    for b in range(bsz):
        r = b if mod_row is None else mod_row
        gate = m[r:r + 1, 2 * D_MODEL:3 * D_MODEL]
        xn = x_ref[b] + gate * y[b * tile:(b + 1) * tile]
        if final_norm:
            ms = jnp.mean(xn * xn, axis=-1, keepdims=True)
            xn = xn * lax.rsqrt(ms + EPS) * fg_ref[...]
        o_ref[b] = xn


def _const_spec(shape):
    return pl.BlockSpec(shape, lambda i: (0,) * len(shape), pipeline_mode=pl.Buffered(1))


def _gate_specs():
    return [
        _const_spec((C_HEADS, C_BLOCK, C_BLOCK)),
        _const_spec((C_HEADS, C_BLOCK, C_BLOCK)),
        _const_spec((1, C_WIDTH)),
        _const_spec((1, C_WIDTH)),
        _const_spec((1, C_WIDTH)),
        _const_spec((N_SLABS, SUBLANES, LANES)),
    ]


def _scan_reverse(x, mod_l, mod_row, g, w_x, conv_w, conv_b, w_a, w_i, b_a, b_i, lam, h0):
    bsz, L, _ = x.shape
    tile = T_SCAN
    n = L // tile
    rev = lambda i: (0, n - 1 - i, 0)
    left = CONV_LEFT * bsz
    right = (CONV_W - 1 - CONV_LEFT) * bsz
    rows = tile * bsz
    return pl.pallas_call(
        functools.partial(_scan_rev_kernel, bsz=bsz, tile=tile, n_tiles=n, mod_row=mod_row),
        grid=(n,),
        in_specs=[
            pl.BlockSpec((bsz, SUBLANES, D_MODEL),
                         lambda i: (0, jnp.maximum((n - 1 - i) * (tile // SUBLANES) - 1, 0), 0)),
            pl.BlockSpec((bsz, tile, D_MODEL), rev),
            _const_spec((MOD_ROWS, 3 * D_MODEL)),
            _const_spec((1, D_MODEL)),
            _const_spec((D_MODEL, C_WIDTH)),
            _const_spec((CONV_W, C_WIDTH)),
            _const_spec((1, C_WIDTH)),
        ] + _gate_specs(),
        out_specs=[
            pl.BlockSpec((rows, C_WIDTH), lambda i: (n - 1 - i, 0)),
            pl.BlockSpec((N_SLABS, rows, LANES), rev),
            pl.BlockSpec((rows, D_MODEL), lambda i: (n - 1 - i, 0)),
            pl.BlockSpec((N_SLABS, SUBLANES, LANES), lambda i: (0, 0, 0)),
        ],
        out_shape=[
            jax.ShapeDtypeStruct((L * bsz, C_WIDTH), BF16),
            jax.ShapeDtypeStruct((N_SLABS, L * bsz, LANES), F32),
            jax.ShapeDtypeStruct((L * bsz, D_MODEL), BF16),
            jax.ShapeDtypeStruct((N_SLABS, SUBLANES, LANES), F32),
        ],
        scratch_shapes=[
            pltpu.VMEM((N_SLABS, left + rows + right, LANES), F32),
            pltpu.VMEM((N_SLABS, rows, LANES), F32),
            pltpu.VMEM((N_SLABS, rows, LANES), F32),
            pltpu.VMEM((N_SLABS, SUBLANES, LANES), F32),
        ],
        compiler_params=pltpu.CompilerParams(
            dimension_semantics=("arbitrary",), vmem_limit_bytes=VMEM_LIMIT),
        name="odd_scan_rev",
    )(x, x, mod_l, g, w_x, conv_w, conv_b, w_a, w_i, b_a, b_i, lam, h0)


def _scan_forward(z, s_rev, he, x, mod_l, mod_row, w_g, w_a, w_i, b_a, b_i, lam, h0, w_out, final_g):
    bsz, L, _ = x.shape
    tile = T_SCAN
    n = L // tile
    fwd = lambda i: (0, i, 0)
    final_norm = final_g is not None
    fg = final_g if final_norm else jnp.ones((1, D_MODEL), F32)
    return pl.pallas_call(
        functools.partial(_scan_fwd_kernel, bsz=bsz, tile=tile, mod_row=mod_row, final_norm=final_norm),
        grid=(n,),
        in_specs=[
            pl.BlockSpec((tile * bsz, C_WIDTH), lambda i: (i, 0)),
            pl.BlockSpec((N_SLABS, tile * bsz, LANES), fwd),
            pl.BlockSpec((tile * bsz, D_MODEL), lambda i: (i, 0)),
            pl.BlockSpec((bsz, tile, D_MODEL), fwd),
            _const_spec((MOD_ROWS, 3 * D_MODEL)),
            _const_spec((D_MODEL, C_WIDTH)),
        ] + _gate_specs() + [_const_spec((C_WIDTH, D_MODEL)), _const_spec((1, D_MODEL))],
        out_specs=[
            pl.BlockSpec((bsz, tile, D_MODEL), fwd),
            pl.BlockSpec((N_SLABS, SUBLANES, LANES), lambda i: (0, 0, 0)),
        ],
        out_shape=[
            jax.ShapeDtypeStruct(x.shape, F32),
            jax.ShapeDtypeStruct((N_SLABS, SUBLANES, LANES), F32),
        ],
        scratch_shapes=[
            pltpu.VMEM((N_SLABS, tile * bsz, LANES), F32),
            pltpu.VMEM((N_SLABS, tile * bsz, LANES), F32),
            pltpu.VMEM((N_SLABS, SUBLANES, LANES), F32),
            pltpu.VMEM((bsz * tile, C_WIDTH), BF16),
            pltpu.VMEM((bsz * tile, C_WIDTH), BF16),
        ],
        compiler_params=pltpu.CompilerParams(
            dimension_semantics=("arbitrary",), vmem_limit_bytes=VMEM_LIMIT),
        name="odd_scan_fwd",
    )(z, s_rev, he, x, mod_l, w_g, w_a, w_i, b_a, b_i, lam, h0, w_out, fg)


def _rope_tables(L):
    rows = L // GRID_W
    r, col = jnp.meshgrid(jnp.arange(rows), jnp.arange(GRID_W), indexing="ij")
    r = r.reshape(-1).astype(F32)
    col = col.reshape(-1).astype(F32)
    n_freq = HEAD_DIM // 4
    inv_freq = ROPE_THETA ** (-jnp.arange(n_freq, dtype=F32) / n_freq)
    ang = jnp.concatenate([r[:, None] * inv_freq, col[:, None] * inv_freq], axis=-1)
    cos, sin = jnp.cos(ang), jnp.sin(ang)
    cosf = jnp.tile(jnp.concatenate([cos, cos], axis=-1), (1, LANES // HEAD_DIM))
    sinf = jnp.tile(jnp.concatenate([-sin, sin], axis=-1), (1, LANES // HEAD_DIM))
    return cosf, sinf


def _dup_heads(w):
    return jnp.concatenate([w[:, :HEAD_DIM], w[:, :HEAD_DIM], w[:, HEAD_DIM:], w[:, HEAD_DIM:]], axis=1)


def kernel(x, c, ctx, c_ctx, norm_g, w_mod, b_mod, ab_w_in, a_ln_g, a_ln_b, a_w_s, a_b_s, b_sink,
           ab_w_out, c_w_in, c_conv_w, c_conv_b, c_w_a, c_b_a, c_w_i, c_b_i, c_lam, c_w_out, final_g):
    bsz, L, d = x.shape
    ctx_len = ctx.shape[1]
    assert d == D_MODEL and bsz == SUBLANES and L % T_SCAN == 0 and L % T_EVEN == 0
    assert ctx_len % BLOCK == 0 and ctx_len % T_SCAN == 0 and DEPTH % 2 == 0
    ctx_row = bsz

    cin = jnp.concatenate([c, c_ctx[None, :], jnp.zeros((MOD_ROWS - bsz - 1, d), F32)], axis=0)
    mod = _modulation(cin, w_mod, b_mod).reshape(DEPTH, MOD_ROWS, 1, 3 * d)
    cosf, sinf = _rope_tables(L)
    zero_state = jnp.zeros((N_SLABS, SUBLANES, LANES), F32)

    xc = ctx
    for layer in range(DEPTH):
        need_ctx = layer < DEPTH - 1
        i = layer // 2
        mod_l = mod[layer]
        g = norm_g[layer][None, :]
        if layer % 2 == 0:
            w_in = ab_w_in[i].astype(BF16)
            w_kv = jnp.concatenate([_dup_heads(w_in[:, AB_K_OFF:AB_K_OFF + KV_WIDTH]),
                                    _dup_heads(w_in[:, AB_K_OFF + KV_WIDTH:AB_GB_OFF])], axis=1)
            w_out = ab_w_out[i].astype(BF16)
            lng = a_ln_g[i][None, :]
            lnb = a_ln_b[i][None, :]
            ws = a_w_s[i].astype(BF16)
            bs = jnp.broadcast_to(a_b_s[i][:, :, None], (A_GROUPS, CHUNK, LANES))
            sink = b_sink[i]
            kc, vc = _ctx_kv_project(xc, mod_l, ctx_row, g, w_kv)
            x = _even_mix(x, mod_l, None, g, w_in, w_out, lng, lnb, ws, bs, sink, kc, vc,
                          cosf=cosf, sinf=sinf, w_kv=w_kv, tile=T_EVEN)
            if need_ctx:
                xc = _even_mix(xc, mod_l, ctx_row, g, w_in, w_out, lng, lnb, ws, bs, sink, kc, vc,
                               tile=ctx_len)
        else:
            w_in = c_w_in[i].astype(BF16)
            w_x, w_g = w_in[:, :C_WIDTH], w_in[:, C_WIDTH:]
            w_out = c_w_out[i].astype(BF16)
            conv_w, conv_b = 0.5 * c_conv_w[i], 0.5 * c_conv_b[i][None, :]
            mod_flat = mod_l.reshape(MOD_ROWS, 3 * d)
            gate_args = [(c_w_a[i, dd].astype(BF16), c_w_i[i, dd].astype(BF16),
                          0.5 * c_b_a[i, dd][None, :], 0.5 * c_b_i[i, dd][None, :], c_lam[i, dd][None, :])
                         for dd in range(2)]
            z_c, s_rev_c, he_c, h0_rev = _scan_reverse(xc, mod_flat, ctx_row, g, w_x, conv_w, conv_b,
                                                       *gate_args[1], zero_state)
            xc_new, h0_fwd = _scan_forward(z_c, s_rev_c, he_c, xc, mod_flat, ctx_row, w_g, *gate_args[0],
                                           zero_state, w_out, None)
            z, s_rev, he, _ = _scan_reverse(x, mod_flat, None, g, w_x, conv_w, conv_b, *gate_args[1], h0_rev)
            fin = final_g[None, :] if layer == DEPTH - 1 else None
            x, _ = _scan_forward(z, s_rev, he, x, mod_flat, None, w_g, *gate_args[0], h0_fwd, w_out, fin)
            if need_ctx:
                xc = xc_new
    return x
```

```python
import functools

import jax
import jax.numpy as jnp
from jax import lax
from jax.experimental import pallas as pl
from jax.experimental.pallas import tpu as pltpu

F32 = jnp.float32
BF16 = jnp.bfloat16

D_MODEL = 1024
DEPTH = 4
GRID_W = 64
EPS = 1e-6
CHUNK = 128
A_GROUPS = 4
A_WIDTH = 512
HEAD_DIM = 64
B_Q_HEADS = 8
B_KV_HEADS = 2
B_GQA = B_Q_HEADS // B_KV_HEADS
B_WIDTH = 512
KV_WIDTH = 128
BLOCK = 128
ROPE_THETA = 10000.0
AB_Q_OFF = 3 * A_WIDTH
AB_K_OFF = 3 * A_WIDTH + B_WIDTH
AB_GB_OFF = AB_K_OFF + 2 * KV_WIDTH
C_WIDTH = 1024
C_HEADS = 4
C_BLOCK = 256
CONV_W = 4
CONV_LEFT = 2
LRU_C = 8.0
LOG2_E = 1.4426950408889634
MASKED = -1e30

LANES = 128
SUBLANES = 8
N_SLABS = C_WIDTH // LANES
MOD_ROWS = 16
VMEM_LIMIT = 56 * 1024 * 1024
KV_DUP = 2 * KV_WIDTH

T_EVEN = 512
T_SCAN = 128


def _sigmoid(x):
    return 0.5 * jnp.tanh(0.5 * x) + 0.5


def _silu(x):
    return x * _sigmoid(x)


def _gelu(x):
    return 0.5 * x * (1.0 + jnp.tanh(0.7978845608028654 * (x + 0.044715 * (x * x * x))))


def _norm_mod(x, g1s, shift):
    ms = jnp.mean(x * x, axis=-1, keepdims=True)
    return x * lax.rsqrt(ms + EPS) * g1s + shift


def _mod_parts(m):
    return m[:, 0:D_MODEL], m[:, D_MODEL:2 * D_MODEL], m[:, 2 * D_MODEL:3 * D_MODEL]


def _rope(x, cosf, sinf):
    n = x.shape[-1]
    lane = lax.broadcasted_iota(jnp.int32, x.shape, 1)
    first = (lane % HEAD_DIM) < (HEAD_DIM // 2)
    sw = jnp.where(first, pltpu.roll(x, n - HEAD_DIM // 2, 1), pltpu.roll(x, HEAD_DIM // 2, 1))
    return x * cosf + sw * sinf


def _dot(a, b):
    return jnp.dot(a, b, preferred_element_type=F32)


def _dot_t(a, b):
    return lax.dot_general(a, b, (((1,), (1,)), ((), ())), preferred_element_type=F32)


def _mod_kernel(c_ref, w_ref, b_ref, o_ref):
    s = _silu(c_ref[...])
    o_ref[...] = _dot(s.astype(BF16), w_ref[...].astype(BF16)) + b_ref[...]


def _modulation(cin, w_mod, b_mod):
    nj = 3
    return pl.pallas_call(
        _mod_kernel,
        grid=(DEPTH, nj),
        in_specs=[
            pl.BlockSpec((MOD_ROWS, D_MODEL), lambda l, j: (0, 0)),
            pl.BlockSpec((None, D_MODEL, D_MODEL), lambda l, j: (l, 0, j)),
            pl.BlockSpec((None, 1, D_MODEL), lambda l, j: (l, 0, j)),
        ],
        out_specs=pl.BlockSpec((None, MOD_ROWS, D_MODEL), lambda l, j: (l, 0, j)),
        out_shape=jax.ShapeDtypeStruct((DEPTH, MOD_ROWS, 3 * D_MODEL), F32),
        compiler_params=pltpu.CompilerParams(
            dimension_semantics=("arbitrary", "arbitrary"), vmem_limit_bytes=VMEM_LIMIT),
        name="modulation",
    )(cin, w_mod, b_mod.reshape(DEPTH, 1, 3 * D_MODEL))


def _ctx_kv_kernel(x_ref, m_ref, g_ref, w_ref, k_ref, v_ref):
    shift, scale, _ = _mod_parts(m_ref[...])
    h = _norm_mod(x_ref[...], g_ref[...] * (1.0 + scale), shift).astype(BF16)
    kv = _dot(h, w_ref[...])
    k_ref[...] = kv[:, 0:KV_DUP].astype(BF16)
    v_ref[...] = kv[:, KV_DUP:2 * KV_DUP].astype(BF16)


def _ctx_kv_project(xc, mod_l, mod_row, g, w_kv):
    bsz, lc, _ = xc.shape
    return pl.pallas_call(
        _ctx_kv_kernel,
        grid=(bsz,),
        in_specs=[
            pl.BlockSpec((None, lc, D_MODEL), lambda b: (b, 0, 0)),
            pl.BlockSpec((None, 1, 3 * D_MODEL), lambda b: (mod_row, 0, 0)),
            pl.BlockSpec((1, D_MODEL), lambda b: (0, 0)),
            pl.BlockSpec((D_MODEL, 2 * KV_DUP), lambda b: (0, 0)),
        ],
        out_specs=[
            pl.BlockSpec((None, lc, KV_DUP), lambda b: (b, 0, 0)),
            pl.BlockSpec((None, lc, KV_DUP), lambda b: (b, 0, 0)),
        ],
        out_shape=[jax.ShapeDtypeStruct((bsz, lc, KV_DUP), BF16)] * 2,
        compiler_params=pltpu.CompilerParams(
            dimension_semantics=("parallel",), vmem_limit_bytes=VMEM_LIMIT),
        name="even_ctx_kv",
    )(xc, mod_l, g, w_kv)


def _even_kernel(*refs, tile, n_blocks_total, latent):
    if latent:
        (x_ref, m_ref, g_ref, win_ref, wout_ref, lng_ref, lnb_ref, ws_ref, bs_ref, sink_ref,
         xn_ref, wkv_ref, cos_ref, cosn_ref, sin_ref, sinn_ref,
         kctx_ref, vctx_ref, o_ref, ga_scr, gb_scr, vln_scr, q_scr, mix_scr, kbuf, vbuf) = refs
    else:
        (x_ref, m_ref, g_ref, win_ref, wout_ref, lng_ref, lnb_ref, ws_ref, bs_ref, sink_ref,
         kctx_ref, vctx_ref, o_ref, ga_scr, gb_scr, vln_scr, q_scr, mix_scr) = refs
    nb = tile // BLOCK
    i = pl.program_id(1)
    rows4 = B_GQA * BLOCK

    x = x_ref[...]
    shift, scale, gate = _mod_parts(m_ref[...])
    g1s = g_ref[...] * (1.0 + scale)
    if latent:
        @pl.when(i == 0)
        def _():
            kbuf[0:BLOCK, :] = jnp.zeros((BLOCK, KV_DUP), BF16)
            vbuf[0:BLOCK, :] = jnp.zeros((BLOCK, KV_DUP), BF16)

        @pl.when(i > 0)
        def _():
            kbuf[0:BLOCK, :] = kbuf[tile:tile + BLOCK, :]
            vbuf[0:BLOCK, :] = vbuf[tile:tile + BLOCK, :]

        he = _norm_mod(jnp.concatenate([x, xn_ref[...]], axis=0), g1s, shift).astype(BF16)
        h = he[0:tile]
        kv = _dot(he, wkv_ref[...])
        cos_e = jnp.concatenate([cos_ref[...], cosn_ref[...]], axis=0)
        sin_e = jnp.concatenate([sin_ref[...], sinn_ref[...]], axis=0)
        for s in range(KV_DUP // LANES):
            sl = slice(s * LANES, (s + 1) * LANES)
            kbuf[BLOCK:, sl] = _rope(kv[:, sl], cos_e, sin_e).astype(BF16)
        vbuf[BLOCK:, :] = kv[:, KV_DUP:2 * KV_DUP].astype(BF16)
    else:
        h = _norm_mod(x, g1s, shift).astype(BF16)

    v = _gelu(_dot(h, win_ref[:, A_WIDTH:2 * A_WIDTH]))
    for g in range(A_GROUPS):
        sl = slice(g * LANES, (g + 1) * LANES)
        vg = v[:, sl]
        mu = jnp.mean(vg, axis=-1, keepdims=True)
        vc = vg - mu
        var = jnp.mean(vc * vc, axis=-1, keepdims=True)
        vln_scr[:, sl] = (vc * lax.rsqrt(var + EPS) * lng_ref[:, sl] + lnb_ref[:, sl]).astype(BF16)

    def gate_inputs():
        gb_scr[...] = _silu(_dot(h, win_ref[:, AB_GB_OFF:AB_GB_OFF + B_WIDTH]))
        u = _dot(h, win_ref[:, 0:A_WIDTH])
        gate_a = _dot(h, win_ref[:, 2 * A_WIDTH:3 * A_WIDTH])
        ga_scr[...] = _gelu(u) * _silu(gate_a)

    q = _dot(h, win_ref[:, AB_Q_OFF:AB_Q_OFF + B_WIDTH])
    low_half = lax.broadcasted_iota(jnp.int32, (tile, LANES), 1) < HEAD_DIM
    for pair in range(B_Q_HEADS // 2):
        qs = q[:, pair * LANES:(pair + 1) * LANES]
        if latent:
            qs = _rope(qs, cos_ref[...], sin_ref[...])
        qs = qs * (HEAD_DIM ** -0.5 * LOG2_E)
        q_lo = jnp.where(low_half, qs, 0.0).astype(BF16)
        q_hi = jnp.where(low_half, 0.0, qs).astype(BF16)
        for jb in range(nb):
            rows = slice(jb * BLOCK, (jb + 1) * BLOCK)
            base = (jb * B_Q_HEADS + 2 * pair) * BLOCK
            q_scr[base:base + BLOCK, :] = q_lo[rows]
            q_scr[base + BLOCK:base + 2 * BLOCK, :] = q_hi[rows]

    if latent:
        qi = lax.broadcasted_iota(jnp.int32, (rows4, 3 * BLOCK), 0) & (BLOCK - 1)
        kj = lax.broadcasted_iota(jnp.int32, (rows4, 3 * BLOCK), 1)
        band = (kj >= qi) & (kj <= qi + 2 * BLOCK)
    out_low = lax.broadcasted_iota(jnp.int32, (BLOCK, LANES), 1) < HEAD_DIM

    def spatial_gate(jb, part):
        rows = slice(jb * BLOCK, (jb + 1) * BLOCK)
        for g in range(part * A_GROUPS // B_KV_HEADS, (part + 1) * A_GROUPS // B_KV_HEADS):
            sl = slice(g * LANES, (g + 1) * LANES)
            sv = _dot(ws_ref[g], vln_scr[rows, sl]) + bs_ref[g]
            mix_scr[rows, sl] = (ga_scr[rows, sl] * sv).astype(BF16)

    def scores(jb, kvh):
        ks = slice(kvh * LANES, (kvh + 1) * LANES)
        base = (jb * B_Q_HEADS + kvh * B_GQA) * BLOCK
        q4 = q_scr[base:base + rows4, :]
        sink = jnp.concatenate(
            [jnp.full((BLOCK, 1), sink_ref[kvh * B_GQA + j] * LOG2_E, F32) for j in range(B_GQA)], axis=0)
        s_c = _dot_t(q4, kctx_ref[:, ks])
        s_l = None
        if latent:
            mask = band
            if jb == 0:
                mask = mask & (kj >= jnp.where(i > 0, 0, BLOCK))
            if jb == nb - 1:
                mask = mask & (kj < jnp.where(i < n_blocks_total // nb - 1, 3 * BLOCK, 2 * BLOCK))
            s_l = jnp.where(mask, _dot_t(q4, kbuf[jb * BLOCK:(jb + 3) * BLOCK, ks]), MASKED)
        return sink, s_c, s_l

    def weighted_values(jb, kvh, sink, s_c, s_l):
        rows = slice(jb * BLOCK, (jb + 1) * BLOCK)
        ks = slice(kvh * LANES, (kvh + 1) * LANES)
        m = jnp.maximum(sink, jnp.max(s_c, axis=-1, keepdims=True))
        if latent:
            m = jnp.maximum(m, jnp.max(s_l, axis=-1, keepdims=True))
        p_c = jnp.exp2(s_c - m)
        den = jnp.exp2(sink - m) + jnp.sum(p_c, axis=-1, keepdims=True)
        o = _dot(p_c.astype(BF16), vctx_ref[:, ks])
        if latent:
            p_l = jnp.exp2(s_l - m)
            den = den + jnp.sum(p_l, axis=-1, keepdims=True)
            o = o + _dot(p_l.astype(BF16), vbuf[jb * BLOCK:(jb + 3) * BLOCK, ks])
        o = o * (1.0 / den)
        for hp in range(B_GQA // 2):
            pair_out = jnp.where(out_low, o[2 * hp * BLOCK:(2 * hp + 1) * BLOCK],
                                 o[(2 * hp + 1) * BLOCK:(2 * hp + 2) * BLOCK])
            col = (kvh * (B_GQA // 2) + hp) * LANES
            yb = pair_out * gb_scr[rows, col:col + LANES]
            mix_scr[rows, A_WIDTH + col:A_WIDTH + col + LANES] = yb.astype(BF16)

    for kvh in range(B_KV_HEADS):
        for jb in range(nb):
            st = scores(jb, kvh)
            if jb == 0 and kvh == 0:
                gate_inputs()
            spatial_gate(jb, kvh)
            weighted_values(jb, kvh, *st)

    y = _dot(mix_scr[...], wout_ref[...])
    o_ref[...] = x + gate * y


def _even_mix(x, mod_l, mod_row, g, w_in, w_out, lng, lnb, ws, bs, sink, kctx, vctx,
              cosf=None, sinf=None, w_kv=None, *, tile):
    latent = w_kv is not None
    bsz, L, _ = x.shape
    n = L // tile
    nb = tile // BLOCK
    nblk = L // BLOCK
    row = (lambda b: b) if mod_row is None else (lambda b: mod_row)
    full = lambda shape: pl.BlockSpec(shape, lambda b, i: (0,) * len(shape), pipeline_mode=pl.Buffered(1))
    in_specs = [
        pl.BlockSpec((None, tile, D_MODEL), lambda b, i: (b, i, 0)),
        pl.BlockSpec((None, 1, 3 * D_MODEL), lambda b, i: (row(b), 0, 0)),
        full((1, D_MODEL)),
        full(w_in.shape),
        full(w_out.shape),
        full((1, A_WIDTH)),
        full((1, A_WIDTH)),
        full(ws.shape),
        full(bs.shape),
        pl.BlockSpec(memory_space=pltpu.SMEM),
    ]
    args = [x, mod_l, g, w_in, w_out, lng, lnb, ws, bs, sink]
    scratch = [
        pltpu.VMEM((tile, A_WIDTH), F32),
        pltpu.VMEM((tile, B_WIDTH), F32),
        pltpu.VMEM((tile, A_WIDTH), BF16),
        pltpu.VMEM((nb * B_Q_HEADS * BLOCK, LANES), BF16),
        pltpu.VMEM((tile, A_WIDTH + B_WIDTH), BF16),
    ]
    if latent:
        next_blk = lambda i: jnp.minimum((i + 1) * nb, nblk - 1)
        table = [pl.BlockSpec((tile, LANES), lambda b, i: (i, 0)),
                 pl.BlockSpec((BLOCK, LANES), lambda b, i: (next_blk(i), 0))]
        in_specs += [
            pl.BlockSpec((None, BLOCK, D_MODEL), lambda b, i: (b, next_blk(i), 0)),
            full(w_kv.shape),
        ] + table + table
        args += [x, w_kv, cosf, cosf, sinf, sinf]
        scratch += [pltpu.VMEM((tile + 2 * BLOCK, KV_DUP), BF16)] * 2
    ctx_len = kctx.shape[1]
    in_specs += [pl.BlockSpec((None, ctx_len, KV_DUP), lambda b, i: (b, 0, 0))] * 2
    args += [kctx, vctx]
    return pl.pallas_call(
        functools.partial(_even_kernel, tile=tile, n_blocks_total=nblk, latent=latent),
        grid=(bsz, n),
        in_specs=in_specs,
        out_specs=pl.BlockSpec((None, tile, D_MODEL), lambda b, i: (b, i, 0)),
        out_shape=jax.ShapeDtypeStruct(x.shape, F32),
        scratch_shapes=scratch,
        compiler_params=pltpu.CompilerParams(
            dimension_semantics=("arbitrary", "arbitrary"), vmem_limit_bytes=VMEM_LIMIT),
        name="even_mix" if latent else "even_mix_ctx",
    )(*args)


def _mod_row_slices(m, g, bsz, mod_row):
    out = []
    for b in range(bsz):
        r = b if mod_row is None else mod_row
        out.append((g * (1.0 + m[r:r + 1, D_MODEL:2 * D_MODEL]), m[r:r + 1, 0:D_MODEL]))
    return out


def _half_decay_rate(lam_ref):
    neg = -lam_ref[...]
    return (0.5 * LRU_C) * (jnp.maximum(neg, 0.0) + jnp.log1p(jnp.exp(-jnp.abs(neg))))


def _gates(hz, hzb, hd, wa_ref, wi_ref, ba_ref, bi_ref, hsp, a_scr, bx_ref):
    cols = slice(hd * C_BLOCK, (hd + 1) * C_BLOCK)
    tr = jnp.tanh(_dot(hzb, wa_ref[hd]) + ba_ref[:, cols])
    ti = jnp.tanh(_dot(hzb, wi_ref[hd]) + bi_ref[:, cols])
    hs = hsp[:, cols]
    nla = tr * hs + hs
    a = jnp.exp2(nla * (-LOG2_E))
    y = jnp.tanh(nla) * (a * a + 1.0)
    root = jnp.where(y > 0.0, y * lax.rsqrt(y), 0.0)
    bx = root * (hz * ti + hz)
    for half in range(C_BLOCK // LANES):
        slab = hd * (C_BLOCK // LANES) + half
        ls = slice(half * LANES, (half + 1) * LANES)
        a_scr[slab] = a[:, ls]
        bx_ref[slab] = bx[:, ls]


def _scan_tile(a_scr, bx_scr, out_ref, h_scr, *, tile, reverse, add_ref=None):
    def body(kk, hs):
        t = (tile - 1 - kk) if reverse else kk
        r = pl.multiple_of(t * SUBLANES, SUBLANES)
        new = []
        for c in range(N_SLABS):
            hc = a_scr[c, pl.ds(r, SUBLANES), :] * hs[c] + bx_scr[c, pl.ds(r, SUBLANES), :]
            if add_ref is None:
                out_ref[c, pl.ds(r, SUBLANES), :] = hc
            else:
                out_ref[c, pl.ds(r, SUBLANES), :] = hc + add_ref[c, pl.ds(r, SUBLANES), :]
            new.append(hc)
        return tuple(new)

    hs = lax.fori_loop(0, tile, body, tuple(h_scr[c] for c in range(N_SLABS)), unroll=8)
    for c in range(N_SLABS):
        h_scr[c] = hs[c]


def _scan_rev_kernel(xh_ref, x_ref, m_ref, g_ref, wx_ref, cw_ref, cb_ref, wa_ref, wi_ref, ba_ref, bi_ref,
                     lam_ref, h0_ref, z_ref, s_ref, he_ref, hfin_ref, ext_scr, a_scr, bx_scr, h_scr,
                     *, bsz, tile, n_tiles, mod_row):
    step = pl.program_id(0)
    tidx = n_tiles - 1 - step
    rows = tile * bsz
    left = CONV_LEFT * bsz
    right = (CONV_W - 1 - CONV_LEFT) * bsz

    @pl.when(step == 0)
    def _():
        h_scr[...] = h0_ref[...]
        ext_scr[:, left + rows:left + rows + right, :] = jnp.zeros((N_SLABS, right, LANES), F32)

    @pl.when(step > 0)
    def _():
        ext_scr[:, left + rows:left + rows + right, :] = ext_scr[:, left:left + right, :]

    mods = _mod_row_slices(m_ref[...], g_ref[...], bsz, mod_row)
    hs = [_norm_mod(x_ref[b], *mods[b]) for b in range(bsz)]
    for j in range(CONV_LEFT):
        r = SUBLANES - CONV_LEFT + j
        hs += [_norm_mod(xh_ref[b, r:r + 1, :], *mods[b]) for b in range(bsz)]
    he = jnp.concatenate(hs, axis=0).astype(BF16)
    he_ref[...] = he[0:rows]
    xr = _dot(he, wx_ref[...])
    for c in range(N_SLABS):
        ls = slice(c * LANES, (c + 1) * LANES)
        for b in range(bsz):
            ext_scr[c, pl.ds(left + b, tile, stride=bsz), :] = xr[b * tile:(b + 1) * tile, ls]
        ext_scr[c, 0:left, :] = jnp.where(tidx > 0, xr[rows:rows + left, ls], 0.0)

    hsp = _half_decay_rate(lam_ref)
    for hd in range(C_HEADS):
        zs = []
        for half in range(C_BLOCK // LANES):
            c = hd * (C_BLOCK // LANES) + half
            ls = slice(c * LANES, (c + 1) * LANES)
            ext = ext_scr[c]
            z = cb_ref[:, ls]
            for j in range(CONV_W):
                z = z + cw_ref[j:j + 1, ls] * ext[j * bsz:j * bsz + rows]
            zs.append(z)
        hz = jnp.concatenate(zs, axis=1)
        hzb = hz.astype(BF16)
        z_ref[:, hd * C_BLOCK:(hd + 1) * C_BLOCK] = hzb
        _gates(hz, hzb, hd, wa_ref, wi_ref, ba_ref, bi_ref, hsp, a_scr, bx_scr)
    _scan_tile(a_scr, bx_scr, s_ref, h_scr, tile=tile, reverse=True)
    hfin_ref[...] = h_scr[...]


def _scan_fwd_kernel(z_ref, srev_ref, he_ref, x_ref, m_ref, wg_ref, wa_ref, wi_ref, ba_ref, bi_ref, lam_ref,
                     h0_ref, wout_ref, fg_ref, o_ref, hfin_ref, a_scr, bx_scr, h_scr, mix_scr,
                     sg_scr, *, bsz, tile, mod_row, final_norm):
    @pl.when(pl.program_id(0) == 0)
    def _():
        h_scr[...] = h0_ref[...]

    m = m_ref[...]
    he = he_ref[...]

    hsp = _half_decay_rate(lam_ref)
    for hd in range(C_HEADS):
        cols = slice(hd * C_BLOCK, (hd + 1) * C_BLOCK)
        sg_scr[:, cols] = _silu(_dot(he, wg_ref[:, cols])).astype(BF16)
        hzb = z_ref[:, cols]
        _gates(hzb.astype(F32), hzb, hd, wa_ref, wi_ref, ba_ref, bi_ref, hsp, a_scr, bx_scr)
    _scan_tile(a_scr, bx_scr, a_scr, h_scr, tile=tile, reverse=False, add_ref=srev_ref)
    hfin_ref[...] = h_scr[...]

    for b in range(bsz):
        rs = slice(b * tile, (b + 1) * tile)
        for c in range(N_SLABS):
            ls = slice(c * LANES, (c + 1) * LANES)
            hsum = a_scr[c, pl.ds(b, tile, stride=bsz), :]
            mix_scr[rs, ls] = (hsum * sg_scr[rs, ls]).astype(BF16)
    y = _dot(mix_scr[...], wout_ref[...])
    for b in range(bsz):
        r = b if mod_row is None else mod_row
        gate = m[r:r + 1, 2 * D_MODEL:3 * D_MODEL]
        xn = x_ref[b] + gate * y[b * tile:(b + 1) * tile]
        if final_norm:
            ms = jnp.mean(xn * xn, axis=-1, keepdims=True)
            xn = xn * lax.rsqrt(ms + EPS) * fg_ref[...]
        o_ref[b] = xn


def _const_spec(shape):
    return pl.BlockSpec(shape, lambda i: (0,) * len(shape), pipeline_mode=pl.Buffered(1))


def _gate_specs():
    return [
        _const_spec((C_HEADS, C_BLOCK, C_BLOCK)),
        _const_spec((C_HEADS, C_BLOCK, C_BLOCK)),
        _const_spec((1, C_WIDTH)),
        _const_spec((1, C_WIDTH)),
        _const_spec((1, C_WIDTH)),
        _const_spec((N_SLABS, SUBLANES, LANES)),
    ]


def _scan_reverse(x, mod_l, mod_row, g, w_in, conv_w, conv_b, w_a, w_i, b_a, b_i, lam, h0):
    bsz, L, _ = x.shape
    tile = T_SCAN
    n = L // tile
    rev = lambda i: (0, n - 1 - i, 0)
    left = CONV_LEFT * bsz
    right = (CONV_W - 1 - CONV_LEFT) * bsz
    rows = tile * bsz
    return pl.pallas_call(
        functools.partial(_scan_rev_kernel, bsz=bsz, tile=tile, n_tiles=n, mod_row=mod_row),
        grid=(n,),
        in_specs=[
            pl.BlockSpec((bsz, SUBLANES, D_MODEL),
                         lambda i: (0, jnp.maximum((n - 1 - i) * (tile // SUBLANES) - 1, 0), 0)),
            pl.BlockSpec((bsz, tile, D_MODEL), rev),
            _const_spec((MOD_ROWS, 3 * D_MODEL)),
            _const_spec((1, D_MODEL)),
            pl.BlockSpec((D_MODEL, C_WIDTH), lambda i: (0, 0), pipeline_mode=pl.Buffered(1)),
            _const_spec((CONV_W, C_WIDTH)),
            _const_spec((1, C_WIDTH)),
        ] + _gate_specs(),
        out_specs=[
            pl.BlockSpec((rows, C_WIDTH), lambda i: (n - 1 - i, 0)),
            pl.BlockSpec((N_SLABS, rows, LANES), rev),
            pl.BlockSpec((rows, D_MODEL), lambda i: (n - 1 - i, 0)),
            pl.BlockSpec((N_SLABS, SUBLANES, LANES), lambda i: (0, 0, 0)),
        ],
        out_shape=[
            jax.ShapeDtypeStruct((L * bsz, C_WIDTH), BF16),
            jax.ShapeDtypeStruct((N_SLABS, L * bsz, LANES), F32),
            jax.ShapeDtypeStruct((L * bsz, D_MODEL), BF16),
            jax.ShapeDtypeStruct((N_SLABS, SUBLANES, LANES), F32),
        ],
        scratch_shapes=[
            pltpu.VMEM((N_SLABS, left + rows + right, LANES), F32),
            pltpu.VMEM((N_SLABS, rows, LANES), F32),
            pltpu.VMEM((N_SLABS, rows, LANES), F32),
            pltpu.VMEM((N_SLABS, SUBLANES, LANES), F32),
        ],
        compiler_params=pltpu.CompilerParams(
            dimension_semantics=("arbitrary",), vmem_limit_bytes=VMEM_LIMIT),
        name="odd_scan_rev",
    )(x, x, mod_l, g, w_in, conv_w, conv_b, w_a, w_i, b_a, b_i, lam, h0)


def _scan_forward(z, s_rev, he, x, mod_l, mod_row, w_in, w_a, w_i, b_a, b_i, lam, h0, w_out, final_g):
    bsz, L, _ = x.shape
    tile = T_SCAN
    n = L // tile
    fwd = lambda i: (0, i, 0)
    final_norm = final_g is not None
    fg = final_g if final_norm else jnp.ones((1, D_MODEL), F32)
    return pl.pallas_call(
        functools.partial(_scan_fwd_kernel, bsz=bsz, tile=tile, mod_row=mod_row, final_norm=final_norm),
        grid=(n,),
        in_specs=[
            pl.BlockSpec((tile * bsz, C_WIDTH), lambda i: (i, 0)),
            pl.BlockSpec((N_SLABS, tile * bsz, LANES), fwd),
            pl.BlockSpec((tile * bsz, D_MODEL), lambda i: (i, 0)),
            pl.BlockSpec((bsz, tile, D_MODEL), fwd),
            _const_spec((MOD_ROWS, 3 * D_MODEL)),
            pl.BlockSpec((D_MODEL, C_WIDTH), lambda i: (0, 1), pipeline_mode=pl.Buffered(1)),
        ] + _gate_specs() + [_const_spec((C_WIDTH, D_MODEL)), _const_spec((1, D_MODEL))],
        out_specs=[
            pl.BlockSpec((bsz, tile, D_MODEL), fwd),
            pl.BlockSpec((N_SLABS, SUBLANES, LANES), lambda i: (0, 0, 0)),
        ],
        out_shape=[
            jax.ShapeDtypeStruct(x.shape, F32),
            jax.ShapeDtypeStruct((N_SLABS, SUBLANES, LANES), F32),
        ],
        scratch_shapes=[
            pltpu.VMEM((N_SLABS, tile * bsz, LANES), F32),
            pltpu.VMEM((N_SLABS, tile * bsz, LANES), F32),
            pltpu.VMEM((N_SLABS, SUBLANES, LANES), F32),
            pltpu.VMEM((bsz * tile, C_WIDTH), BF16),
            pltpu.VMEM((bsz * tile, C_WIDTH), BF16),
        ],
        compiler_params=pltpu.CompilerParams(
            dimension_semantics=("arbitrary",), vmem_limit_bytes=VMEM_LIMIT),
        name="odd_scan_fwd",
    )(z, s_rev, he, x, mod_l, w_in, w_a, w_i, b_a, b_i, lam, h0, w_out, fg)


def _rope_tables(L):
    rows = L // GRID_W
    r, col = jnp.meshgrid(jnp.arange(rows), jnp.arange(GRID_W), indexing="ij")
    r = r.reshape(-1).astype(F32)
    col = col.reshape(-1).astype(F32)
    n_freq = HEAD_DIM // 4
    inv_freq = ROPE_THETA ** (-jnp.arange(n_freq, dtype=F32) / n_freq)
    ang = jnp.concatenate([r[:, None] * inv_freq, col[:, None] * inv_freq], axis=-1)
    cos, sin = jnp.cos(ang), jnp.sin(ang)
    cosf = jnp.tile(jnp.concatenate([cos, cos], axis=-1), (1, LANES // HEAD_DIM))
    sinf = jnp.tile(jnp.concatenate([-sin, sin], axis=-1), (1, LANES // HEAD_DIM))
    return cosf, sinf


def _dup_heads(w):
    return jnp.concatenate([w[:, :HEAD_DIM], w[:, :HEAD_DIM], w[:, HEAD_DIM:], w[:, HEAD_DIM:]], axis=1)


def kernel(x, c, ctx, c_ctx, norm_g, w_mod, b_mod, ab_w_in, a_ln_g, a_ln_b, a_w_s, a_b_s, b_sink,
           ab_w_out, c_w_in, c_conv_w, c_conv_b, c_w_a, c_b_a, c_w_i, c_b_i, c_lam, c_w_out, final_g):
    bsz, L, d = x.shape
    ctx_len = ctx.shape[1]
    assert d == D_MODEL and bsz == SUBLANES and L % T_SCAN == 0 and L % T_EVEN == 0
    assert ctx_len % BLOCK == 0 and ctx_len % T_SCAN == 0 and DEPTH % 2 == 0
    ctx_row = bsz

    cin = jnp.concatenate([c, c_ctx[None, :], jnp.zeros((MOD_ROWS - bsz - 1, d), F32)], axis=0)
    mod = _modulation(cin, w_mod, b_mod).reshape(DEPTH, MOD_ROWS, 1, 3 * d)
    cosf, sinf = _rope_tables(L)
    zero_state = jnp.zeros((N_SLABS, SUBLANES, LANES), F32)

    xc = ctx
    for layer in range(DEPTH):
        need_ctx = layer < DEPTH - 1
        i = layer // 2
        mod_l = mod[layer]
        g = norm_g[layer][None, :]
        if layer % 2 == 0:
            w_in = ab_w_in[i].astype(BF16)
            w_kv = jnp.concatenate([_dup_heads(w_in[:, AB_K_OFF:AB_K_OFF + KV_WIDTH]),
                                    _dup_heads(w_in[:, AB_K_OFF + KV_WIDTH:AB_GB_OFF])], axis=1)
            w_out = ab_w_out[i].astype(BF16)
            lng = a_ln_g[i][None, :]
            lnb = a_ln_b[i][None, :]
            ws = a_w_s[i].astype(BF16)
            bs = jnp.broadcast_to(a_b_s[i][:, :, None], (A_GROUPS, CHUNK, LANES))
            sink = b_sink[i]
            kc, vc = _ctx_kv_project(xc, mod_l, ctx_row, g, w_kv)
            x = _even_mix(x, mod_l, None, g, w_in, w_out, lng, lnb, ws, bs, sink, kc, vc,
                          cosf=cosf, sinf=sinf, w_kv=w_kv, tile=T_EVEN)
            if need_ctx:
                xc = _even_mix(xc, mod_l, ctx_row, g, w_in, w_out, lng, lnb, ws, bs, sink, kc, vc,
                               tile=ctx_len)
        else:
            w_in = c_w_in[i].astype(BF16)
            w_out = c_w_out[i].astype(BF16)
            conv_w, conv_b = 0.5 * c_conv_w[i], 0.5 * c_conv_b[i][None, :]
            mod_flat = mod_l.reshape(MOD_ROWS, 3 * d)
            gate_args = [(c_w_a[i, dd].astype(BF16), c_w_i[i, dd].astype(BF16),
                          0.5 * c_b_a[i, dd][None, :], 0.5 * c_b_i[i, dd][None, :], c_lam[i, dd][None, :])
                         for dd in range(2)]
            z_c, s_rev_c, he_c, h0_rev = _scan_reverse(xc, mod_flat, ctx_row, g, w_in, conv_w, conv_b,
                                                       *gate_args[1], zero_state)
            xc_new, h0_fwd = _scan_forward(z_c, s_rev_c, he_c, xc, mod_flat, ctx_row, w_in, *gate_args[0],
                                           zero_state, w_out, None)
            z, s_rev, he, _ = _scan_reverse(x, mod_flat, None, g, w_in, conv_w, conv_b, *gate_args[1], h0_rev)
            fin = final_g[None, :] if layer == DEPTH - 1 else None
            x, _ = _scan_forward(z, s_rev, he, x, mod_flat, None, w_in, *gate_args[0], h0_fwd, w_out, fin)
            if need_ctx:
                xc = xc_new
    return x
```

```python
import functools

import jax
import jax.numpy as jnp
from jax import lax
from jax.experimental import pallas as pl
from jax.experimental.pallas import tpu as pltpu

F32 = jnp.float32
BF16 = jnp.bfloat16

D_MODEL = 1024
DEPTH = 4
GRID_W = 64
EPS = 1e-6
CHUNK = 128
A_GROUPS = 4
A_WIDTH = 512
HEAD_DIM = 64
B_Q_HEADS = 8
B_KV_HEADS = 2
B_GQA = B_Q_HEADS // B_KV_HEADS
B_WIDTH = 512
KV_WIDTH = 128
BLOCK = 128
ROPE_THETA = 10000.0
AB_Q_OFF = 3 * A_WIDTH
AB_K_OFF = 3 * A_WIDTH + B_WIDTH
AB_GB_OFF = AB_K_OFF + 2 * KV_WIDTH
C_WIDTH = 1024
C_HEADS = 4
C_BLOCK = 256
CONV_W = 4
CONV_LEFT = 2
LRU_C = 8.0
LOG2_E = 1.4426950408889634
MASKED = -1e30

LANES = 128
SUBLANES = 8
N_SLABS = C_WIDTH // LANES
MOD_ROWS = 16
VMEM_LIMIT = 56 * 1024 * 1024
KV_DUP = 2 * KV_WIDTH

T_EVEN = 512
T_SCAN = 128
SCAN_CHUNK = 8


def _sigmoid(x):
    return 0.5 * jnp.tanh(0.5 * x) + 0.5


def _silu(x):
    return x * _sigmoid(x)


def _gelu(x):
    return 0.5 * x * (1.0 + jnp.tanh(0.7978845608028654 * (x + 0.044715 * (x * x * x))))


def _norm_mod(x, g1s, shift):
    ms = jnp.mean(x * x, axis=-1, keepdims=True)
    return x * lax.rsqrt(ms + EPS) * g1s + shift


def _mod_parts(m):
    return m[:, 0:D_MODEL], m[:, D_MODEL:2 * D_MODEL], m[:, 2 * D_MODEL:3 * D_MODEL]


def _rope(x, cosf, sinf):
    n = x.shape[-1]
    lane = lax.broadcasted_iota(jnp.int32, x.shape, 1)
    first = (lane % HEAD_DIM) < (HEAD_DIM // 2)
    sw = jnp.where(first, pltpu.roll(x, n - HEAD_DIM // 2, 1), pltpu.roll(x, HEAD_DIM // 2, 1))
    return x * cosf + sw * sinf


def _dot(a, b):
    return jnp.dot(a, b, preferred_element_type=F32)


def _dot_t(a, b):
    return lax.dot_general(a, b, (((1,), (1,)), ((), ())), preferred_element_type=F32)


def _mod_kernel(c_ref, w_ref, b_ref, o_ref):
    s = _silu(c_ref[...])
    o_ref[...] = _dot(s.astype(BF16), w_ref[...].astype(BF16)) + b_ref[...]


def _modulation(cin, w_mod, b_mod):
    nj = 3
    return pl.pallas_call(
        _mod_kernel,
        grid=(DEPTH, nj),
        in_specs=[
            pl.BlockSpec((MOD_ROWS, D_MODEL), lambda l, j: (0, 0)),
            pl.BlockSpec((None, D_MODEL, D_MODEL), lambda l, j: (l, 0, j)),
            pl.BlockSpec((None, 1, D_MODEL), lambda l, j: (l, 0, j)),
        ],
        out_specs=pl.BlockSpec((None, MOD_ROWS, D_MODEL), lambda l, j: (l, 0, j)),
        out_shape=jax.ShapeDtypeStruct((DEPTH, MOD_ROWS, 3 * D_MODEL), F32),
        compiler_params=pltpu.CompilerParams(
            dimension_semantics=("arbitrary", "arbitrary"), vmem_limit_bytes=VMEM_LIMIT),
        name="modulation",
    )(cin, w_mod, b_mod.reshape(DEPTH, 1, 3 * D_MODEL))


def _ctx_kv_kernel(x_ref, m_ref, g_ref, w_ref, k_ref, v_ref):
    shift, scale, _ = _mod_parts(m_ref[...])
    h = _norm_mod(x_ref[...], g_ref[...] * (1.0 + scale), shift).astype(BF16)
    kv = _dot(h, w_ref[...])
    k_ref[...] = kv[:, 0:KV_DUP].astype(BF16)
    v_ref[...] = kv[:, KV_DUP:2 * KV_DUP].astype(BF16)


def _ctx_kv_project(xc, mod_l, mod_row, g, w_kv):
    bsz, lc, _ = xc.shape
    return pl.pallas_call(
        _ctx_kv_kernel,
        grid=(bsz,),
        in_specs=[
            pl.BlockSpec((None, lc, D_MODEL), lambda b: (b, 0, 0)),
            pl.BlockSpec((None, 1, 3 * D_MODEL), lambda b: (mod_row, 0, 0)),
            pl.BlockSpec((1, D_MODEL), lambda b: (0, 0)),
            pl.BlockSpec((D_MODEL, 2 * KV_DUP), lambda b: (0, 0)),
        ],
        out_specs=[
            pl.BlockSpec((None, lc, KV_DUP), lambda b: (b, 0, 0)),
            pl.BlockSpec((None, lc, KV_DUP), lambda b: (b, 0, 0)),
        ],
        out_shape=[jax.ShapeDtypeStruct((bsz, lc, KV_DUP), BF16)] * 2,
        compiler_params=pltpu.CompilerParams(
            dimension_semantics=("parallel",), vmem_limit_bytes=VMEM_LIMIT),
        name="even_ctx_kv",
    )(xc, mod_l, g, w_kv)


def _even_kernel(*refs, tile, n_blocks_total, latent):
    if latent:
        (x_ref, m_ref, g_ref, win_ref, wout_ref, lng_ref, lnb_ref, ws_ref, bs_ref, sink_ref,
         xn_ref, wkv_ref, cos_ref, cosn_ref, sin_ref, sinn_ref,
         kctx_ref, vctx_ref, o_ref, ga_scr, gb_scr, vln_scr, q_scr, mix_scr, kbuf, vbuf) = refs
    else:
        (x_ref, m_ref, g_ref, win_ref, wout_ref, lng_ref, lnb_ref, ws_ref, bs_ref, sink_ref,
         kctx_ref, vctx_ref, o_ref, ga_scr, gb_scr, vln_scr, q_scr, mix_scr) = refs
    nb = tile // BLOCK
    i = pl.program_id(1)
    rows4 = B_GQA * BLOCK

    x = x_ref[...]
    shift, scale, gate = _mod_parts(m_ref[...])
    g1s = g_ref[...] * (1.0 + scale)
    if latent:
        @pl.when(i == 0)
        def _():
            kbuf[0:BLOCK, :] = jnp.zeros((BLOCK, KV_DUP), BF16)
            vbuf[0:BLOCK, :] = jnp.zeros((BLOCK, KV_DUP), BF16)

        @pl.when(i > 0)
        def _():
            kbuf[0:BLOCK, :] = kbuf[tile:tile + BLOCK, :]
            vbuf[0:BLOCK, :] = vbuf[tile:tile + BLOCK, :]

        he = _norm_mod(jnp.concatenate([x, xn_ref[...]], axis=0), g1s, shift).astype(BF16)
        h = he[0:tile]
        kv = _dot(he, wkv_ref[...])
        cos_e = jnp.concatenate([cos_ref[...], cosn_ref[...]], axis=0)
        sin_e = jnp.concatenate([sin_ref[...], sinn_ref[...]], axis=0)
        for s in range(KV_DUP // LANES):
            sl = slice(s * LANES, (s + 1) * LANES)
            kbuf[BLOCK:, sl] = _rope(kv[:, sl], cos_e, sin_e).astype(BF16)
        vbuf[BLOCK:, :] = kv[:, KV_DUP:2 * KV_DUP].astype(BF16)
    else:
        h = _norm_mod(x, g1s, shift).astype(BF16)

    v = _gelu(_dot(h, win_ref[:, A_WIDTH:2 * A_WIDTH]))
    for g in range(A_GROUPS):
        sl = slice(g * LANES, (g + 1) * LANES)
        vg = v[:, sl]
        mu = jnp.mean(vg, axis=-1, keepdims=True)
        vc = vg - mu
        var = jnp.mean(vc * vc, axis=-1, keepdims=True)
        vln_scr[:, sl] = (vc * lax.rsqrt(var + EPS) * lng_ref[:, sl] + lnb_ref[:, sl]).astype(BF16)

    def gate_inputs():
        gb_scr[...] = _silu(_dot(h, win_ref[:, AB_GB_OFF:AB_GB_OFF + B_WIDTH]))
        u = _dot(h, win_ref[:, 0:A_WIDTH])
        gate_a = _dot(h, win_ref[:, 2 * A_WIDTH:3 * A_WIDTH])
        ga_scr[...] = _gelu(u) * _silu(gate_a)

    q = _dot(h, win_ref[:, AB_Q_OFF:AB_Q_OFF + B_WIDTH])
    low_half = lax.broadcasted_iota(jnp.int32, (tile, LANES), 1) < HEAD_DIM
    for pair in range(B_Q_HEADS // 2):
        qs = q[:, pair * LANES:(pair + 1) * LANES]
        if latent:
            qs = _rope(qs, cos_ref[...], sin_ref[...])
        qs = qs * (HEAD_DIM ** -0.5 * LOG2_E)
        q_lo = jnp.where(low_half, qs, 0.0).astype(BF16)
        q_hi = jnp.where(low_half, 0.0, qs).astype(BF16)
        for jb in range(nb):
            rows = slice(jb * BLOCK, (jb + 1) * BLOCK)
            base = (jb * B_Q_HEADS + 2 * pair) * BLOCK
            q_scr[base:base + BLOCK, :] = q_lo[rows]
            q_scr[base + BLOCK:base + 2 * BLOCK, :] = q_hi[rows]

    if latent:
        qi = lax.broadcasted_iota(jnp.int32, (rows4, 3 * BLOCK), 0) & (BLOCK - 1)
        kj = lax.broadcasted_iota(jnp.int32, (rows4, 3 * BLOCK), 1)
        band = (kj >= qi) & (kj <= qi + 2 * BLOCK)
    out_low = lax.broadcasted_iota(jnp.int32, (BLOCK, LANES), 1) < HEAD_DIM

    def spatial_gate(jb, part):
        rows = slice(jb * BLOCK, (jb + 1) * BLOCK)
        for g in range(part * A_GROUPS // B_KV_HEADS, (part + 1) * A_GROUPS // B_KV_HEADS):
            sl = slice(g * LANES, (g + 1) * LANES)
            sv = _dot(ws_ref[g], vln_scr[rows, sl]) + bs_ref[g]
            mix_scr[rows, sl] = (ga_scr[rows, sl] * sv).astype(BF16)

    def scores(jb, kvh):
        ks = slice(kvh * LANES, (kvh + 1) * LANES)
        base = (jb * B_Q_HEADS + kvh * B_GQA) * BLOCK
        q4 = q_scr[base:base + rows4, :]
        sink = jnp.concatenate(
            [jnp.full((BLOCK, 1), sink_ref[kvh * B_GQA + j] * LOG2_E, F32) for j in range(B_GQA)], axis=0)
        s_c = _dot_t(q4, kctx_ref[:, ks])
        s_l = None
        if latent:
            mask = band
            if jb == 0:
                mask = mask & (kj >= jnp.where(i > 0, 0, BLOCK))
            if jb == nb - 1:
                mask = mask & (kj < jnp.where(i < n_blocks_total // nb - 1, 3 * BLOCK, 2 * BLOCK))
            s_l = jnp.where(mask, _dot_t(q4, kbuf[jb * BLOCK:(jb + 3) * BLOCK, ks]), MASKED)
        return sink, s_c, s_l

    def weighted_values(jb, kvh, sink, s_c, s_l):
        rows = slice(jb * BLOCK, (jb + 1) * BLOCK)
        ks = slice(kvh * LANES, (kvh + 1) * LANES)
        m = jnp.maximum(sink, jnp.max(s_c, axis=-1, keepdims=True))
        if latent:
            m = jnp.maximum(m, jnp.max(s_l, axis=-1, keepdims=True))
        p_c = jnp.exp2(s_c - m)
        den = jnp.exp2(sink - m) + jnp.sum(p_c, axis=-1, keepdims=True)
        o = _dot(p_c.astype(BF16), vctx_ref[:, ks])
        if latent:
            p_l = jnp.exp2(s_l - m)
            den = den + jnp.sum(p_l, axis=-1, keepdims=True)
            o = o + _dot(p_l.astype(BF16), vbuf[jb * BLOCK:(jb + 3) * BLOCK, ks])
        o = o * (1.0 / den)
        for hp in range(B_GQA // 2):
            pair_out = jnp.where(out_low, o[2 * hp * BLOCK:(2 * hp + 1) * BLOCK],
                                 o[(2 * hp + 1) * BLOCK:(2 * hp + 2) * BLOCK])
            col = (kvh * (B_GQA // 2) + hp) * LANES
            yb = pair_out * gb_scr[rows, col:col + LANES]
            mix_scr[rows, A_WIDTH + col:A_WIDTH + col + LANES] = yb.astype(BF16)

    for kvh in range(B_KV_HEADS):
        for jb in range(nb):
            st = scores(jb, kvh)
            if jb == 0 and kvh == 0:
                gate_inputs()
            spatial_gate(jb, kvh)
            weighted_values(jb, kvh, *st)

    y = _dot(mix_scr[...], wout_ref[...])
    o_ref[...] = x + gate * y


def _even_mix(x, mod_l, mod_row, g, w_in, w_out, lng, lnb, ws, bs, sink, kctx, vctx,
              cosf=None, sinf=None, w_kv=None, *, tile):
    latent = w_kv is not None
    bsz, L, _ = x.shape
    n = L // tile
    nb = tile // BLOCK
    nblk = L // BLOCK
    row = (lambda b: b) if mod_row is None else (lambda b: mod_row)
    full = lambda shape: pl.BlockSpec(shape, lambda b, i: (0,) * len(shape), pipeline_mode=pl.Buffered(1))
    in_specs = [
        pl.BlockSpec((None, tile, D_MODEL), lambda b, i: (b, i, 0)),
        pl.BlockSpec((None, 1, 3 * D_MODEL), lambda b, i: (row(b), 0, 0)),
        full((1, D_MODEL)),
        full(w_in.shape),
        full(w_out.shape),
        full((1, A_WIDTH)),
        full((1, A_WIDTH)),
        full(ws.shape),
        full(bs.shape),
        pl.BlockSpec(memory_space=pltpu.SMEM),
    ]
    args = [x, mod_l, g, w_in, w_out, lng, lnb, ws, bs, sink]
    scratch = [
        pltpu.VMEM((tile, A_WIDTH), F32),
        pltpu.VMEM((tile, B_WIDTH), F32),
        pltpu.VMEM((tile, A_WIDTH), BF16),
        pltpu.VMEM((nb * B_Q_HEADS * BLOCK, LANES), BF16),
        pltpu.VMEM((tile, A_WIDTH + B_WIDTH), BF16),
    ]
    if latent:
        next_blk = lambda i: jnp.minimum((i + 1) * nb, nblk - 1)
        table = [pl.BlockSpec((tile, LANES), lambda b, i: (i, 0)),
                 pl.BlockSpec((BLOCK, LANES), lambda b, i: (next_blk(i), 0))]
        in_specs += [
            pl.BlockSpec((None, BLOCK, D_MODEL), lambda b, i: (b, next_blk(i), 0)),
            full(w_kv.shape),
        ] + table + table
        args += [x, w_kv, cosf, cosf, sinf, sinf]
        scratch += [pltpu.VMEM((tile + 2 * BLOCK, KV_DUP), BF16)] * 2
    ctx_len = kctx.shape[1]
    in_specs += [pl.BlockSpec((None, ctx_len, KV_DUP), lambda b, i: (b, 0, 0))] * 2
    args += [kctx, vctx]
    return pl.pallas_call(
        functools.partial(_even_kernel, tile=tile, n_blocks_total=nblk, latent=latent),
        grid=(bsz, n),
        in_specs=in_specs,
        out_specs=pl.BlockSpec((None, tile, D_MODEL), lambda b, i: (b, i, 0)),
        out_shape=jax.ShapeDtypeStruct(x.shape, F32),
        scratch_shapes=scratch,
        compiler_params=pltpu.CompilerParams(
            dimension_semantics=("arbitrary", "arbitrary"), vmem_limit_bytes=VMEM_LIMIT),
        name="even_mix" if latent else "even_mix_ctx",
    )(*args)


def _mod_row_slices(m, g, bsz, mod_row):
    out = []
    for b in range(bsz):
        r = b if mod_row is None else mod_row
        out.append((g * (1.0 + m[r:r + 1, D_MODEL:2 * D_MODEL]), m[r:r + 1, 0:D_MODEL]))
    return out


def _half_decay_rate(lam_ref):
    neg = -lam_ref[...]
    return (0.5 * LRU_C) * (jnp.maximum(neg, 0.0) + jnp.log1p(jnp.exp(-jnp.abs(neg))))


def _gates(hz, hzb, hd, wa_ref, wi_ref, ba_ref, bi_ref, hsp, a_scr, bx_ref):
    cols = slice(hd * C_BLOCK, (hd + 1) * C_BLOCK)
    tr = jnp.tanh(_dot(hzb, wa_ref[hd]) + ba_ref[:, cols])
    ti = jnp.tanh(_dot(hzb, wi_ref[hd]) + bi_ref[:, cols])
    hs = hsp[:, cols]
    nla = tr * hs + hs
    a = jnp.exp2(nla * (-LOG2_E))
    y = jnp.tanh(nla) * (a * a + 1.0)
    root = jnp.where(y > 0.0, y * lax.rsqrt(y), 0.0)
    bx = root * (hz * ti + hz)
    for half in range(C_BLOCK // LANES):
        slab = hd * (C_BLOCK // LANES) + half
        ls = slice(half * LANES, (half + 1) * LANES)
        a_scr[slab] = a[:, ls]
        bx_ref[slab] = bx[:, ls]


def _scan_tile(a_scr, bx_scr, out_ref, h_scr, *, tile, reverse, add_ref=None):
    n_chunks = tile // SCAN_CHUNK
    chunk_rows = SCAN_CHUNK * SUBLANES

    def body(j, hs):
        base = pl.multiple_of(((n_chunks - 1 - j) if reverse else j) * chunk_rows, chunk_rows)
        hs = list(hs)
        for u in (reversed(range(SCAN_CHUNK)) if reverse else range(SCAN_CHUNK)):
            rows = pl.ds(base + u * SUBLANES, SUBLANES)
            for c in range(N_SLABS):
                hc = a_scr[c, rows, :] * hs[c] + bx_scr[c, rows, :]
                if add_ref is None:
                    out_ref[c, rows, :] = hc
                else:
                    out_ref[c, rows, :] = hc + add_ref[c, rows, :]
                hs[c] = hc
        return tuple(hs)

    hs = lax.fori_loop(0, n_chunks, body, tuple(h_scr[c] for c in range(N_SLABS)))
    for c in range(N_SLABS):
        h_scr[c] = hs[c]


def _scan_rev_kernel(xh_ref, x_ref, m_ref, g_ref, wx_ref, cw_ref, cb_ref, wa_ref, wi_ref, ba_ref, bi_ref,
                     lam_ref, h0_ref, z_ref, s_ref, he_ref, hfin_ref, ext_scr, a_scr, bx_scr, h_scr,
                     *, bsz, tile, n_tiles, mod_row):
    step = pl.program_id(0)
    tidx = n_tiles - 1 - step
    rows = tile * bsz
    left = CONV_LEFT * bsz
    right = (CONV_W - 1 - CONV_LEFT) * bsz

    @pl.when(step == 0)
    def _():
        h_scr[...] = h0_ref[...]
        ext_scr[:, left + rows:left + rows + right, :] = jnp.zeros((N_SLABS, right, LANES), F32)

    @pl.when(step > 0)
    def _():
        ext_scr[:, left + rows:left + rows + right, :] = ext_scr[:, left:left + right, :]

    mods = _mod_row_slices(m_ref[...], g_ref[...], bsz, mod_row)
    hs = [_norm_mod(x_ref[b], *mods[b]) for b in range(bsz)]
    for j in range(CONV_LEFT):
        r = SUBLANES - CONV_LEFT + j
        hs += [_norm_mod(xh_ref[b, r:r + 1, :], *mods[b]) for b in range(bsz)]
    he = jnp.concatenate(hs, axis=0).astype(BF16)
    he_ref[...] = he[0:rows]
    xr = _dot(he, wx_ref[...])
    for c in range(N_SLABS):
        ls = slice(c * LANES, (c + 1) * LANES)
        for b in range(bsz):
            ext_scr[c, pl.ds(left + b, tile, stride=bsz), :] = xr[b * tile:(b + 1) * tile, ls]
        ext_scr[c, 0:left, :] = jnp.where(tidx > 0, xr[rows:rows + left, ls], 0.0)

    hsp = _half_decay_rate(lam_ref)
    for hd in range(C_HEADS):
        zs = []
        for half in range(C_BLOCK // LANES):
            c = hd * (C_BLOCK // LANES) + half
            ls = slice(c * LANES, (c + 1) * LANES)
            ext = ext_scr[c]
            z = cb_ref[:, ls]
            for j in range(CONV_W):
                z = z + cw_ref[j:j + 1, ls] * ext[j * bsz:j * bsz + rows]
            zs.append(z)
        hz = jnp.concatenate(zs, axis=1)
        hzb = hz.astype(BF16)
        z_ref[:, hd * C_BLOCK:(hd + 1) * C_BLOCK] = hzb
        _gates(hz, hzb, hd, wa_ref, wi_ref, ba_ref, bi_ref, hsp, a_scr, bx_scr)
    _scan_tile(a_scr, bx_scr, s_ref, h_scr, tile=tile, reverse=True)
    hfin_ref[...] = h_scr[...]


def _scan_fwd_kernel(z_ref, srev_ref, he_ref, x_ref, m_ref, wg_ref, wa_ref, wi_ref, ba_ref, bi_ref, lam_ref,
                     h0_ref, wout_ref, fg_ref, o_ref, hfin_ref, a_scr, bx_scr, h_scr, mix_scr,
                     sg_scr, *, bsz, tile, mod_row, final_norm):
    @pl.when(pl.program_id(0) == 0)
    def _():
        h_scr[...] = h0_ref[...]

    m = m_ref[...]
    he = he_ref[...]

    hsp = _half_decay_rate(lam_ref)
    for hd in range(C_HEADS):
        cols = slice(hd * C_BLOCK, (hd + 1) * C_BLOCK)
        sg_scr[:, cols] = _silu(_dot(he, wg_ref[:, cols])).astype(BF16)
        hzb = z_ref[:, cols]
        _gates(hzb.astype(F32), hzb, hd, wa_ref, wi_ref, ba_ref, bi_ref, hsp, a_scr, bx_scr)
    _scan_tile(a_scr, bx_scr, a_scr, h_scr, tile=tile, reverse=False, add_ref=srev_ref)
    hfin_ref[...] = h_scr[...]

    for b in range(bsz):
        rs = slice(b * tile, (b + 1) * tile)
        for c in range(N_SLABS):
            ls = slice(c * LANES, (c + 1) * LANES)
            hsum = a_scr[c, pl.ds(b, tile, stride=bsz), :]
            mix_scr[rs, ls] = (hsum * sg_scr[rs, ls]).astype(BF16)
    y = _dot(mix_scr[...], wout_ref[...])
    for b in range(bsz):
        r = b if mod_row is None else mod_row
        gate = m[r:r + 1, 2 * D_MODEL:3 * D_MODEL]
        xn = x_ref[b] + gate * y[b * tile:(b + 1) * tile]
        if final_norm:
            ms = jnp.mean(xn * xn, axis=-1, keepdims=True)
            xn = xn * lax.rsqrt(ms + EPS) * fg_ref[...]
        o_ref[b] = xn


def _const_spec(shape):
    return pl.BlockSpec(shape, lambda i: (0,) * len(shape), pipeline_mode=pl.Buffered(1))


def _gate_specs():
    return [
        _const_spec((C_HEADS, C_BLOCK, C_BLOCK)),
        _const_spec((C_HEADS, C_BLOCK, C_BLOCK)),
        _const_spec((1, C_WIDTH)),
        _const_spec((1, C_WIDTH)),
        _const_spec((1, C_WIDTH)),
        _const_spec((N_SLABS, SUBLANES, LANES)),
    ]


def _scan_reverse(x, mod_l, mod_row, g, w_in, conv_w, conv_b, w_a, w_i, b_a, b_i, lam, h0):
    bsz, L, _ = x.shape
    tile = T_SCAN
    n = L // tile
    rev = lambda i: (0, n - 1 - i, 0)
    left = CONV_LEFT * bsz
    right = (CONV_W - 1 - CONV_LEFT) * bsz
    rows = tile * bsz
    return pl.pallas_call(
        functools.partial(_scan_rev_kernel, bsz=bsz, tile=tile, n_tiles=n, mod_row=mod_row),
        grid=(n,),
        in_specs=[
            pl.BlockSpec((bsz, SUBLANES, D_MODEL),
                         lambda i: (0, jnp.maximum((n - 1 - i) * (tile // SUBLANES) - 1, 0), 0)),
            pl.BlockSpec((bsz, tile, D_MODEL), rev),
            _const_spec((MOD_ROWS, 3 * D_MODEL)),
            _const_spec((1, D_MODEL)),
            pl.BlockSpec((D_MODEL, C_WIDTH), lambda i: (0, 0), pipeline_mode=pl.Buffered(1)),
            _const_spec((CONV_W, C_WIDTH)),
            _const_spec((1, C_WIDTH)),
        ] + _gate_specs(),
        out_specs=[
            pl.BlockSpec((rows, C_WIDTH), lambda i: (n - 1 - i, 0)),
            pl.BlockSpec((N_SLABS, rows, LANES), rev),
            pl.BlockSpec((rows, D_MODEL), lambda i: (n - 1 - i, 0)),
            pl.BlockSpec((N_SLABS, SUBLANES, LANES), lambda i: (0, 0, 0)),
        ],
        out_shape=[
            jax.ShapeDtypeStruct((L * bsz, C_WIDTH), BF16),
            jax.ShapeDtypeStruct((N_SLABS, L * bsz, LANES), F32),
            jax.ShapeDtypeStruct((L * bsz, D_MODEL), BF16),
            jax.ShapeDtypeStruct((N_SLABS, SUBLANES, LANES), F32),
        ],
        scratch_shapes=[
            pltpu.VMEM((N_SLABS, left + rows + right, LANES), F32),
            pltpu.VMEM((N_SLABS, rows, LANES), F32),
            pltpu.VMEM((N_SLABS, rows, LANES), F32),
            pltpu.VMEM((N_SLABS, SUBLANES, LANES), F32),
        ],
        compiler_params=pltpu.CompilerParams(
            dimension_semantics=("arbitrary",), vmem_limit_bytes=VMEM_LIMIT),
        name="odd_scan_rev",
    )(x, x, mod_l, g, w_in, conv_w, conv_b, w_a, w_i, b_a, b_i, lam, h0)


def _scan_forward(z, s_rev, he, x, mod_l, mod_row, w_in, w_a, w_i, b_a, b_i, lam, h0, w_out, final_g):
    bsz, L, _ = x.shape
    tile = T_SCAN
    n = L // tile
    fwd = lambda i: (0, i, 0)
    final_norm = final_g is not None
    fg = final_g if final_norm else jnp.ones((1, D_MODEL), F32)
    return pl.pallas_call(
        functools.partial(_scan_fwd_kernel, bsz=bsz, tile=tile, mod_row=mod_row, final_norm=final_norm),
        grid=(n,),
        in_specs=[
            pl.BlockSpec((tile * bsz, C_WIDTH), lambda i: (i, 0)),
            pl.BlockSpec((N_SLABS, tile * bsz, LANES), fwd),
            pl.BlockSpec((tile * bsz, D_MODEL), lambda i: (i, 0)),
            pl.BlockSpec((bsz, tile, D_MODEL), fwd),
            _const_spec((MOD_ROWS, 3 * D_MODEL)),
            pl.BlockSpec((D_MODEL, C_WIDTH), lambda i: (0, 1), pipeline_mode=pl.Buffered(1)),
        ] + _gate_specs() + [_const_spec((C_WIDTH, D_MODEL)), _const_spec((1, D_MODEL))],
        out_specs=[
            pl.BlockSpec((bsz, tile, D_MODEL), fwd),
            pl.BlockSpec((N_SLABS, SUBLANES, LANES), lambda i: (0, 0, 0)),
        ],
        out_shape=[
            jax.ShapeDtypeStruct(x.shape, F32),
            jax.ShapeDtypeStruct((N_SLABS, SUBLANES, LANES), F32),
        ],
        scratch_shapes=[
            pltpu.VMEM((N_SLABS, tile * bsz, LANES), F32),
            pltpu.VMEM((N_SLABS, tile * bsz, LANES), F32),
            pltpu.VMEM((N_SLABS, SUBLANES, LANES), F32),
            pltpu.VMEM((bsz * tile, C_WIDTH), BF16),
            pltpu.VMEM((bsz * tile, C_WIDTH), BF16),
        ],
        compiler_params=pltpu.CompilerParams(
            dimension_semantics=("arbitrary",), vmem_limit_bytes=VMEM_LIMIT),
        name="odd_scan_fwd",
    )(z, s_rev, he, x, mod_l, w_in, w_a, w_i, b_a, b_i, lam, h0, w_out, fg)


def _rope_tables(L):
    rows = L // GRID_W
    r, col = jnp.meshgrid(jnp.arange(rows), jnp.arange(GRID_W), indexing="ij")
    r = r.reshape(-1).astype(F32)
    col = col.reshape(-1).astype(F32)
    n_freq = HEAD_DIM // 4
    inv_freq = ROPE_THETA ** (-jnp.arange(n_freq, dtype=F32) / n_freq)
    ang = jnp.concatenate([r[:, None] * inv_freq, col[:, None] * inv_freq], axis=-1)
    cos, sin = jnp.cos(ang), jnp.sin(ang)
    cosf = jnp.tile(jnp.concatenate([cos, cos], axis=-1), (1, LANES // HEAD_DIM))
    sinf = jnp.tile(jnp.concatenate([-sin, sin], axis=-1), (1, LANES // HEAD_DIM))
    return cosf, sinf


def _dup_heads(w):
    return jnp.concatenate([w[:, :HEAD_DIM], w[:, :HEAD_DIM], w[:, HEAD_DIM:], w[:, HEAD_DIM:]], axis=1)


def kernel(x, c, ctx, c_ctx, norm_g, w_mod, b_mod, ab_w_in, a_ln_g, a_ln_b, a_w_s, a_b_s, b_sink,
           ab_w_out, c_w_in, c_conv_w, c_conv_b, c_w_a, c_b_a, c_w_i, c_b_i, c_lam, c_w_out, final_g):
    bsz, L, d = x.shape
    ctx_len = ctx.shape[1]
    assert d == D_MODEL and bsz == SUBLANES and L % T_SCAN == 0 and L % T_EVEN == 0
    assert ctx_len % BLOCK == 0 and ctx_len % T_SCAN == 0 and DEPTH % 2 == 0
    ctx_row = bsz

    cin = jnp.concatenate([c, c_ctx[None, :], jnp.zeros((MOD_ROWS - bsz - 1, d), F32)], axis=0)
    mod = _modulation(cin, w_mod, b_mod).reshape(DEPTH, MOD_ROWS, 1, 3 * d)
    cosf, sinf = _rope_tables(L)
    zero_state = jnp.zeros((N_SLABS, SUBLANES, LANES), F32)

    xc = ctx
    for layer in range(DEPTH):
        need_ctx = layer < DEPTH - 1
        i = layer // 2
        mod_l = mod[layer]
        g = norm_g[layer][None, :]
        if layer % 2 == 0:
            w_in = ab_w_in[i].astype(BF16)
            w_kv = jnp.concatenate([_dup_heads(w_in[:, AB_K_OFF:AB_K_OFF + KV_WIDTH]),
                                    _dup_heads(w_in[:, AB_K_OFF + KV_WIDTH:AB_GB_OFF])], axis=1)
            w_out = ab_w_out[i].astype(BF16)
            lng = a_ln_g[i][None, :]
            lnb = a_ln_b[i][None, :]
            ws = a_w_s[i].astype(BF16)
            bs = jnp.broadcast_to(a_b_s[i][:, :, None], (A_GROUPS, CHUNK, LANES))
            sink = b_sink[i]
            kc, vc = _ctx_kv_project(xc, mod_l, ctx_row, g, w_kv)
            x = _even_mix(x, mod_l, None, g, w_in, w_out, lng, lnb, ws, bs, sink, kc, vc,
                          cosf=cosf, sinf=sinf, w_kv=w_kv, tile=T_EVEN)
            if need_ctx:
                xc = _even_mix(xc, mod_l, ctx_row, g, w_in, w_out, lng, lnb, ws, bs, sink, kc, vc,
                               tile=ctx_len)
        else:
            w_in = c_w_in[i].astype(BF16)
            w_out = c_w_out[i].astype(BF16)
            conv_w, conv_b = 0.5 * c_conv_w[i], 0.5 * c_conv_b[i][None, :]
            mod_flat = mod_l.reshape(MOD_ROWS, 3 * d)
            gate_args = [(c_w_a[i, dd].astype(BF16), c_w_i[i, dd].astype(BF16),
                          0.5 * c_b_a[i, dd][None, :], 0.5 * c_b_i[i, dd][None, :], c_lam[i, dd][None, :])
                         for dd in range(2)]
            z_c, s_rev_c, he_c, h0_rev = _scan_reverse(xc, mod_flat, ctx_row, g, w_in, conv_w, conv_b,
                                                       *gate_args[1], zero_state)
            xc_new, h0_fwd = _scan_forward(z_c, s_rev_c, he_c, xc, mod_flat, ctx_row, w_in, *gate_args[0],
                                           zero_state, w_out, None)
            z, s_rev, he, _ = _scan_reverse(x, mod_flat, None, g, w_in, conv_w, conv_b, *gate_args[1], h0_rev)
            fin = final_g[None, :] if layer == DEPTH - 1 else None
            x, _ = _scan_forward(z, s_rev, he, x, mod_flat, None, w_in, *gate_args[0], h0_fwd, w_out, fin)
            if need_ctx:
                xc = xc_new
    return x
```

```python
import functools

import jax
import jax.numpy as jnp
from jax import lax
from jax.experimental import pallas as pl
from jax.experimental.pallas import tpu as pltpu

F32 = jnp.float32
BF16 = jnp.bfloat16

D_MODEL = 1024
DEPTH = 4
GRID_W = 64
EPS = 1e-6
CHUNK = 128
A_GROUPS = 4
A_WIDTH = 512
HEAD_DIM = 64
B_Q_HEADS = 8
B_KV_HEADS = 2
B_GQA = B_Q_HEADS // B_KV_HEADS
B_WIDTH = 512
KV_WIDTH = 128
BLOCK = 128
ROPE_THETA = 10000.0
AB_Q_OFF = 3 * A_WIDTH
AB_K_OFF = 3 * A_WIDTH + B_WIDTH
AB_GB_OFF = AB_K_OFF + 2 * KV_WIDTH
C_WIDTH = 1024
C_HEADS = 4
C_BLOCK = 256
CONV_W = 4
CONV_LEFT = 2
LRU_C = 8.0
LOG2_E = 1.4426950408889634
MASKED = -1e30

LANES = 128
SUBLANES = 8
N_SLABS = C_WIDTH // LANES
MOD_ROWS = 16
VMEM_LIMIT = 56 * 1024 * 1024
KV_DUP = 2 * KV_WIDTH

T_EVEN = 512
T_SCAN = 128
SCAN_CHUNK = 16


def _sigmoid(x):
    return 0.5 * jnp.tanh(0.5 * x) + 0.5


def _silu(x):
    return x * _sigmoid(x)


def _gelu(x):
    return 0.5 * x * (1.0 + jnp.tanh(0.7978845608028654 * (x + 0.044715 * (x * x * x))))


def _norm_mod(x, g1s, shift):
    ms = jnp.mean(x * x, axis=-1, keepdims=True)
    return x * lax.rsqrt(ms + EPS) * g1s + shift


def _mod_parts(m):
    return m[:, 0:D_MODEL], m[:, D_MODEL:2 * D_MODEL], m[:, 2 * D_MODEL:3 * D_MODEL]


def _rope(x, cosf, sinf):
    n = x.shape[-1]
    lane = lax.broadcasted_iota(jnp.int32, x.shape, 1)
    first = (lane % HEAD_DIM) < (HEAD_DIM // 2)
    sw = jnp.where(first, pltpu.roll(x, n - HEAD_DIM // 2, 1), pltpu.roll(x, HEAD_DIM // 2, 1))
    return x * cosf + sw * sinf


def _dot(a, b):
    return jnp.dot(a, b, preferred_element_type=F32)


def _dot_t(a, b):
    return lax.dot_general(a, b, (((1,), (1,)), ((), ())), preferred_element_type=F32)


def _mod_kernel(c_ref, w_ref, b_ref, o_ref):
    s = _silu(c_ref[...])
    o_ref[...] = _dot(s.astype(BF16), w_ref[...].astype(BF16)) + b_ref[...]


def _modulation(cin, w_mod, b_mod):
    nj = 3
    return pl.pallas_call(
        _mod_kernel,
        grid=(DEPTH, nj),
        in_specs=[
            pl.BlockSpec((MOD_ROWS, D_MODEL), lambda l, j: (0, 0)),
            pl.BlockSpec((None, D_MODEL, D_MODEL), lambda l, j: (l, 0, j)),
            pl.BlockSpec((None, 1, D_MODEL), lambda l, j: (l, 0, j)),
        ],
        out_specs=pl.BlockSpec((None, MOD_ROWS, D_MODEL), lambda l, j: (l, 0, j)),
        out_shape=jax.ShapeDtypeStruct((DEPTH, MOD_ROWS, 3 * D_MODEL), F32),
        compiler_params=pltpu.CompilerParams(
            dimension_semantics=("arbitrary", "arbitrary"), vmem_limit_bytes=VMEM_LIMIT),
        name="modulation",
    )(cin, w_mod, b_mod.reshape(DEPTH, 1, 3 * D_MODEL))


def _ctx_kv_kernel(x_ref, m_ref, g_ref, w_ref, k_ref, v_ref):
    shift, scale, _ = _mod_parts(m_ref[...])
    h = _norm_mod(x_ref[...], g_ref[...] * (1.0 + scale), shift).astype(BF16)
    kv = _dot(h, w_ref[...])
    k_ref[...] = kv[:, 0:KV_DUP].astype(BF16)
    v_ref[...] = kv[:, KV_DUP:2 * KV_DUP].astype(BF16)


def _ctx_kv_project(xc, mod_l, mod_row, g, w_kv):
    bsz, lc, _ = xc.shape
    return pl.pallas_call(
        _ctx_kv_kernel,
        grid=(bsz,),
        in_specs=[
            pl.BlockSpec((None, lc, D_MODEL), lambda b: (b, 0, 0)),
            pl.BlockSpec((None, 1, 3 * D_MODEL), lambda b: (mod_row, 0, 0)),
            pl.BlockSpec((1, D_MODEL), lambda b: (0, 0)),
            pl.BlockSpec((D_MODEL, 2 * KV_DUP), lambda b: (0, 0)),
        ],
        out_specs=[
            pl.BlockSpec((None, lc, KV_DUP), lambda b: (b, 0, 0)),
            pl.BlockSpec((None, lc, KV_DUP), lambda b: (b, 0, 0)),
        ],
        out_shape=[jax.ShapeDtypeStruct((bsz, lc, KV_DUP), BF16)] * 2,
        compiler_params=pltpu.CompilerParams(
            dimension_semantics=("parallel",), vmem_limit_bytes=VMEM_LIMIT),
        name="even_ctx_kv",
    )(xc, mod_l, g, w_kv)


def _even_kernel(*refs, tile, n_blocks_total, latent):
    if latent:
        (x_ref, m_ref, g_ref, win_ref, wout_ref, lng_ref, lnb_ref, ws_ref, bs_ref, sink_ref,
         xn_ref, wkv_ref, cos_ref, cosn_ref, sin_ref, sinn_ref,
         kctx_ref, vctx_ref, o_ref, ga_scr, gb_scr, vln_scr, q_scr, mix_scr, kbuf, vbuf) = refs
    else:
        (x_ref, m_ref, g_ref, win_ref, wout_ref, lng_ref, lnb_ref, ws_ref, bs_ref, sink_ref,
         kctx_ref, vctx_ref, o_ref, ga_scr, gb_scr, vln_scr, q_scr, mix_scr) = refs
    nb = tile // BLOCK
    i = pl.program_id(1)
    rows4 = B_GQA * BLOCK

    x = x_ref[...]
    shift, scale, gate = _mod_parts(m_ref[...])
    g1s = g_ref[...] * (1.0 + scale)
    if latent:
        @pl.when(i == 0)
        def _():
            kbuf[0:BLOCK, :] = jnp.zeros((BLOCK, KV_DUP), BF16)
            vbuf[0:BLOCK, :] = jnp.zeros((BLOCK, KV_DUP), BF16)

        @pl.when(i > 0)
        def _():
            kbuf[0:BLOCK, :] = kbuf[tile:tile + BLOCK, :]
            vbuf[0:BLOCK, :] = vbuf[tile:tile + BLOCK, :]

        he = _norm_mod(jnp.concatenate([x, xn_ref[...]], axis=0), g1s, shift).astype(BF16)
        h = he[0:tile]
        kv = _dot(he, wkv_ref[...])
        cos_e = jnp.concatenate([cos_ref[...], cosn_ref[...]], axis=0)
        sin_e = jnp.concatenate([sin_ref[...], sinn_ref[...]], axis=0)
        for s in range(KV_DUP // LANES):
            sl = slice(s * LANES, (s + 1) * LANES)
            kbuf[BLOCK:, sl] = _rope(kv[:, sl], cos_e, sin_e).astype(BF16)
        vbuf[BLOCK:, :] = kv[:, KV_DUP:2 * KV_DUP].astype(BF16)
    else:
        h = _norm_mod(x, g1s, shift).astype(BF16)

    v = _gelu(_dot(h, win_ref[:, A_WIDTH:2 * A_WIDTH]))
    for g in range(A_GROUPS):
        sl = slice(g * LANES, (g + 1) * LANES)
        vg = v[:, sl]
        mu = jnp.mean(vg, axis=-1, keepdims=True)
        vc = vg - mu
        var = jnp.mean(vc * vc, axis=-1, keepdims=True)
        vln_scr[:, sl] = (vc * lax.rsqrt(var + EPS) * lng_ref[:, sl] + lnb_ref[:, sl]).astype(BF16)

    def gate_inputs():
        gb_scr[...] = _silu(_dot(h, win_ref[:, AB_GB_OFF:AB_GB_OFF + B_WIDTH]))
        u = _dot(h, win_ref[:, 0:A_WIDTH])
        gate_a = _dot(h, win_ref[:, 2 * A_WIDTH:3 * A_WIDTH])
        ga_scr[...] = _gelu(u) * _silu(gate_a)

    q = _dot(h, win_ref[:, AB_Q_OFF:AB_Q_OFF + B_WIDTH])
    low_half = lax.broadcasted_iota(jnp.int32, (tile, LANES), 1) < HEAD_DIM
    for pair in range(B_Q_HEADS // 2):
        qs = q[:, pair * LANES:(pair + 1) * LANES]
        if latent:
            qs = _rope(qs, cos_ref[...], sin_ref[...])
        qs = qs * (HEAD_DIM ** -0.5 * LOG2_E)
        q_lo = jnp.where(low_half, qs, 0.0).astype(BF16)
        q_hi = jnp.where(low_half, 0.0, qs).astype(BF16)
        for jb in range(nb):
            rows = slice(jb * BLOCK, (jb + 1) * BLOCK)
            base = (jb * B_Q_HEADS + 2 * pair) * BLOCK
            q_scr[base:base + BLOCK, :] = q_lo[rows]
            q_scr[base + BLOCK:base + 2 * BLOCK, :] = q_hi[rows]

    if latent:
        qi = lax.broadcasted_iota(jnp.int32, (rows4, 3 * BLOCK), 0) & (BLOCK - 1)
        kj = lax.broadcasted_iota(jnp.int32, (rows4, 3 * BLOCK), 1)
        band = (kj >= qi) & (kj <= qi + 2 * BLOCK)
    out_low = lax.broadcasted_iota(jnp.int32, (BLOCK, LANES), 1) < HEAD_DIM

    def spatial_gate(jb, part):
        rows = slice(jb * BLOCK, (jb + 1) * BLOCK)
        for g in range(part * A_GROUPS // B_KV_HEADS, (part + 1) * A_GROUPS // B_KV_HEADS):
            sl = slice(g * LANES, (g + 1) * LANES)
            sv = _dot(ws_ref[g], vln_scr[rows, sl]) + bs_ref[g]
            mix_scr[rows, sl] = (ga_scr[rows, sl] * sv).astype(BF16)

    def scores(jb, kvh):
        ks = slice(kvh * LANES, (kvh + 1) * LANES)
        base = (jb * B_Q_HEADS + kvh * B_GQA) * BLOCK
        q4 = q_scr[base:base + rows4, :]
        sink = jnp.concatenate(
            [jnp.full((BLOCK, 1), sink_ref[kvh * B_GQA + j] * LOG2_E, F32) for j in range(B_GQA)], axis=0)
        s_c = _dot_t(q4, kctx_ref[:, ks])
        s_l = None
        if latent:
            mask = band
            if jb == 0:
                mask = mask & (kj >= jnp.where(i > 0, 0, BLOCK))
            if jb == nb - 1:
                mask = mask & (kj < jnp.where(i < n_blocks_total // nb - 1, 3 * BLOCK, 2 * BLOCK))
            s_l = jnp.where(mask, _dot_t(q4, kbuf[jb * BLOCK:(jb + 3) * BLOCK, ks]), MASKED)
        return sink, s_c, s_l

    def weighted_values(jb, kvh, sink, s_c, s_l):
        rows = slice(jb * BLOCK, (jb + 1) * BLOCK)
        ks = slice(kvh * LANES, (kvh + 1) * LANES)
        m = jnp.maximum(sink, jnp.max(s_c, axis=-1, keepdims=True))
        if latent:
            m = jnp.maximum(m, jnp.max(s_l, axis=-1, keepdims=True))
        p_c = jnp.exp2(s_c - m)
        den = jnp.exp2(sink - m) + jnp.sum(p_c, axis=-1, keepdims=True)
        o = _dot(p_c.astype(BF16), vctx_ref[:, ks])
        if latent:
            p_l = jnp.exp2(s_l - m)
            den = den + jnp.sum(p_l, axis=-1, keepdims=True)
            o = o + _dot(p_l.astype(BF16), vbuf[jb * BLOCK:(jb + 3) * BLOCK, ks])
        o = o * (1.0 / den)
        for hp in range(B_GQA // 2):
            pair_out = jnp.where(out_low, o[2 * hp * BLOCK:(2 * hp + 1) * BLOCK],
                                 o[(2 * hp + 1) * BLOCK:(2 * hp + 2) * BLOCK])
            col = (kvh * (B_GQA // 2) + hp) * LANES
            yb = pair_out * gb_scr[rows, col:col + LANES]
            mix_scr[rows, A_WIDTH + col:A_WIDTH + col + LANES] = yb.astype(BF16)

    for kvh in range(B_KV_HEADS):
        for jb in range(nb):
            st = scores(jb, kvh)
            if jb == 0 and kvh == 0:
                gate_inputs()
            spatial_gate(jb, kvh)
            weighted_values(jb, kvh, *st)

    y = _dot(mix_scr[...], wout_ref[...])
    o_ref[...] = x + gate * y


def _even_mix(x, mod_l, mod_row, g, w_in, w_out, lng, lnb, ws, bs, sink, kctx, vctx,
              cosf=None, sinf=None, w_kv=None, *, tile):
    latent = w_kv is not None
    bsz, L, _ = x.shape
    n = L // tile
    nb = tile // BLOCK
    nblk = L // BLOCK
    row = (lambda b: b) if mod_row is None else (lambda b: mod_row)
    full = lambda shape: pl.BlockSpec(shape, lambda b, i: (0,) * len(shape), pipeline_mode=pl.Buffered(1))
    in_specs = [
        pl.BlockSpec((None, tile, D_MODEL), lambda b, i: (b, i, 0)),
        pl.BlockSpec((None, 1, 3 * D_MODEL), lambda b, i: (row(b), 0, 0)),
        full((1, D_MODEL)),
        full(w_in.shape),
        full(w_out.shape),
        full((1, A_WIDTH)),
        full((1, A_WIDTH)),
        full(ws.shape),
        full(bs.shape),
        pl.BlockSpec(memory_space=pltpu.SMEM),
    ]
    args = [x, mod_l, g, w_in, w_out, lng, lnb, ws, bs, sink]
    scratch = [
        pltpu.VMEM((tile, A_WIDTH), F32),
        pltpu.VMEM((tile, B_WIDTH), F32),
        pltpu.VMEM((tile, A_WIDTH), BF16),
        pltpu.VMEM((nb * B_Q_HEADS * BLOCK, LANES), BF16),
        pltpu.VMEM((tile, A_WIDTH + B_WIDTH), BF16),
    ]
    if latent:
        next_blk = lambda i: jnp.minimum((i + 1) * nb, nblk - 1)
        table = [pl.BlockSpec((tile, LANES), lambda b, i: (i, 0)),
                 pl.BlockSpec((BLOCK, LANES), lambda b, i: (next_blk(i), 0))]
        in_specs += [
            pl.BlockSpec((None, BLOCK, D_MODEL), lambda b, i: (b, next_blk(i), 0)),
            full(w_kv.shape),
        ] + table + table
        args += [x, w_kv, cosf, cosf, sinf, sinf]
        scratch += [pltpu.VMEM((tile + 2 * BLOCK, KV_DUP), BF16)] * 2
    ctx_len = kctx.shape[1]
    in_specs += [pl.BlockSpec((None, ctx_len, KV_DUP), lambda b, i: (b, 0, 0))] * 2
    args += [kctx, vctx]
    return pl.pallas_call(
        functools.partial(_even_kernel, tile=tile, n_blocks_total=nblk, latent=latent),
        grid=(bsz, n),
        in_specs=in_specs,
        out_specs=pl.BlockSpec((None, tile, D_MODEL), lambda b, i: (b, i, 0)),
        out_shape=jax.ShapeDtypeStruct(x.shape, F32),
        scratch_shapes=scratch,
        compiler_params=pltpu.CompilerParams(
            dimension_semantics=("arbitrary", "arbitrary"), vmem_limit_bytes=VMEM_LIMIT),
        name="even_mix" if latent else "even_mix_ctx",
    )(*args)


def _mod_row_slices(m, g, bsz, mod_row):
    out = []
    for b in range(bsz):
        r = b if mod_row is None else mod_row
        out.append((g * (1.0 + m[r:r + 1, D_MODEL:2 * D_MODEL]), m[r:r + 1, 0:D_MODEL]))
    return out


def _half_decay_rate(lam_ref):
    neg = -lam_ref[...]
    return (0.5 * LRU_C) * (jnp.maximum(neg, 0.0) + jnp.log1p(jnp.exp(-jnp.abs(neg))))


def _gates(hz, hzb, hd, wa_ref, wi_ref, ba_ref, bi_ref, hsp, a_scr, bx_ref):
    cols = slice(hd * C_BLOCK, (hd + 1) * C_BLOCK)
    tr = jnp.tanh(_dot(hzb, wa_ref[hd]) + ba_ref[:, cols])
    ti = jnp.tanh(_dot(hzb, wi_ref[hd]) + bi_ref[:, cols])
    hs = hsp[:, cols]
    nla = tr * hs + hs
    a = jnp.exp2(nla * (-LOG2_E))
    y = jnp.tanh(nla) * (a * a + 1.0)
    root = jnp.where(y > 0.0, y * lax.rsqrt(y), 0.0)
    bx = root * (hz * ti + hz)
    for half in range(C_BLOCK // LANES):
        slab = hd * (C_BLOCK // LANES) + half
        ls = slice(half * LANES, (half + 1) * LANES)
        a_scr[slab] = a[:, ls]
        bx_ref[slab] = bx[:, ls]


def _scan_tile(a_scr, bx_scr, out_ref, h_scr, *, tile, reverse, add_ref=None):
    n_chunks = tile // SCAN_CHUNK
    chunk_rows = SCAN_CHUNK * SUBLANES

    def body(j, hs):
        base = pl.multiple_of(((n_chunks - 1 - j) if reverse else j) * chunk_rows, chunk_rows)
        hs = list(hs)
        for u in (reversed(range(SCAN_CHUNK)) if reverse else range(SCAN_CHUNK)):
            rows = pl.ds(base + u * SUBLANES, SUBLANES)
            for c in range(N_SLABS):
                hc = a_scr[c, rows, :] * hs[c] + bx_scr[c, rows, :]
                if add_ref is None:
                    out_ref[c, rows, :] = hc
                else:
                    out_ref[c, rows, :] = hc + add_ref[c, rows, :]
                hs[c] = hc
        return tuple(hs)

    hs = lax.fori_loop(0, n_chunks, body, tuple(h_scr[c] for c in range(N_SLABS)))
    for c in range(N_SLABS):
        h_scr[c] = hs[c]


def _scan_rev_kernel(xh_ref, x_ref, m_ref, g_ref, wx_ref, cw_ref, cb_ref, wa_ref, wi_ref, ba_ref, bi_ref,
                     lam_ref, h0_ref, z_ref, s_ref, he_ref, hfin_ref, ext_scr, a_scr, bx_scr, h_scr,
                     *, bsz, tile, n_tiles, mod_row):
    step = pl.program_id(0)
    tidx = n_tiles - 1 - step
    rows = tile * bsz
    left = CONV_LEFT * bsz
    right = (CONV_W - 1 - CONV_LEFT) * bsz

    @pl.when(step == 0)
    def _():
        h_scr[...] = h0_ref[...]
        ext_scr[:, left + rows:left + rows + right, :] = jnp.zeros((N_SLABS, right, LANES), F32)

    @pl.when(step > 0)
    def _():
        ext_scr[:, left + rows:left + rows + right, :] = ext_scr[:, left:left + right, :]

    mods = _mod_row_slices(m_ref[...], g_ref[...], bsz, mod_row)
    hs = [_norm_mod(x_ref[b], *mods[b]) for b in range(bsz)]
    for j in range(CONV_LEFT):
        r = SUBLANES - CONV_LEFT + j
        hs += [_norm_mod(xh_ref[b, r:r + 1, :], *mods[b]) for b in range(bsz)]
    he = jnp.concatenate(hs, axis=0).astype(BF16)
    he_ref[...] = he[0:rows]
    xr = _dot(he, wx_ref[...])
    for c in range(N_SLABS):
        ls = slice(c * LANES, (c + 1) * LANES)
        for b in range(bsz):
            ext_scr[c, pl.ds(left + b, tile, stride=bsz), :] = xr[b * tile:(b + 1) * tile, ls]
        ext_scr[c, 0:left, :] = jnp.where(tidx > 0, xr[rows:rows + left, ls], 0.0)

    hsp = _half_decay_rate(lam_ref)
    for hd in range(C_HEADS):
        zs = []
        for half in range(C_BLOCK // LANES):
            c = hd * (C_BLOCK // LANES) + half
            ls = slice(c * LANES, (c + 1) * LANES)
            ext = ext_scr[c]
            z = cb_ref[:, ls]
            for j in range(CONV_W):
                z = z + cw_ref[j:j + 1, ls] * ext[j * bsz:j * bsz + rows]
            zs.append(z)
        hz = jnp.concatenate(zs, axis=1)
        hzb = hz.astype(BF16)
        z_ref[:, hd * C_BLOCK:(hd + 1) * C_BLOCK] = hzb
        _gates(hz, hzb, hd, wa_ref, wi_ref, ba_ref, bi_ref, hsp, a_scr, bx_scr)
    _scan_tile(a_scr, bx_scr, s_ref, h_scr, tile=tile, reverse=True)
    hfin_ref[...] = h_scr[...]


def _scan_fwd_kernel(z_ref, srev_ref, he_ref, x_ref, m_ref, wg_ref, wa_ref, wi_ref, ba_ref, bi_ref, lam_ref,
                     h0_ref, wout_ref, fg_ref, o_ref, hfin_ref, a_scr, bx_scr, h_scr, mix_scr,
                     sg_scr, *, bsz, tile, mod_row, final_norm):
    @pl.when(pl.program_id(0) == 0)
    def _():
        h_scr[...] = h0_ref[...]

    m = m_ref[...]
    he = he_ref[...]

    hsp = _half_decay_rate(lam_ref)
    for hd in range(C_HEADS):
        cols = slice(hd * C_BLOCK, (hd + 1) * C_BLOCK)
        sg_scr[:, cols] = _silu(_dot(he, wg_ref[:, cols])).astype(BF16)
        hzb = z_ref[:, cols]
        _gates(hzb.astype(F32), hzb, hd, wa_ref, wi_ref, ba_ref, bi_ref, hsp, a_scr, bx_scr)
    _scan_tile(a_scr, bx_scr, a_scr, h_scr, tile=tile, reverse=False, add_ref=srev_ref)
    hfin_ref[...] = h_scr[...]

    for b in range(bsz):
        rs = slice(b * tile, (b + 1) * tile)
        for c in range(N_SLABS):
            ls = slice(c * LANES, (c + 1) * LANES)
            hsum = a_scr[c, pl.ds(b, tile, stride=bsz), :]
            mix_scr[rs, ls] = (hsum * sg_scr[rs, ls]).astype(BF16)
    y = _dot(mix_scr[...], wout_ref[...])
    for b in range(bsz):
        r = b if mod_row is None else mod_row
        gate = m[r:r + 1, 2 * D_MODEL:3 * D_MODEL]
        xn = x_ref[b] + gate * y[b * tile:(b + 1) * tile]
        if final_norm:
            ms = jnp.mean(xn * xn, axis=-1, keepdims=True)
            xn = xn * lax.rsqrt(ms + EPS) * fg_ref[...]
        o_ref[b] = xn


def _const_spec(shape):
    return pl.BlockSpec(shape, lambda i: (0,) * len(shape), pipeline_mode=pl.Buffered(1))


def _gate_specs():
    return [
        _const_spec((C_HEADS, C_BLOCK, C_BLOCK)),
        _const_spec((C_HEADS, C_BLOCK, C_BLOCK)),
        _const_spec((1, C_WIDTH)),
        _const_spec((1, C_WIDTH)),
        _const_spec((1, C_WIDTH)),
        _const_spec((N_SLABS, SUBLANES, LANES)),
    ]


def _scan_reverse(x, mod_l, mod_row, g, w_in, conv_w, conv_b, w_a, w_i, b_a, b_i, lam, h0):
    bsz, L, _ = x.shape
    tile = T_SCAN
    n = L // tile
    rev = lambda i: (0, n - 1 - i, 0)
    left = CONV_LEFT * bsz
    right = (CONV_W - 1 - CONV_LEFT) * bsz
    rows = tile * bsz
    return pl.pallas_call(
        functools.partial(_scan_rev_kernel, bsz=bsz, tile=tile, n_tiles=n, mod_row=mod_row),
        grid=(n,),
        in_specs=[
            pl.BlockSpec((bsz, SUBLANES, D_MODEL),
                         lambda i: (0, jnp.maximum((n - 1 - i) * (tile // SUBLANES) - 1, 0), 0)),
            pl.BlockSpec((bsz, tile, D_MODEL), rev),
            _const_spec((MOD_ROWS, 3 * D_MODEL)),
            _const_spec((1, D_MODEL)),
            pl.BlockSpec((D_MODEL, C_WIDTH), lambda i: (0, 0), pipeline_mode=pl.Buffered(1)),
            _const_spec((CONV_W, C_WIDTH)),
            _const_spec((1, C_WIDTH)),
        ] + _gate_specs(),
        out_specs=[
            pl.BlockSpec((rows, C_WIDTH), lambda i: (n - 1 - i, 0)),
            pl.BlockSpec((N_SLABS, rows, LANES), rev),
            pl.BlockSpec((rows, D_MODEL), lambda i: (n - 1 - i, 0)),
            pl.BlockSpec((N_SLABS, SUBLANES, LANES), lambda i: (0, 0, 0)),
        ],
        out_shape=[
            jax.ShapeDtypeStruct((L * bsz, C_WIDTH), BF16),
            jax.ShapeDtypeStruct((N_SLABS, L * bsz, LANES), F32),
            jax.ShapeDtypeStruct((L * bsz, D_MODEL), BF16),
            jax.ShapeDtypeStruct((N_SLABS, SUBLANES, LANES), F32),
        ],
        scratch_shapes=[
            pltpu.VMEM((N_SLABS, left + rows + right, LANES), F32),
            pltpu.VMEM((N_SLABS, rows, LANES), F32),
            pltpu.VMEM((N_SLABS, rows, LANES), F32),
            pltpu.VMEM((N_SLABS, SUBLANES, LANES), F32),
        ],
        compiler_params=pltpu.CompilerParams(
            dimension_semantics=("arbitrary",), vmem_limit_bytes=VMEM_LIMIT),
        name="odd_scan_rev",
    )(x, x, mod_l, g, w_in, conv_w, conv_b, w_a, w_i, b_a, b_i, lam, h0)


def _scan_forward(z, s_rev, he, x, mod_l, mod_row, w_in, w_a, w_i, b_a, b_i, lam, h0, w_out, final_g):
    bsz, L, _ = x.shape
    tile = T_SCAN
    n = L // tile
    fwd = lambda i: (0, i, 0)
    final_norm = final_g is not None
    fg = final_g if final_norm else jnp.ones((1, D_MODEL), F32)
    return pl.pallas_call(
        functools.partial(_scan_fwd_kernel, bsz=bsz, tile=tile, mod_row=mod_row, final_norm=final_norm),
        grid=(n,),
        in_specs=[
            pl.BlockSpec((tile * bsz, C_WIDTH), lambda i: (i, 0)),
            pl.BlockSpec((N_SLABS, tile * bsz, LANES), fwd),
            pl.BlockSpec((tile * bsz, D_MODEL), lambda i: (i, 0)),
            pl.BlockSpec((bsz, tile, D_MODEL), fwd),
            _const_spec((MOD_ROWS, 3 * D_MODEL)),
            pl.BlockSpec((D_MODEL, C_WIDTH), lambda i: (0, 1), pipeline_mode=pl.Buffered(1)),
        ] + _gate_specs() + [_const_spec((C_WIDTH, D_MODEL)), _const_spec((1, D_MODEL))],
        out_specs=[
            pl.BlockSpec((bsz, tile, D_MODEL), fwd),
            pl.BlockSpec((N_SLABS, SUBLANES, LANES), lambda i: (0, 0, 0)),
        ],
        out_shape=[
            jax.ShapeDtypeStruct(x.shape, F32),
            jax.ShapeDtypeStruct((N_SLABS, SUBLANES, LANES), F32),
        ],
        scratch_shapes=[
            pltpu.VMEM((N_SLABS, tile * bsz, LANES), F32),
            pltpu.VMEM((N_SLABS, tile * bsz, LANES), F32),
            pltpu.VMEM((N_SLABS, SUBLANES, LANES), F32),
            pltpu.VMEM((bsz * tile, C_WIDTH), BF16),
            pltpu.VMEM((bsz * tile, C_WIDTH), BF16),
        ],
        compiler_params=pltpu.CompilerParams(
            dimension_semantics=("arbitrary",), vmem_limit_bytes=VMEM_LIMIT),
        name="odd_scan_fwd",
    )(z, s_rev, he, x, mod_l, w_in, w_a, w_i, b_a, b_i, lam, h0, w_out, fg)


def _rope_tables(L):
    rows = L // GRID_W
    r, col = jnp.meshgrid(jnp.arange(rows), jnp.arange(GRID_W), indexing="ij")
    r = r.reshape(-1).astype(F32)
    col = col.reshape(-1).astype(F32)
    n_freq = HEAD_DIM // 4
    inv_freq = ROPE_THETA ** (-jnp.arange(n_freq, dtype=F32) / n_freq)
    ang = jnp.concatenate([r[:, None] * inv_freq, col[:, None] * inv_freq], axis=-1)
    cos, sin = jnp.cos(ang), jnp.sin(ang)
    cosf = jnp.tile(jnp.concatenate([cos, cos], axis=-1), (1, LANES // HEAD_DIM))
    sinf = jnp.tile(jnp.concatenate([-sin, sin], axis=-1), (1, LANES // HEAD_DIM))
    return cosf, sinf


def _dup_heads(w):
    return jnp.concatenate([w[:, :HEAD_DIM], w[:, :HEAD_DIM], w[:, HEAD_DIM:], w[:, HEAD_DIM:]], axis=1)


def kernel(x, c, ctx, c_ctx, norm_g, w_mod, b_mod, ab_w_in, a_ln_g, a_ln_b, a_w_s, a_b_s, b_sink,
           ab_w_out, c_w_in, c_conv_w, c_conv_b, c_w_a, c_b_a, c_w_i, c_b_i, c_lam, c_w_out, final_g):
    bsz, L, d = x.shape
    ctx_len = ctx.shape[1]
    assert d == D_MODEL and bsz == SUBLANES and L % T_SCAN == 0 and L % T_EVEN == 0
    assert ctx_len % BLOCK == 0 and ctx_len % T_SCAN == 0 and DEPTH % 2 == 0
    ctx_row = bsz

    cin = jnp.concatenate([c, c_ctx[None, :], jnp.zeros((MOD_ROWS - bsz - 1, d), F32)], axis=0)
    mod = _modulation(cin, w_mod, b_mod).reshape(DEPTH, MOD_ROWS, 1, 3 * d)
    cosf, sinf = _rope_tables(L)
    zero_state = jnp.zeros((N_SLABS, SUBLANES, LANES), F32)

    xc = ctx
    for layer in range(DEPTH):
        need_ctx = layer < DEPTH - 1
        i = layer // 2
        mod_l = mod[layer]
        g = norm_g[layer][None, :]
        if layer % 2 == 0:
            w_in = ab_w_in[i].astype(BF16)
            w_kv = jnp.concatenate([_dup_heads(w_in[:, AB_K_OFF:AB_K_OFF + KV_WIDTH]),
                                    _dup_heads(w_in[:, AB_K_OFF + KV_WIDTH:AB_GB_OFF])], axis=1)
            w_out = ab_w_out[i].astype(BF16)
            lng = a_ln_g[i][None, :]
            lnb = a_ln_b[i][None, :]
            ws = a_w_s[i].astype(BF16)
            bs = jnp.broadcast_to(a_b_s[i][:, :, None], (A_GROUPS, CHUNK, LANES))
            sink = b_sink[i]
            kc, vc = _ctx_kv_project(xc, mod_l, ctx_row, g, w_kv)
            x = _even_mix(x, mod_l, None, g, w_in, w_out, lng, lnb, ws, bs, sink, kc, vc,
                          cosf=cosf, sinf=sinf, w_kv=w_kv, tile=T_EVEN)
            if need_ctx:
                xc = _even_mix(xc, mod_l, ctx_row, g, w_in, w_out, lng, lnb, ws, bs, sink, kc, vc,
                               tile=ctx_len)
        else:
            w_in = c_w_in[i].astype(BF16)
            w_out = c_w_out[i].astype(BF16)
            conv_w, conv_b = 0.5 * c_conv_w[i], 0.5 * c_conv_b[i][None, :]
            mod_flat = mod_l.reshape(MOD_ROWS, 3 * d)
            gate_args = [(c_w_a[i, dd].astype(BF16), c_w_i[i, dd].astype(BF16),
                          0.5 * c_b_a[i, dd][None, :], 0.5 * c_b_i[i, dd][None, :], c_lam[i, dd][None, :])
                         for dd in range(2)]
            z_c, s_rev_c, he_c, h0_rev = _scan_reverse(xc, mod_flat, ctx_row, g, w_in, conv_w, conv_b,
                                                       *gate_args[1], zero_state)
            xc_new, h0_fwd = _scan_forward(z_c, s_rev_c, he_c, xc, mod_flat, ctx_row, w_in, *gate_args[0],
                                           zero_state, w_out, None)
            z, s_rev, he, _ = _scan_reverse(x, mod_flat, None, g, w_in, conv_w, conv_b, *gate_args[1], h0_rev)
            fin = final_g[None, :] if layer == DEPTH - 1 else None
            x, _ = _scan_forward(z, s_rev, he, x, mod_flat, None, w_in, *gate_args[0], h0_fwd, w_out, fin)
            if need_ctx:
                xc = xc_new
    return x
```
